```python
import math
import jax
import jax.numpy as jnp
from jax import lax
import numpy as np

D_MODEL = 2048
BATCH = 2
SEQ = 4096
DEPTH = 4

MIX_WIDTH = D_MODEL
FNET_GROUPS = 4
FNET_GW = D_MODEL // 16
FNET_W = FNET_GROUPS * FNET_GW
POOL_WINDOWS = (2, 4, 8, 16)
POOL_GROUPS = len(POOL_WINDOWS)
POOL_GW = D_MODEL // 16
POOL_W = POOL_GROUPS * POOL_GW
N_HEADS = 8
DK = D_MODEL // 32
DV = 2 * DK
ATT_W = N_HEADS * DV
QK_W = N_HEADS * 2 * DK
IN_W = FNET_W + POOL_W + 2 * QK_W + ATT_W
Q_BLOCK = 128
N_BUCKETS = 32
MAX_DISTANCE = 128
N_EXPERTS = 16
EC_CAPACITY = 2
D_FF = D_MODEL // 2
N_MOD = 6
EPS = 1e-6

kernel_name = "hybrid_fnet_pool_diffattn_ecmoe_encoder"


def _rmsnorm(x, g):
    xf = x.astype(jnp.float32)
    y = xf * lax.rsqrt(jnp.mean(xf * xf, axis=-1, keepdims=True) + EPS)
    return (y * g.astype(jnp.float32)).astype(x.dtype)


def _rel_bucket(rel):
    nb = N_BUCKETS // 2
    max_exact = nb // 2
    ret = (rel > 0).astype(jnp.int32) * nb
    n = jnp.abs(rel)
    nf = jnp.maximum(n, 1).astype(jnp.float32)
    large = max_exact + (jnp.log(nf / max_exact) / math.log(MAX_DISTANCE / max_exact)
                         * (nb - max_exact)).astype(jnp.int32)
    large = jnp.minimum(large, nb - 1)
    return ret + jnp.where(n < max_exact, n, large)


def _fourier_mixer(u, w):
    B, S, G, Cg = u.shape
    f = jnp.fft.fft2(u.astype(jnp.float32), axes=(1, 3), norm="ortho").real.astype(u.dtype)
    y = jnp.einsum('bsgc,gcd->bsgd', f, w)
    return y.reshape(B, S, G * Cg)


def _pool_mixer(u, w, scale):
    B, S, G, Cg = u.shape
    uf = u.astype(jnp.float32)
    cs = jnp.concatenate([jnp.zeros((B, 1, G, Cg), jnp.float32), lax.cumsum(uf, axis=1)], axis=1)
    i = jnp.arange(S)
    pooled = []
    for g, win in enumerate(POOL_WINDOWS):
        lo = jnp.maximum(i - win // 2, 0)
        hi = jnp.minimum(i + win // 2 - 1, S - 1)
        ssum = cs[:, hi + 1, g] - cs[:, lo, g]
        cnt = (hi - lo + 1).astype(jnp.float32)[None, :, None]
        pooled.append(ssum / cnt)
    d = (jnp.stack(pooled, axis=2) - uf).astype(u.dtype)
    y = jnp.einsum('bsgc,gcd->bsgd', d, w) * scale.reshape(G, Cg)
    return y.reshape(B, S, G * Cg)


def _diff_attention(uq, uk, uv, positions, rel_bias, lq1, lk1, lq2, lk2, sub_g, layer_idx):
    B, S, _ = uq.shape
    q = uq.reshape(B, S, N_HEADS, 2, DK)
    k = uk.reshape(B, S, N_HEADS, 2, DK)
    q1 = q[..., 0, :].transpose(0, 2, 1, 3)
    q2 = q[..., 1, :].transpose(0, 2, 1, 3)
    k1 = k[..., 0, :].transpose(0, 2, 1, 3)
    k2 = k[..., 1, :].transpose(0, 2, 1, 3)
    v = uv.reshape(B, S, N_HEADS, DV).transpose(0, 2, 1, 3)

    lam_init = 0.8 - 0.6 * math.exp(-0.3 * layer_idx)
    lam = (jnp.exp(jnp.sum(lq1.astype(jnp.float32) * lk1.astype(jnp.float32)))
           - jnp.exp(jnp.sum(lq2.astype(jnp.float32) * lk2.astype(jnp.float32))) + lam_init)
    scale = DK ** -0.5
    nblk = S // Q_BLOCK

    def block(args):
        q1b, q2b, pb = args
        bucket = _rel_bucket(positions[None, :] - pb[:, None])
        bias = jnp.transpose(rel_bias[bucket], (2, 0, 1)).astype(jnp.float32)
        s1 = jnp.einsum('bhqd,bhkd->bhqk', q1b, k1).astype(jnp.float32) * scale + bias
        s2 = jnp.einsum('bhqd,bhkd->bhqk', q2b, k2).astype(jnp.float32) * scale + bias
        p = jax.nn.softmax(s1, axis=-1) - lam * jax.nn.softmax(s2, axis=-1)
        return jnp.einsum('bhqk,bhkd->bhqd', p.astype(v.dtype), v)

    q1b = q1.reshape(B, N_HEADS, nblk, Q_BLOCK, DK).transpose(2, 0, 1, 3, 4)
    q2b = q2.reshape(B, N_HEADS, nblk, Q_BLOCK, DK).transpose(2, 0, 1, 3, 4)
    out = lax.map(block, (q1b, q2b, positions.reshape(nblk, Q_BLOCK)))
    out = out.transpose(1, 2, 0, 3, 4).reshape(B, N_HEADS, S, DV)
    out = _rmsnorm(out, sub_g) * (1.0 - lam_init)
    return out.transpose(0, 2, 1, 3).reshape(B, S, N_HEADS * DV)


def _ec_moe(h, w_r, w1, w3, w2):
    B, S, D = h.shape
    cap = EC_CAPACITY * S // N_EXPERTS
    aff = jax.nn.softmax((h @ w_r).astype(jnp.float32), axis=-1)
    gates, idx = lax.top_k(aff.transpose(0, 2, 1), cap)
    bidx = jnp.arange(B)[:, None, None]
    xg = h[bidx, idx]
    a = jnp.einsum('becd,edf->becf', xg, w1)
    g = jnp.einsum('becd,edf->becf', xg, w3)
    y = jnp.einsum('becf,efd->becd', jax.nn.silu(a) * g, w2) * gates[..., None].astype(h.dtype)
    return jnp.zeros_like(h).at[bidx, idx].add(y)


def setup_inputs(seed: int = 0) -> dict:
    key = jax.random.key(seed)
    ks = jax.random.split(key, 24)
    f32 = jnp.float32
    n = lambda k, shape, s: jax.random.normal(k, shape, f32) * s
    return {
        "x": n(ks[0], (BATCH, SEQ, D_MODEL), 1.0),
        "c": n(ks[1], (BATCH, D_MODEL), 1.0),
        "positions": jnp.arange(SEQ, dtype=jnp.int32),
        "w_mod": n(ks[2], (DEPTH, D_MODEL, N_MOD * D_MODEL), 0.5 * D_MODEL ** -0.5),
        "b_mod": n(ks[3], (DEPTH, N_MOD * D_MODEL), 0.02),
        "norm1_g": 1.0 + n(ks[4], (DEPTH, D_MODEL), 0.02),
        "norm2_g": 1.0 + n(ks[5], (DEPTH, D_MODEL), 0.02),
        "w_in": n(ks[6], (DEPTH, D_MODEL, IN_W), D_MODEL ** -0.5),
        "fnet_w": n(ks[7], (DEPTH, FNET_GROUPS, FNET_GW, FNET_GW), FNET_GW ** -0.5),
        "pool_w": n(ks[8], (DEPTH, POOL_GROUPS, POOL_GW, POOL_GW), POOL_GW ** -0.5),
        "pool_scale": 1.0 + n(ks[9], (DEPTH, POOL_W), 0.1),
        "lam_q1": n(ks[10], (DEPTH, DK), 0.1),
        "lam_k1": n(ks[11], (DEPTH, DK), 0.1),
        "lam_q2": n(ks[12], (DEPTH, DK), 0.1),
        "lam_k2": n(ks[13], (DEPTH, DK), 0.1),
        "sub_g": 1.0 + n(ks[14], (DEPTH, DV), 0.02),
        "rel_bias": n(ks[15], (N_BUCKETS, N_HEADS), 0.5),
        "w_out": n(ks[16], (DEPTH, MIX_WIDTH, D_MODEL), MIX_WIDTH ** -0.5),
        "w_router": n(ks[17], (DEPTH, D_MODEL, N_EXPERTS), D_MODEL ** -0.5),
        "w1": n(ks[18], (DEPTH, N_EXPERTS, D_MODEL, D_FF), D_MODEL ** -0.5),
        "w3": n(ks[19], (DEPTH, N_EXPERTS, D_MODEL, D_FF), D_MODEL ** -0.5),
        "w2": n(ks[20], (DEPTH, N_EXPERTS, D_FF, D_MODEL), D_FF ** -0.5),
        "final_g": 1.0 + n(ks[21], (D_MODEL,), 0.02),
    }


def reference(x, c, positions, w_mod, b_mod, norm1_g, norm2_g, w_in, fnet_w, pool_w,
              pool_scale, lam_q1, lam_k1, lam_q2, lam_k2, sub_g, rel_bias, w_out,
              w_router, w1, w3, w2, final_g):
    B, S, D = x.shape
    c_act = jax.nn.silu(c)
    splits = [FNET_W, FNET_W + POOL_W, FNET_W + POOL_W + QK_W, FNET_W + POOL_W + 2 * QK_W]
    for l in range(DEPTH):
        mod = c_act @ w_mod[l] + b_mod[l]
        sh1, sc1, g1, sh2, sc2, g2 = jnp.split(mod[:, None, :], N_MOD, axis=-1)

        h = _rmsnorm(x, norm1_g[l]) * (1.0 + sc1) + sh1
        u = h @ w_in[l]
        ua, ub, uq, uk, uv = jnp.split(u, splits, axis=-1)
        ya = _fourier_mixer(ua.reshape(B, S, FNET_GROUPS, FNET_GW), fnet_w[l])
        yb = _pool_mixer(ub.reshape(B, S, POOL_GROUPS, POOL_GW), pool_w[l], pool_scale[l])
        yc = _diff_attention(uq, uk, uv, positions, rel_bias, lam_q1[l], lam_k1[l],
                             lam_q2[l], lam_k2[l], sub_g[l], l)
        mix = jnp.concatenate([ya, yb, yc], axis=-1) @ w_out[l]
        x = x + g1 * mix

        h = _rmsnorm(x, norm2_g[l]) * (1.0 + sc2) + sh2
        x = x + g2 * _ec_moe(h, w_router[l], w1[l], w3[l], w2[l])
    return _rmsnorm(x, final_g)
```

```python
import functools
import math

import numpy as np
import jax
import jax.numpy as jnp
from jax import lax
from jax.experimental import pallas as pl
from jax.experimental.pallas import tpu as pltpu

F32 = jnp.float32
BF16 = jnp.bfloat16
I32 = jnp.int32

FNET_GROUPS = 4
POOL_GROUPS = 4
POOL_HALO = 64
N_HEADS = 8
DK = 64
DV = 128
N_BUCKETS = 32
MAX_DISTANCE = 128
N_EXPERTS = 16
EC_CAPACITY = 2
N_MOD = 6
EPS = 1e-6
LANES = 128


def _cparams(sem, vmem_mb=None):
    kw = dict(dimension_semantics=sem)
    if vmem_mb is not None:
        kw["vmem_limit_bytes"] = vmem_mb * 1024 * 1024
    return pltpu.CompilerParams(**kw)


def _silu(x):
    return x * jax.nn.sigmoid(x)


def _mod_kernel(c_ref, w_ref, b_ref, o_ref):
    ca = _silu(c_ref[...]).astype(BF16)
    o_ref[...] = jnp.dot(ca, w_ref[...].astype(BF16), preferred_element_type=F32) + b_ref[...]


def _modulation(c8, w_mod, b_mod3):
    depth, d, n = w_mod.shape
    tn = 1024
    return pl.pallas_call(
        _mod_kernel,
        grid=(depth, n // tn),
        in_specs=[
            pl.BlockSpec((8, d), lambda l, j: (0, 0)),
            pl.BlockSpec((None, d, tn), lambda l, j: (l, 0, j)),
            pl.BlockSpec((None, 1, tn), lambda l, j: (l, 0, j)),
        ],
        out_specs=pl.BlockSpec((None, 8, tn), lambda l, j: (l, 0, j)),
        out_shape=jax.ShapeDtypeStruct((depth, 8, n), F32),
        compiler_params=_cparams(("arbitrary", "arbitrary")),
        name="modulation",
    )(c8, w_mod, b_mod3)


def _norm_mod_kernel(x_ref, g_ref, sc_ref, sh_ref, o_ref):
    x = x_ref[...]
    ms = jnp.mean(x * x, axis=-1, keepdims=True)
    y = x * lax.rsqrt(ms + EPS) * g_ref[...]
    o_ref[...] = (y * (1.0 + sc_ref[...]) + sh_ref[...]).astype(o_ref.dtype)


def _norm_mod(x, g3, modr, l, sh_idx, sc_idx):
    b, s, d = x.shape
    tm = 512
    return pl.pallas_call(
        _norm_mod_kernel,
        grid=(b, s // tm),
        in_specs=[
            pl.BlockSpec((None, tm, d), lambda bi, i: (bi, i, 0)),
            pl.BlockSpec((None, 1, d), lambda bi, i: (l, 0, 0)),
            pl.BlockSpec((None, None, None, 1, d), lambda bi, i: (l, bi, sc_idx, 0, 0)),
            pl.BlockSpec((None, None, None, 1, d), lambda bi, i: (l, bi, sh_idx, 0, 0)),
        ],
        out_specs=pl.BlockSpec((None, tm, d), lambda bi, i: (bi, i, 0)),
        out_shape=jax.ShapeDtypeStruct((b, s, d), BF16),
        compiler_params=_cparams(("arbitrary", "arbitrary")),
        name="norm_mod",
    )(x, g3, modr, modr)


def _final_norm_kernel(x_ref, g_ref, o_ref):
    x = x_ref[...]
    ms = jnp.mean(x * x, axis=-1, keepdims=True)
    o_ref[...] = x * lax.rsqrt(ms + EPS) * g_ref[...]


def _final_norm(x, g2d):
    b, s, d = x.shape
    tm = 512
    return pl.pallas_call(
        _final_norm_kernel,
        grid=(b, s // tm),
        in_specs=[
            pl.BlockSpec((None, tm, d), lambda bi, i: (bi, i, 0)),
            pl.BlockSpec((1, d), lambda bi, i: (0, 0)),
        ],
        out_specs=pl.BlockSpec((None, tm, d), lambda bi, i: (bi, i, 0)),
        out_shape=jax.ShapeDtypeStruct((b, s, d), F32),
        compiler_params=_cparams(("arbitrary", "arbitrary")),
        name="final_norm",
    )(x, g2d)


def _mm_kernel(*refs, k_sizes):
    n_a = len(k_sizes)
    a_refs = refs[:n_a]
    w_ref, o_ref, wb_ref = refs[n_a], refs[n_a + 1], refs[n_a + 2]

    @pl.when(pl.program_id(1) == 0)
    def _():
        wb_ref[...] = w_ref[...].astype(BF16)

    acc = None
    off = 0
    for a_ref, ks in zip(a_refs, k_sizes):
        part = jnp.dot(a_ref[...], wb_ref[off:off + ks, :], preferred_element_type=F32)
        acc = part if acc is None else acc + part
        off += ks
    o_ref[...] = acc.astype(o_ref.dtype)


def _project(a_list, w, l, col0, ncols, out_dtype, tm=1024, tn=512, name="project"):
    m = a_list[0].shape[0]
    k_sizes = tuple(a.shape[1] for a in a_list)
    k = sum(k_sizes)
    assert w.shape[1] == k and col0 % tn == 0 and ncols % tn == 0 and m % tm == 0
    cb0 = col0 // tn
    in_specs = [pl.BlockSpec((tm, ks), lambda j, i: (i, 0)) for ks in k_sizes]
    in_specs.append(pl.BlockSpec((None, k, tn), lambda j, i: (l, 0, cb0 + j)))
    return pl.pallas_call(
        functools.partial(_mm_kernel, k_sizes=k_sizes),
        grid=(ncols // tn, m // tm),
        in_specs=in_specs,
        out_specs=pl.BlockSpec((tm, tn), lambda j, i: (i, j)),
        out_shape=jax.ShapeDtypeStruct((m, ncols), out_dtype),
        scratch_shapes=[pltpu.VMEM((k, tn), BF16)],
        compiler_params=_cparams(("arbitrary", "arbitrary"), 48),
        name=name,
    )(*a_list, w)


def _dft_tables(s):
    sp = np.arange(s, dtype=np.int64)
    a = np.arange(64, dtype=np.int64)[:, None]
    ang1 = 2.0 * np.pi * ((a * sp[None, :]) % 64) / 64.0
    ang2 = 2.0 * np.pi * ((a * sp[None, :]) % s) / float(s)
    t1c = np.cos(ang1).astype(np.float32).reshape(64, 1, s)
    t1s = np.sin(ang1).astype(np.float32).reshape(64, 1, s)
    t2c = np.cos(ang2).astype(np.float32)
    t2s = np.sin(ang2).astype(np.float32)
    return t1c, t1s, t2c, t2s


def _dftgen_kernel(t1c_ref, t1s_ref, t2c_ref, t2s_ref, o_ref):
    s = t2c_ref.shape[1]
    c1, s1 = t1c_ref[...], t1s_ref[...]
    c2, s2 = t2c_ref[...], t2s_ref[...]
    o_ref[:, :s] = (c1 * c2 - s1 * s2).astype(BF16)
    o_ref[:, s:] = (-(s1 * c2 + c1 * s2)).astype(BF16)


def _dft_matrix(s):
    assert s % 64 == 0 and s // 64 == 64
    t1c, t1s, t2c, t2s = _dft_tables(s)
    return pl.pallas_call(
        _dftgen_kernel,
        grid=(64,),
        in_specs=[
            pl.BlockSpec((None, 1, s), lambda a: (a, 0, 0)),
            pl.BlockSpec((None, 1, s), lambda a: (a, 0, 0)),
            pl.BlockSpec((64, s), lambda a: (0, 0)),
            pl.BlockSpec((64, s), lambda a: (0, 0)),
        ],
        out_specs=pl.BlockSpec((64, 2 * s), lambda a: (a, 0)),
        out_shape=jax.ShapeDtypeStruct((s, 2 * s), BF16),
        compiler_params=_cparams(("arbitrary",)),
        name="dft_matrix",
    )(t1c, t1s, t2c, t2s)


def _fnet_w_kernel(cc_ref, sc_ref, w_ref, o_ref, *, norm):
    depth, groups, cg, _ = w_ref.shape
    cc, sc = cc_ref[...], sc_ref[...]
    for l in range(depth):
        for g in range(groups):
            w = w_ref[l, g]
            a = jnp.dot(cc, w, preferred_element_type=F32, precision=lax.Precision.HIGHEST)
            b = jnp.dot(sc, w, preferred_element_type=F32, precision=lax.Precision.HIGHEST)
            o_ref[l, g, :, :cg] = (a * norm).astype(BF16)
            o_ref[l, g, :, cg:] = (b * norm).astype(BF16)


def _fnet_weights(fnet_w, s):
    depth, groups, cg, _ = fnet_w.shape
    idx = np.arange(cg, dtype=np.int64)
    ang = 2.0 * np.pi * ((idx[:, None] * idx[None, :]) % cg) / float(cg)
    cc = np.cos(ang).astype(np.float32)
    sc = np.sin(ang).astype(np.float32)
    norm = 1.0 / math.sqrt(float(s) * float(cg))
    return pl.pallas_call(
        functools.partial(_fnet_w_kernel, norm=norm),
        out_shape=jax.ShapeDtypeStruct((depth, groups, cg, 2 * cg), BF16),
        name="fnet_weights",
    )(cc, sc, fnet_w)


def _fnet_z_kernel(u_ref, ab_ref, za_ref, zb_ref):
    groups, cg = ab_ref.shape[0], ab_ref.shape[1]
    for g in range(groups):
        ug = u_ref[:, g * cg:(g + 1) * cg].astype(BF16)
        z = jnp.dot(ug, ab_ref[g], preferred_element_type=F32)
        za_ref[:, g * cg:(g + 1) * cg] = z[:, :cg].astype(BF16)
        zb_ref[:, g * cg:(g + 1) * cg] = z[:, cg:].astype(BF16)


def _fnet_z(u_ab, ab, l):
    b, s, _ = u_ab.shape
    groups, cg = ab.shape[1], ab.shape[2]
    fw = groups * cg
    ts = 512
    shp = jax.ShapeDtypeStruct((s, b * fw), BF16)
    return pl.pallas_call(
        _fnet_z_kernel,
        grid=(b, s // ts),
        in_specs=[
            pl.BlockSpec((None, ts, fw), lambda bi, i: (bi, i, 0)),
            pl.BlockSpec((None, groups, cg, 2 * cg), lambda bi, i: (l, 0, 0, 0)),
        ],
        out_specs=[pl.BlockSpec((ts, fw), lambda bi, i: (i, bi)),
                   pl.BlockSpec((ts, fw), lambda bi, i: (i, bi))],
        out_shape=[shp, shp],
        compiler_params=_cparams(("arbitrary", "arbitrary")),
        name="fnet_z",
    )(u_ab, ab)


def _dft_apply_kernel(wd_ref, za_ref, zb_ref, o_ref):
    s = za_ref.shape[0]
    acc = jnp.dot(wd_ref[:, :s], za_ref[...], preferred_element_type=F32)
    acc = acc + jnp.dot(wd_ref[:, s:], zb_ref[...], preferred_element_type=F32)
    o_ref[...] = acc.astype(o_ref.dtype)


def _dft_apply(wd, za, zb, b):
    s = wd.shape[0]
    fw = za.shape[1] // b
    tm = 512
    return pl.pallas_call(
        _dft_apply_kernel,
        grid=(b, s // tm),
        in_specs=[
            pl.BlockSpec((tm, 2 * s), lambda bi, i: (i, 0)),
            pl.BlockSpec((s, fw), lambda bi, i: (0, bi)),
            pl.BlockSpec((s, fw), lambda bi, i: (0, bi)),
        ],
        out_specs=pl.BlockSpec((None, tm, fw), lambda bi, i: (bi, i, 0)),
        out_shape=jax.ShapeDtypeStruct((b, s, fw), BF16),
        compiler_params=_cparams(("arbitrary", "arbitrary"), 48),
        name="dft_apply",
    )(wd, za, zb)


def _pool_kernel(u_ref, w_ref, sc_ref, o_ref, pad_ref):
    s, cg = u_ref.shape
    t = 256
    half = jnp.left_shift(jnp.int32(1), pl.program_id(1))
    pad_ref[0:POOL_HALO, :] = jnp.zeros((POOL_HALO, cg), F32)
    pad_ref[s + POOL_HALO:s + 2 * POOL_HALO, :] = jnp.zeros((POOL_HALO, cg), F32)
    pad_ref[POOL_HALO:s + POOL_HALO, :] = u_ref[...]
    ii = lax.broadcasted_iota(I32, (t, t + 2 * POOL_HALO), 0)
    jj = lax.broadcasted_iota(I32, (t, t + 2 * POOL_HALO), 1)
    dlt = jj - ii - POOL_HALO
    band = jnp.where(dlt >= -half, jnp.where(dlt <= half - 1, 1.0, 0.0), 0.0).astype(BF16)
    wb = w_ref[...].astype(BF16)
    scale = sc_ref[...]

    def body(ti, carry):
        r0 = pl.multiple_of(ti * t, t)
        seg = pad_ref[pl.ds(r0, t + 2 * POOL_HALO), :]
        hi = seg.astype(BF16)
        lo = (seg - hi.astype(F32)).astype(BF16)
        win = jnp.dot(band, hi, preferred_element_type=F32) + jnp.dot(band, lo, preferred_element_type=F32)
        gi = r0 + lax.broadcasted_iota(I32, (t, cg), 0)
        lo_i = jnp.maximum(gi - half, 0)
        hi_i = jnp.minimum(gi + half - 1, s - 1)
        cnt = (hi_i - lo_i + 1).astype(F32)
        dmean = win / cnt - seg[POOL_HALO:POOL_HALO + t, :]
        y = jnp.dot(dmean.astype(BF16), wb, preferred_element_type=F32) * scale
        o_ref[pl.ds(r0, t), :] = y.astype(o_ref.dtype)
        return carry

    lax.fori_loop(0, s // t, body, 0)


def _pool_mixer(u_ab, pool_w, pool_scale3, l, col_block0):
    b, s, _ = u_ab.shape
    groups, cg = pool_w.shape[1], pool_w.shape[2]
    return pl.pallas_call(
        _pool_kernel,
        grid=(b, groups),
        in_specs=[
            pl.BlockSpec((None, s, cg), lambda bi, g: (bi, 0, col_block0 + g)),
            pl.BlockSpec((None, None, cg, cg), lambda bi, g: (l, g, 0, 0)),
            pl.BlockSpec((None, 1, cg), lambda bi, g: (l, 0, g)),
        ],
        out_specs=pl.BlockSpec((None, s, cg), lambda bi, g: (bi, 0, g)),
        out_shape=jax.ShapeDtypeStruct((b, s, groups * cg), BF16),
        scratch_shapes=[pltpu.VMEM((s + 2 * POOL_HALO, cg), F32)],
        compiler_params=_cparams(("arbitrary", "arbitrary")),
        name="pool_mixer",
    )(u_ab, pool_w, pool_scale3)


def _bias_kernel(tab_ref, pq_ref, pk_ref, o_ref):
    nb = N_BUCKETS // 2
    max_exact = nb // 2
    rel = pk_ref[...] - pq_ref[...]
    n = jnp.abs(rel)
    nf = jnp.maximum(n, 1).astype(F32)
    large = max_exact + (jnp.log(nf / max_exact) / math.log(MAX_DISTANCE / max_exact)
                         * (nb - max_exact)).astype(I32)
    large = jnp.minimum(large, nb - 1)
    bucket = jnp.where(rel > 0, nb, 0) + jnp.where(n < max_exact, n, large)
    masks = [bucket == j for j in range(1, N_BUCKETS)]
    for h in range(N_HEADS):
        val = jnp.full(rel.shape, tab_ref[h], F32)
        for j in range(1, N_BUCKETS):
            val = jnp.where(masks[j - 1], tab_ref[j * N_HEADS + h], val)
        o_ref[h] = val


def _bias_table(rel_bias, positions):
    s = positions.shape[0]
    tq, tk = 128, 512
    pq = positions.reshape(s, 1)
    pk = positions.reshape(1, s)
    return pl.pallas_call(
        _bias_kernel,
        grid=(s // tq, s // tk),
        in_specs=[
            pl.BlockSpec(memory_space=pltpu.SMEM),
            pl.BlockSpec((tq, 1), lambda i, j: (i, 0)),
            pl.BlockSpec((1, tk), lambda i, j: (0, j)),
        ],
        out_specs=pl.BlockSpec((N_HEADS, tq, tk), lambda i, j: (0, i, j)),
        out_shape=jax.ShapeDtypeStruct((N_HEADS, s, s), F32),
        compiler_params=_cparams(("arbitrary", "arbitrary")),
        name="rel_bias_table",
    )(rel_bias.reshape(-1), pq, pk)


def _attn_kernel(lq1_ref, lk1_ref, lq2_ref, lk2_ref, q_ref, k_ref, v_ref, bias_ref, sg_ref, o_ref,
                 *, lam_init):
    lam = (jnp.exp(jnp.sum(lq1_ref[...] * lk1_ref[...], axis=-1, keepdims=True))
           - jnp.exp(jnp.sum(lq2_ref[...] * lk2_ref[...], axis=-1, keepdims=True)) + lam_init)
    q = q_ref[...] * (DK ** -0.5)
    lane = lax.broadcasted_iota(I32, q.shape, 1)
    zero = jnp.zeros_like(q)
    k = k_ref[...]
    v = v_ref[...]
    bias = bias_ref[...]

    def one_map(qm):
        sc = lax.dot_general(qm, k, (((1,), (1,)), ((), ())), preferred_element_type=F32) + bias
        m = jnp.max(sc, axis=-1, keepdims=True)
        e = jnp.exp(sc - m)
        den = jnp.sum(e, axis=-1, keepdims=True)
        o = jnp.dot(e.astype(BF16), v, preferred_element_type=F32)
        return o / den

    o1 = one_map(jnp.where(lane < DK, q, zero))
    o2 = one_map(jnp.where(lane >= DK, q, zero))
    o = o1 - lam * o2
    ms = jnp.mean(o * o, axis=-1, keepdims=True)
    y = o * lax.rsqrt(ms + EPS) * sg_ref[...]
    o_ref[...] = (y * (1.0 - lam_init)).astype(o_ref.dtype)


def _diff_attention(qkv, bias_tab, lam4, sub_g3, l):
    b, s, _ = qkv.shape
    tq = 128
    lam_init = 0.8 - 0.6 * math.exp(-0.3 * l)
    lam_specs = [pl.BlockSpec((None, 1, DK), lambda bi, h, i: (l, 0, 0)) for _ in range(4)]
    return pl.pallas_call(
        functools.partial(_attn_kernel, lam_init=lam_init),
        grid=(b, N_HEADS, s // tq),
        in_specs=lam_specs + [
            pl.BlockSpec((None, tq, 2 * DK), lambda bi, h, i: (bi, i, h)),
            pl.BlockSpec((None, s, 2 * DK), lambda bi, h, i: (bi, 0, N_HEADS + h)),
            pl.BlockSpec((None, s, DV), lambda bi, h, i: (bi, 0, 2 * N_HEADS + h)),
            pl.BlockSpec((None, tq, s), lambda bi, h, i: (h, i, 0)),
            pl.BlockSpec((None, 1, DV), lambda bi, h, i: (l, 0, 0)),
        ],
        out_specs=pl.BlockSpec((None, tq, DV), lambda bi, h, i: (bi, i, h)),
        out_shape=jax.ShapeDtypeStruct((b, s, N_HEADS * DV), BF16),
        compiler_params=_cparams(("arbitrary", "arbitrary", "arbitrary"), 48),
        name="diff_attention",
    )(*lam4, qkv, qkv, qkv, bias_tab, sub_g3)


def _post_mix_kernel(x_ref, mo_ref, g1_ref, ng_ref, sc_ref, sh_ref, wr_ref, x1_ref, h_ref, lg_ref):
    x1 = x_ref[...] + g1_ref[...] * mo_ref[...]
    x1_ref[...] = x1
    ms = jnp.mean(x1 * x1, axis=-1, keepdims=True)
    h = x1 * lax.rsqrt(ms + EPS) * ng_ref[...]
    h = h * (1.0 + sc_ref[...]) + sh_ref[...]
    h_ref[...] = h.astype(h_ref.dtype)
    lg_ref[...] = jnp.dot(h, wr_ref[...], preferred_element_type=F32, precision=lax.Precision.HIGHEST)


def _post_mix(x, mixo, modr, norm2_g3, w_router_p, l):
    b, s, d = x.shape
    tm = 256
    mspec = lambda idx: pl.BlockSpec((None, None, None, 1, d), lambda bi, i: (l, bi, idx, 0, 0))
    return pl.pallas_call(
        _post_mix_kernel,
        grid=(b, s // tm),
        in_specs=[
            pl.BlockSpec((None, tm, d), lambda bi, i: (bi, i, 0)),
            pl.BlockSpec((None, tm, d), lambda bi, i: (bi, i, 0)),
            mspec(2),
            pl.BlockSpec((None, 1, d), lambda bi, i: (l, 0, 0)),
            mspec(4),
            mspec(3),
            pl.BlockSpec((None, d, LANES), lambda bi, i: (l, 0, 0)),
        ],
        out_specs=[
            pl.BlockSpec((None, tm, d), lambda bi, i: (bi, i, 0)),
            pl.BlockSpec((None, tm, d), lambda bi, i: (bi, i, 0)),
            pl.BlockSpec((None, tm, LANES), lambda bi, i: (bi, i, 0)),
        ],
        out_shape=[
            jax.ShapeDtypeStruct((b, s, d), F32),
            jax.ShapeDtypeStruct((b, s, d), BF16),
            jax.ShapeDtypeStruct((b, s, LANES), F32),
        ],
        compiler_params=_cparams(("arbitrary", "arbitrary")),
        name="post_mix",
    )(x, mixo, modr, norm2_g3, modr, modr, w_router_p)


def _cumsum_lanes(x01):
    rows, n = x01.shape
    blk = 512
    ii = lax.broadcasted_iota(I32, (blk, blk), 0)
    jj = lax.broadcasted_iota(I32, (blk, blk), 1)
    tri = jnp.where(ii <= jj, 1.0, 0.0).astype(BF16)
    xb = x01.astype(BF16)
    carry = jnp.zeros((rows, 1), F32)
    outs = []
    for c in range(n // blk):
        part = jnp.dot(xb[:, c * blk:(c + 1) * blk], tri, preferred_element_type=F32) + carry
        outs.append(part)
        carry = part[:, blk - 1:blk]
    return jnp.concatenate(outs, axis=1)


def _route_kernel(lg_ref, slot_ref, slott_ref, afft_ref, *, cap):
    lg = lg_ref[...]
    lane = lax.broadcasted_iota(I32, lg.shape, 1)
    valid = lane < N_EXPERTS
    lgm = jnp.where(valid, lg, -1e30)
    m = jnp.max(lgm, axis=-1, keepdims=True)
    ex = jnp.where(valid, jnp.exp(lgm - m), 0.0)
    aff = ex / jnp.sum(ex, axis=-1, keepdims=True)
    afft_ref[...] = aff
    at = aff.T[:N_EXPERTS, :]
    keys = lax.bitcast_convert_type(at, I32)

    def body(i, prefix):
        cand = prefix | jnp.left_shift(jnp.int32(1), 30 - i)
        cnt = jnp.sum(jnp.where(keys >= cand, 1.0, 0.0), axis=1, keepdims=True)
        return jnp.where(cnt >= cap, cand, prefix)

    thr = lax.fori_loop(0, 31, body, jnp.zeros((N_EXPERTS, 1), I32))
    gt = jnp.where(keys > thr, 1.0, 0.0)
    eq = jnp.where(keys == thr, 1.0, 0.0)
    need = cap - jnp.sum(gt, axis=1, keepdims=True)
    take = eq * jnp.where(_cumsum_lanes(eq) <= need, 1.0, 0.0)
    sel = gt + take
    pos = _cumsum_lanes(sel) - 1.0
    slot = jnp.where(sel > 0.5, pos, -1.0).astype(I32)
    slot_ref[...] = slot
    pad = jnp.full((LANES - N_EXPERTS, slot.shape[1]), -1, I32)
    slott_ref[...] = jnp.concatenate([slot, pad], axis=0).T


def _route(logits, cap):
    b, s, _ = logits.shape
    return pl.pallas_call(
        functools.partial(_route_kernel, cap=cap),
        grid=(b,),
        in_specs=[pl.BlockSpec((None, s, LANES), lambda bi: (bi, 0, 0))],
        out_specs=[
            pl.BlockSpec((None, N_EXPERTS, s), lambda bi: (bi, 0, 0)),
            pl.BlockSpec((None, s, LANES), lambda bi: (bi, 0, 0)),
            pl.BlockSpec((None, s, LANES), lambda bi: (bi, 0, 0)),
        ],
        out_shape=[
            jax.ShapeDtypeStruct((b, N_EXPERTS, s), I32),
            jax.ShapeDtypeStruct((b, s, LANES), I32),
            jax.ShapeDtypeStruct((b, s, LANES), F32),
        ],
        compiler_params=_cparams(("arbitrary",), 48),
        name="route",
    )(logits)


def _gather_kernel(slot_ref, h_ref, xg_ref):
    cap = xg_ref.shape[0]
    s = h_ref.shape[0]
    slot = slot_ref[...]
    cc = 128
    for c in range(cap // cc):
        cidx = lax.broadcasted_iota(I32, (cc, s), 0) + c * cc
        onehot = jnp.where(cidx == slot, 1.0, 0.0).astype(BF16)
        xg_ref[c * cc:(c + 1) * cc, :] = jnp.dot(
            onehot, h_ref[...], preferred_element_type=F32).astype(xg_ref.dtype)


def _gather(slot4, h2, cap):
    b, s, d = h2.shape
    return pl.pallas_call(
        _gather_kernel,
        grid=(b, N_EXPERTS),
        in_specs=[
            pl.BlockSpec((None, None, 1, s), lambda bi, e: (bi, e, 0, 0)),
            pl.BlockSpec((None, s, d), lambda bi, e: (bi, 0, 0)),
        ],
        out_specs=pl.BlockSpec((None, None, cap, d), lambda bi, e: (bi, e, 0, 0)),
        out_shape=jax.ShapeDtypeStruct((b, N_EXPERTS, cap, d), BF16),
        compiler_params=_cparams(("arbitrary", "arbitrary"), 56),
        name="moe_gather",
    )(slot4, h2)


def _expert_kernel(xg_ref, w1_ref, w3_ref, w2_ref, y_ref, acc_ref):
    f = pl.program_id(1)
    nb, cap, d = xg_ref.shape
    xg = xg_ref[...].reshape(nb * cap, d)
    a = jnp.dot(xg, w1_ref[...].astype(BF16), preferred_element_type=F32)
    g = jnp.dot(xg, w3_ref[...].astype(BF16), preferred_element_type=F32)
    act = (_silu(a) * g).astype(BF16)
    part = jnp.dot(act, w2_ref[...].astype(BF16), preferred_element_type=F32)

    @pl.when(f == 0)
    def _():
        acc_ref[...] = part

    @pl.when(f > 0)
    def _():
        acc_ref[...] += part

    @pl.when(f == pl.num_programs(1) - 1)
    def _():
        y_ref[...] = acc_ref[...].reshape(nb, cap, d).astype(y_ref.dtype)


def _experts(xg, w1, w3, w2, l):
    b, e, cap, d = xg.shape
    ff = w1.shape[-1]
    fc = 256
    return pl.pallas_call(
        _expert_kernel,
        grid=(e, ff // fc),
        in_specs=[
            pl.BlockSpec((b, None, cap, d), lambda ei, f: (0, ei, 0, 0)),
            pl.BlockSpec((None, None, d, fc), lambda ei, f: (l, ei, 0, f)),
            pl.BlockSpec((None, None, d, fc), lambda ei, f: (l, ei, 0, f)),
            pl.BlockSpec((None, None, fc, d), lambda ei, f: (l, ei, f, 0)),
        ],
        out_specs=pl.BlockSpec((b, None, cap, d), lambda ei, f: (0, ei, 0, 0)),
        out_shape=jax.ShapeDtypeStruct((b, e, cap, d), BF16),
        scratch_shapes=[pltpu.VMEM((b * cap, d), F32)],
        compiler_params=_cparams(("arbitrary", "arbitrary"), 56),
        name="moe_experts",
    )(xg, w1, w3, w2)


def _scatter_kernel(slott_ref, afft_ref, y_ref, x1_ref, g2_ref, o_ref, acc_ref):
    e = pl.program_id(2)
    ts = slott_ref.shape[0]
    cap = y_ref.shape[0]
    lane = lax.broadcasted_iota(I32, (ts, LANES), 1)
    mine = lane == e
    col = jnp.sum(jnp.where(mine, slott_ref[...].astype(F32), 0.0), axis=1, keepdims=True).astype(I32)
    gate = jnp.sum(jnp.where(mine, afft_ref[...], 0.0), axis=1, keepdims=True)
    cidx = lax.broadcasted_iota(I32, (ts, cap), 1)
    onehot = jnp.where(cidx == col, 1.0, 0.0).astype(BF16)
    contrib = jnp.dot(onehot, y_ref[...], preferred_element_type=F32) * gate

    @pl.when(e == 0)
    def _():
        acc_ref[...] = contrib

    @pl.when(e > 0)
    def _():
        acc_ref[...] += contrib

    @pl.when(e == pl.num_programs(2) - 1)
    def _():
        o_ref[...] = x1_ref[...] + g2_ref[...] * acc_ref[...]


def _scatter(slott, afft, y, x1, modr, l):
    b, s, d = x1.shape
    cap = y.shape[2]
    ts = 512
    return pl.pallas_call(
        _scatter_kernel,
        grid=(b, s // ts, N_EXPERTS),
        in_specs=[
            pl.BlockSpec((None, ts, LANES), lambda bi, i, e: (bi, i, 0)),
            pl.BlockSpec((None, ts, LANES), lambda bi, i, e: (bi, i, 0)),
            pl.BlockSpec((None, None, cap, d), lambda bi, i, e: (bi, e, 0, 0)),
            pl.BlockSpec((None, ts, d), lambda bi, i, e: (bi, i, 0)),
            pl.BlockSpec((None, None, None, 1, d), lambda bi, i, e: (l, bi, 5, 0, 0)),
        ],
        out_specs=pl.BlockSpec((None, ts, d), lambda bi, i, e: (bi, i, 0)),
        out_shape=jax.ShapeDtypeStruct((b, s, d), F32),
        scratch_shapes=[pltpu.VMEM((ts, d), F32)],
        compiler_params=_cparams(("arbitrary", "arbitrary", "arbitrary"), 48),
        name="moe_scatter",
    )(slott, afft, y, x1, modr)


def kernel(x, c, positions, w_mod, b_mod, norm1_g, norm2_g, w_in, fnet_w, pool_w, pool_scale,
           lam_q1, lam_k1, lam_q2, lam_k2, sub_g, rel_bias, w_out, w_router, w1, w3, w2, final_g):
    b, s, d = x.shape
    depth = w_mod.shape[0]
    fnet_wd = fnet_w.shape[1] * fnet_w.shape[2]
    pool_wd = pool_w.shape[1] * pool_w.shape[2]
    ab_w = fnet_wd + pool_wd
    in_w = w_in.shape[-1]
    cap = EC_CAPACITY * s // N_EXPERTS

    c8 = jnp.zeros((8, d), F32).at[:b].set(c)
    mod = _modulation(c8, w_mod, b_mod.reshape(depth, 1, N_MOD * d))
    modr = mod[:, :b].reshape(depth, b, N_MOD, 1, d)

    norm1_g3 = norm1_g.reshape(depth, 1, d)
    norm2_g3 = norm2_g.reshape(depth, 1, d)
    pool_scale3 = pool_scale.reshape(depth, 1, pool_wd)
    sub_g3 = sub_g.reshape(depth, 1, DV)
    lam4 = [a.reshape(depth, 1, DK) for a in (lam_q1, lam_k1, lam_q2, lam_k2)]
    w_router_p = jnp.zeros((depth, d, LANES), F32).at[:, :, :N_EXPERTS].set(w_router)

    wd = _dft_matrix(s)
    ab = _fnet_weights(fnet_w, s)
    bias_tab = _bias_table(rel_bias, positions)

    for l in range(depth):
        h1 = _norm_mod(x, norm1_g3, modr, l, 0, 1).reshape(b * s, d)
        u_ab = _project([h1], w_in, l, 0, ab_w, F32, name="proj_in_ab").reshape(b, s, ab_w)
        qkv = _project([h1], w_in, l, ab_w, in_w - ab_w, BF16, name="proj_in_qkv").reshape(b, s, in_w - ab_w)

        za, zb = _fnet_z(u_ab, ab, l)
        ya = _dft_apply(wd, za, zb, b)
        yb = _pool_mixer(u_ab, pool_w, pool_scale3, l, fnet_wd // pool_w.shape[2])
        yc = _diff_attention(qkv, bias_tab, lam4, sub_g3, l)

        mixo = _project([ya.reshape(b * s, -1), yb.reshape(b * s, -1), yc.reshape(b * s, -1)],
                        w_out, l, 0, d, F32, name="proj_out").reshape(b, s, d)
        x1, h2, logits = _post_mix(x, mixo, modr, norm2_g3, w_router_p, l)

        slot, slott, afft = _route(logits, cap)
        xg = _gather(slot.reshape(b, N_EXPERTS, 1, s), h2, cap)
        y = _experts(xg, w1, w3, w2, l)
        x = _scatter(slott, afft, y, x1, modr, l)

    return _final_norm(x, final_g.reshape(1, d))
```

```python
import functools
import math

import numpy as np
import jax
import jax.numpy as jnp
from jax import lax
from jax.experimental import pallas as pl
from jax.experimental.pallas import tpu as pltpu

F32 = jnp.float32
BF16 = jnp.bfloat16
I32 = jnp.int32

FNET_GROUPS = 4
POOL_GROUPS = 4
POOL_HALO = 64
N_HEADS = 8
DK = 64
DV = 128
N_BUCKETS = 32
MAX_DISTANCE = 128
N_EXPERTS = 16
EC_CAPACITY = 2
N_MOD = 6
EPS = 1e-6
LANES = 128
LOG2E = 1.4426950408889634
ATT_TQ = 256
ATT_TK = 512
ATT_NS = 2


def _cparams(sem, vmem_mb=None):
    kw = dict(dimension_semantics=sem)
    if vmem_mb is not None:
        kw["vmem_limit_bytes"] = vmem_mb * 1024 * 1024
    return pltpu.CompilerParams(**kw)


def _silu(x):
    return x * jax.nn.sigmoid(x)


def _mod_kernel(c_ref, w_ref, b_ref, o_ref):
    ca = _silu(c_ref[...]).astype(BF16)
    o_ref[...] = jnp.dot(ca, w_ref[...].astype(BF16), preferred_element_type=F32) + b_ref[...]


def _modulation(c8, w_mod, b_mod3):
    depth, d, n = w_mod.shape
    tn = 1024
    return pl.pallas_call(
        _mod_kernel,
        grid=(depth, n // tn),
        in_specs=[
            pl.BlockSpec((8, d), lambda l, j: (0, 0)),
            pl.BlockSpec((None, d, tn), lambda l, j: (l, 0, j)),
            pl.BlockSpec((None, 1, tn), lambda l, j: (l, 0, j)),
        ],
        out_specs=pl.BlockSpec((None, 8, tn), lambda l, j: (l, 0, j)),
        out_shape=jax.ShapeDtypeStruct((depth, 8, n), F32),
        compiler_params=_cparams(("arbitrary", "arbitrary")),
        name="modulation",
    )(c8, w_mod, b_mod3)


def _norm_mod_kernel(x_ref, g_ref, sc_ref, sh_ref, o_ref):
    x = x_ref[...]
    ms = jnp.mean(x * x, axis=-1, keepdims=True)
    y = x * lax.rsqrt(ms + EPS) * g_ref[...]
    o_ref[...] = (y * (1.0 + sc_ref[...]) + sh_ref[...]).astype(o_ref.dtype)


def _norm_mod(x, g3, modr, l, sh_idx, sc_idx):
    b, s, d = x.shape
    tm = 512
    return pl.pallas_call(
        _norm_mod_kernel,
        grid=(b, s // tm),
        in_specs=[
            pl.BlockSpec((None, tm, d), lambda bi, i: (bi, i, 0)),
            pl.BlockSpec((None, 1, d), lambda bi, i: (l, 0, 0)),
            pl.BlockSpec((None, None, None, 1, d), lambda bi, i: (l, bi, sc_idx, 0, 0)),
            pl.BlockSpec((None, None, None, 1, d), lambda bi, i: (l, bi, sh_idx, 0, 0)),
        ],
        out_specs=pl.BlockSpec((None, tm, d), lambda bi, i: (bi, i, 0)),
        out_shape=jax.ShapeDtypeStruct((b, s, d), BF16),
        compiler_params=_cparams(("arbitrary", "arbitrary")),
        name="norm_mod",
    )(x, g3, modr, modr)


def _final_norm_kernel(x_ref, g_ref, o_ref):
    x = x_ref[...]
    ms = jnp.mean(x * x, axis=-1, keepdims=True)
    o_ref[...] = x * lax.rsqrt(ms + EPS) * g_ref[...]


def _final_norm(x, g2d):
    b, s, d = x.shape
    tm = 512
    return pl.pallas_call(
        _final_norm_kernel,
        grid=(b, s // tm),
        in_specs=[
            pl.BlockSpec((None, tm, d), lambda bi, i: (bi, i, 0)),
            pl.BlockSpec((1, d), lambda bi, i: (0, 0)),
        ],
        out_specs=pl.BlockSpec((None, tm, d), lambda bi, i: (bi, i, 0)),
        out_shape=jax.ShapeDtypeStruct((b, s, d), F32),
        compiler_params=_cparams(("arbitrary", "arbitrary")),
        name="final_norm",
    )(x, g2d)


def _mm_kernel(*refs, k_sizes, lead_blocks, lead_scale):
    n_a = len(k_sizes)
    a_refs = refs[:n_a]
    w_ref, o_ref, wb_ref = refs[n_a], refs[n_a + 1], refs[n_a + 2]

    @pl.when(pl.program_id(1) == 0)
    def _():
        wb_ref[...] = w_ref[...].astype(BF16)

    acc = None
    off = 0
    for a_ref, ks in zip(a_refs, k_sizes):
        part = jnp.dot(a_ref[...], wb_ref[off:off + ks, :], preferred_element_type=F32)
        acc = part if acc is None else acc + part
        off += ks
    if lead_blocks:
        acc = acc * jnp.where(pl.program_id(0) < lead_blocks, lead_scale, 1.0)
    o_ref[...] = acc.astype(o_ref.dtype)


def _project(a_list, w, l, col0, ncols, out_dtype, tm=1024, tn=512, lead_cols=0, lead_scale=1.0,
             name="project"):
    m = a_list[0].shape[0]
    k_sizes = tuple(a.shape[1] for a in a_list)
    k = sum(k_sizes)
    assert w.shape[1] == k and col0 % tn == 0 and ncols % tn == 0 and m % tm == 0 and lead_cols % tn == 0
    cb0 = col0 // tn
    in_specs = [pl.BlockSpec((tm, ks), lambda j, i: (i, 0)) for ks in k_sizes]
    in_specs.append(pl.BlockSpec((None, k, tn), lambda j, i: (l, 0, cb0 + j)))
    return pl.pallas_call(
        functools.partial(_mm_kernel, k_sizes=k_sizes, lead_blocks=lead_cols // tn, lead_scale=lead_scale),
        grid=(ncols // tn, m // tm),
        in_specs=in_specs,
        out_specs=pl.BlockSpec((tm, tn), lambda j, i: (i, j)),
        out_shape=jax.ShapeDtypeStruct((m, ncols), out_dtype),
        scratch_shapes=[pltpu.VMEM((k, tn), BF16)],
        compiler_params=_cparams(("arbitrary", "arbitrary"), 48),
        name=name,
    )(*a_list, w)


def _dft_tables(s):
    sp = np.arange(s, dtype=np.int64)
    a = np.arange(64, dtype=np.int64)[:, None]
    ang1 = 2.0 * np.pi * ((a * sp[None, :]) % 64) / 64.0
    ang2 = 2.0 * np.pi * ((a * sp[None, :]) % s) / float(s)
    t1c = np.cos(ang1).astype(np.float32).reshape(64, 1, s)
    t1s = np.sin(ang1).astype(np.float32).reshape(64, 1, s)
    t2c = np.cos(ang2).astype(np.float32)
    t2s = np.sin(ang2).astype(np.float32)
    return t1c, t1s, t2c, t2s


def _dftgen_kernel(t1c_ref, t1s_ref, t2c_ref, t2s_ref, o_ref):
    s = t2c_ref.shape[1]
    c1, s1 = t1c_ref[...], t1s_ref[...]
    c2, s2 = t2c_ref[...], t2s_ref[...]
    o_ref[:, :s] = (c1 * c2 - s1 * s2).astype(BF16)
    o_ref[:, s:] = (-(s1 * c2 + c1 * s2)).astype(BF16)


def _dft_matrix(s):
    assert s % 64 == 0 and s // 64 == 64
    t1c, t1s, t2c, t2s = _dft_tables(s)
    return pl.pallas_call(
        _dftgen_kernel,
        grid=(64,),
        in_specs=[
            pl.BlockSpec((None, 1, s), lambda a: (a, 0, 0)),
            pl.BlockSpec((None, 1, s), lambda a: (a, 0, 0)),
            pl.BlockSpec((64, s), lambda a: (0, 0)),
            pl.BlockSpec((64, s), lambda a: (0, 0)),
        ],
        out_specs=pl.BlockSpec((64, 2 * s), lambda a: (a, 0)),
        out_shape=jax.ShapeDtypeStruct((s, 2 * s), BF16),
        compiler_params=_cparams(("arbitrary",)),
        name="dft_matrix",
    )(t1c, t1s, t2c, t2s)


def _fnet_w_kernel(cc_ref, sc_ref, w_ref, o_ref, *, norm):
    depth, groups, cg, _ = w_ref.shape
    cc, sc = cc_ref[...], sc_ref[...]
    for l in range(depth):
        for g in range(groups):
            w = w_ref[l, g]
            a = jnp.dot(cc, w, preferred_element_type=F32, precision=lax.Precision.HIGHEST)
            b = jnp.dot(sc, w, preferred_element_type=F32, precision=lax.Precision.HIGHEST)
            o_ref[l, g, :, :cg] = (a * norm).astype(BF16)
            o_ref[l, g, :, cg:] = (b * norm).astype(BF16)


def _fnet_weights(fnet_w, s):
    depth, groups, cg, _ = fnet_w.shape
    idx = np.arange(cg, dtype=np.int64)
    ang = 2.0 * np.pi * ((idx[:, None] * idx[None, :]) % cg) / float(cg)
    cc = np.cos(ang).astype(np.float32)
    sc = np.sin(ang).astype(np.float32)
    norm = 1.0 / math.sqrt(float(s) * float(cg))
    return pl.pallas_call(
        functools.partial(_fnet_w_kernel, norm=norm),
        out_shape=jax.ShapeDtypeStruct((depth, groups, cg, 2 * cg), BF16),
        name="fnet_weights",
    )(cc, sc, fnet_w)


def _fnet_z_kernel(u_ref, ab_ref, za_ref, zb_ref):
    groups, cg = ab_ref.shape[0], ab_ref.shape[1]
    for g in range(groups):
        ug = u_ref[:, g * cg:(g + 1) * cg].astype(BF16)
        z = jnp.dot(ug, ab_ref[g], preferred_element_type=F32)
        za_ref[:, g * cg:(g + 1) * cg] = z[:, :cg].astype(BF16)
        zb_ref[:, g * cg:(g + 1) * cg] = z[:, cg:].astype(BF16)


def _fnet_z(u_ab, ab, l):
    b, s, _ = u_ab.shape
    groups, cg = ab.shape[1], ab.shape[2]
    fw = groups * cg
    ts = 512
    shp = jax.ShapeDtypeStruct((s, b * fw), BF16)
    return pl.pallas_call(
        _fnet_z_kernel,
        grid=(b, s // ts),
        in_specs=[
            pl.BlockSpec((None, ts, fw), lambda bi, i: (bi, i, 0)),
            pl.BlockSpec((None, groups, cg, 2 * cg), lambda bi, i: (l, 0, 0, 0)),
        ],
        out_specs=[pl.BlockSpec((ts, fw), lambda bi, i: (i, bi)),
                   pl.BlockSpec((ts, fw), lambda bi, i: (i, bi))],
        out_shape=[shp, shp],
        compiler_params=_cparams(("arbitrary", "arbitrary")),
        name="fnet_z",
    )(u_ab, ab)


def _dft_apply_kernel(wd_ref, za_ref, zb_ref, o_ref):
    s = za_ref.shape[0]
    acc = jnp.dot(wd_ref[:, :s], za_ref[...], preferred_element_type=F32)
    acc = acc + jnp.dot(wd_ref[:, s:], zb_ref[...], preferred_element_type=F32)
    o_ref[...] = acc.astype(o_ref.dtype)


def _dft_apply(wd, za, zb, b):
    s = wd.shape[0]
    fw = za.shape[1] // b
    tm = 512
    return pl.pallas_call(
        _dft_apply_kernel,
        grid=(b, s // tm),
        in_specs=[
            pl.BlockSpec((tm, 2 * s), lambda bi, i: (i, 0)),
            pl.BlockSpec((s, fw), lambda bi, i: (0, bi)),
            pl.BlockSpec((s, fw), lambda bi, i: (0, bi)),
        ],
        out_specs=pl.BlockSpec((None, tm, fw), lambda bi, i: (bi, i, 0)),
        out_shape=jax.ShapeDtypeStruct((b, s, fw), BF16),
        compiler_params=_cparams(("arbitrary", "arbitrary"), 48),
        name="dft_apply",
    )(wd, za, zb)


def _pool_kernel(u_ref, w_ref, sc_ref, o_ref, pad_ref):
    s, cg = u_ref.shape
    t = 256
    half = jnp.left_shift(jnp.int32(1), pl.program_id(1))
    pad_ref[0:POOL_HALO, :] = jnp.zeros((POOL_HALO, cg), F32)
    pad_ref[s + POOL_HALO:s + 2 * POOL_HALO, :] = jnp.zeros((POOL_HALO, cg), F32)
    pad_ref[POOL_HALO:s + POOL_HALO, :] = u_ref[...]
    ii = lax.broadcasted_iota(I32, (t, t + 2 * POOL_HALO), 0)
    jj = lax.broadcasted_iota(I32, (t, t + 2 * POOL_HALO), 1)
    dlt = jj - ii - POOL_HALO
    band = jnp.where(dlt >= -half, jnp.where(dlt <= half - 1, 1.0, 0.0), 0.0).astype(BF16)
    wb = w_ref[...].astype(BF16)
    scale = sc_ref[...]

    def body(ti, carry):
        r0 = pl.multiple_of(ti * t, t)
        seg = pad_ref[pl.ds(r0, t + 2 * POOL_HALO), :]
        hi = seg.astype(BF16)
        lo = (seg - hi.astype(F32)).astype(BF16)
        win = jnp.dot(band, hi, preferred_element_type=F32) + jnp.dot(band, lo, preferred_element_type=F32)
        gi = r0 + lax.broadcasted_iota(I32, (t, cg), 0)
        lo_i = jnp.maximum(gi - half, 0)
        hi_i = jnp.minimum(gi + half - 1, s - 1)
        cnt = (hi_i - lo_i + 1).astype(F32)
        dmean = win / cnt - seg[POOL_HALO:POOL_HALO + t, :]
        y = jnp.dot(dmean.astype(BF16), wb, preferred_element_type=F32) * scale
        o_ref[pl.ds(r0, t), :] = y.astype(o_ref.dtype)
        return carry

    lax.fori_loop(0, s // t, body, 0)


def _pool_mixer(u_ab, pool_w, pool_scale3, l, col_block0):
    b, s, _ = u_ab.shape
    groups, cg = pool_w.shape[1], pool_w.shape[2]
    return pl.pallas_call(
        _pool_kernel,
        grid=(b, groups),
        in_specs=[
            pl.BlockSpec((None, s, cg), lambda bi, g: (bi, 0, col_block0 + g)),
            pl.BlockSpec((None, None, cg, cg), lambda bi, g: (l, g, 0, 0)),
            pl.BlockSpec((None, 1, cg), lambda bi, g: (l, 0, g)),
        ],
        out_specs=pl.BlockSpec((None, s, cg), lambda bi, g: (bi, 0, g)),
        out_shape=jax.ShapeDtypeStruct((b, s, groups * cg), BF16),
        scratch_shapes=[pltpu.VMEM((s + 2 * POOL_HALO, cg), F32)],
        compiler_params=_cparams(("arbitrary", "arbitrary")),
        name="pool_mixer",
    )(u_ab, pool_w, pool_scale3)


def _bias_tile(tab_ref, h, rel):
    nb = N_BUCKETS // 2
    max_exact = nb // 2
    n = jnp.abs(rel)
    nf = jnp.maximum(n, 1).astype(F32)
    large = max_exact + (jnp.log(nf / max_exact) / math.log(MAX_DISTANCE / max_exact)
                         * (nb - max_exact)).astype(I32)
    large = jnp.minimum(large, nb - 1)
    bucket = jnp.where(rel > 0, nb, 0) + jnp.where(n < max_exact, n, large)
    masks = [bucket == j for j in range(1, N_BUCKETS)]

    def one_head(hh):
        val = jnp.full(rel.shape, tab_ref[hh], F32)
        for j in range(1, N_BUCKETS):
            val = jnp.where(masks[j - 1], tab_ref[j * N_HEADS + hh], val)
        return val * LOG2E

    if h is None:
        return [one_head(hh) for hh in range(N_HEADS)]
    return one_head(h)


def _near_bias_kernel(slotj_ref, nnear_ref, tab_ref, pq_ref, pk_ref, o_ref):
    del slotj_ref
    used = jnp.where(pl.program_id(1) < nnear_ref[pl.program_id(0)], 1.0, 0.0)
    tiles = _bias_tile(tab_ref, None, pk_ref[...] - pq_ref[...])
    for h in range(N_HEADS):
        o_ref[h] = tiles[h] * used


def _near_bias_table(rel_bias, positions, slotj, nnear):
    s = positions.shape[0]
    nq = s // ATT_TQ
    grid_spec = pltpu.PrefetchScalarGridSpec(
        num_scalar_prefetch=2,
        grid=(nq, ATT_NS),
        in_specs=[
            pl.BlockSpec(memory_space=pltpu.SMEM),
            pl.BlockSpec((ATT_TQ, 1), lambda i, n, sj, nn: (i, 0)),
            pl.BlockSpec((1, ATT_TK), lambda i, n, sj, nn: (0, sj[i * ATT_NS + n])),
        ],
        out_specs=pl.BlockSpec((N_HEADS, None, None, ATT_TQ, ATT_TK), lambda i, n, sj, nn: (0, i, n, 0, 0)),
    )
    return pl.pallas_call(
        _near_bias_kernel,
        grid_spec=grid_spec,
        out_shape=jax.ShapeDtypeStruct((N_HEADS, nq, ATT_NS, ATT_TQ, ATT_TK), F32),
        compiler_params=_cparams(("arbitrary", "arbitrary")),
        name="near_bias_table",
    )(slotj, nnear, rel_bias.reshape(-1), positions.reshape(s, 1), positions.reshape(1, s))


def _attn_plan(positions, rel_bias):
    s = positions.shape[0]
    nq, nk = s // ATT_TQ, s // ATT_TK
    pq = positions.reshape(nq, ATT_TQ)
    pk = positions.reshape(nk, ATT_TK)
    rel_min = pk.min(axis=1)[None, :] - pq.max(axis=1)[:, None]
    rel_max = pk.max(axis=1)[None, :] - pq.min(axis=1)[:, None]
    cls = jnp.where(rel_min >= MAX_DISTANCE, 1, jnp.where(rel_max <= -MAX_DISTANCE, 0, 2)).astype(I32)
    near = cls == 2
    nnear = near.sum(axis=1).astype(I32)
    slotj = jnp.argsort(jnp.logical_not(near), axis=1, stable=True)[:, :ATT_NS].astype(I32)
    fits = jnp.all(nnear <= ATT_NS)
    nb = N_BUCKETS // 2
    ctab = jnp.stack([rel_bias[nb - 1], rel_bias[2 * nb - 1], jnp.zeros((N_HEADS,), F32)], axis=1) * LOG2E
    return cls.reshape(-1), slotj.reshape(-1), nnear, ctab.reshape(-1).astype(F32), fits


def _lambda(lq1_ref, lk1_ref, lq2_ref, lk2_ref, lam_init):
    return (jnp.exp(jnp.sum(lq1_ref[...] * lk1_ref[...], axis=-1, keepdims=True))
            - jnp.exp(jnp.sum(lq2_ref[...] * lk2_ref[...], axis=-1, keepdims=True)) + lam_init)


def _stack_maps(q):
    lane = lax.broadcasted_iota(I32, q.shape, 1)
    zero = jnp.zeros_like(q)
    return jnp.concatenate([jnp.where(lane < DK, q, zero), jnp.where(lane >= DK, q, zero)], axis=0)


def _attn_finish(o1, o2, lam, sg, lam_init, dtype):
    o = o1 - lam * o2
    ms = jnp.mean(o * o, axis=-1, keepdims=True)
    y = o * lax.rsqrt(ms + EPS) * sg
    return (y * (1.0 - lam_init)).astype(dtype)


def _attn_kernel(cls_ref, slotj_ref, ctab_ref, lq1_ref, lk1_ref, lq2_ref, lk2_ref,
                 q_ref, k_ref, v_ref, nbp0_ref, nbp1_ref, nba0_ref, nba1_ref, nbb0_ref, nbb1_ref,
                 sg_ref, o_ref, sa_scr, sb_scr, vaug_scr, *, lam_init):
    h = pl.program_id(1)
    ip = pl.program_id(2)
    tq = ATT_TQ
    s = k_ref.shape[0]
    nk = s // ATT_TK
    nq = s // tq
    lam = _lambda(lq1_ref, lk1_ref, lq2_ref, lk2_ref, lam_init)
    sg = sg_ref[...]

    def scores(t, s_scr, nb_refs):
        qs = _stack_maps(q_ref[pl.ds(pl.multiple_of(t * tq, tq), tq), :])
        for j in range(nk):
            cval = ctab_ref[h * 3 + cls_ref[t * nk + j]]
            sc = lax.dot_general(qs, k_ref[j * ATT_TK:(j + 1) * ATT_TK, :], (((1,), (1,)), ((), ())),
                                 preferred_element_type=F32)
            s_scr[j] = sc + cval
        for n, nb_ref in enumerate(nb_refs):
            j = slotj_ref[t * ATT_NS + n]
            bias = nb_ref[...]
            s_scr[j, 0:tq, :] = s_scr[j, 0:tq, :] + bias
            s_scr[j, tq:2 * tq, :] = s_scr[j, tq:2 * tq, :] + bias

    def outputs(s_scr, row0):
        m128 = None
        for j in range(nk):
            blk = s_scr[j]
            for c in range(ATT_TK // LANES):
                part = blk[:, c * LANES:(c + 1) * LANES]
                m128 = part if m128 is None else jnp.maximum(m128, part)
        m = jnp.max(m128, axis=-1, keepdims=True)
        acc = None
        for j in range(nk):
            e = jnp.exp2(s_scr[j] - m).astype(BF16)
            part = jnp.dot(e, vaug_scr[j * ATT_TK:(j + 1) * ATT_TK, :], preferred_element_type=F32)
            acc = part if acc is None else acc + part
        o1 = acc[:tq, :DV] / acc[:tq, DV:DV + 1]
        o2 = acc[tq:, :DV] / acc[tq:, DV:DV + 1]
        o_ref[row0:row0 + tq, :] = _attn_finish(o1, o2, lam, sg, lam_init, o_ref.dtype)

    @pl.when(ip == 0)
    def _():
        vaug_scr[:, :DV] = v_ref[...]
        lane = lax.broadcasted_iota(I32, (s, DV), 1)
        vaug_scr[:, DV:] = jnp.where(lane == 0, 1.0, 0.0).astype(BF16)
        scores(0, sa_scr, (nbp0_ref, nbp1_ref))

    t_odd = 2 * ip + 1
    t_next = jnp.minimum(2 * ip + 2, nq - 1)
    scores(t_odd, sb_scr, (nbb0_ref, nbb1_ref))
    outputs(sa_scr, 0)
    scores(t_next, sa_scr, (nba0_ref, nba1_ref))
    outputs(sb_scr, tq)


def _attn_fast(qkv, nbt, plan, lam4, sub_g3, l):
    b, s, _ = qkv.shape
    cls, slotj, _, ctab, _ = plan
    nq = s // ATT_TQ
    assert nq % 2 == 0 and ATT_NS == 2
    lam_init = 0.8 - 0.6 * math.exp(-0.3 * l)
    smem = pl.BlockSpec(memory_space=pltpu.SMEM)
    lam_specs = [pl.BlockSpec((None, 1, DK), lambda bi, h, ip: (l, 0, 0)) for _ in range(4)]
    nb_tile = (None, None, None, ATT_TQ, ATT_TK)
    nb_specs = ([pl.BlockSpec(nb_tile, lambda bi, h, ip, n=n: (h, 0, n, 0, 0)) for n in range(ATT_NS)]
                + [pl.BlockSpec(nb_tile, lambda bi, h, ip, n=n: (h, jnp.minimum(2 * ip + 2, nq - 1), n, 0, 0))
                   for n in range(ATT_NS)]
                + [pl.BlockSpec(nb_tile, lambda bi, h, ip, n=n: (h, 2 * ip + 1, n, 0, 0)) for n in range(ATT_NS)])
    return pl.pallas_call(
        functools.partial(_attn_kernel, lam_init=lam_init),
        grid=(b, N_HEADS, nq // 2),
        in_specs=[smem, smem, smem] + lam_specs + [
            pl.BlockSpec((None, s, 2 * DK), lambda bi, h, ip: (bi, 0, h)),
            pl.BlockSpec((None, s, 2 * DK), lambda bi, h, ip: (bi, 0, N_HEADS + h)),
            pl.BlockSpec((None, s, DV), lambda bi, h, ip: (bi, 0, 2 * N_HEADS + h)),
        ] + nb_specs + [pl.BlockSpec((None, 1, DV), lambda bi, h, ip: (l, 0, 0))],
        out_specs=pl.BlockSpec((None, 2 * ATT_TQ, DV), lambda bi, h, ip: (bi, ip, h)),
        out_shape=jax.ShapeDtypeStruct((b, s, N_HEADS * DV), BF16),
        scratch_shapes=[pltpu.VMEM((s // ATT_TK, 2 * ATT_TQ, ATT_TK), F32),
                        pltpu.VMEM((s // ATT_TK, 2 * ATT_TQ, ATT_TK), F32),
                        pltpu.VMEM((s, 2 * DV), BF16)],
        compiler_params=_cparams(("arbitrary", "arbitrary", "arbitrary"), 56),
        name="diff_attention",
    )(cls, slotj, ctab, *lam4, qkv, qkv, qkv, nbt, nbt, nbt, nbt, nbt, nbt, sub_g3)


def _attn_any_kernel(tab_ref, lq1_ref, lk1_ref, lq2_ref, lk2_ref, q_ref, k_ref, v_ref, pq_ref, pk_ref,
                     sg_ref, o_ref, *, lam_init):
    h = pl.program_id(1)
    tq = q_ref.shape[0]
    bias = _bias_tile(tab_ref, h, pk_ref[...] - pq_ref[...])
    sc = lax.dot_general(_stack_maps(q_ref[...]), k_ref[...], (((1,), (1,)), ((), ())),
                         preferred_element_type=F32)
    v = v_ref[...]

    def one_map(sm):
        sm = sm + bias
        e = jnp.exp2(sm - jnp.max(sm, axis=-1, keepdims=True))
        den = jnp.sum(e, axis=-1, keepdims=True)
        return jnp.dot(e.astype(BF16), v, preferred_element_type=F32) / den

    lam = _lambda(lq1_ref, lk1_ref, lq2_ref, lk2_ref, lam_init)
    o_ref[...] = _attn_finish(one_map(sc[:tq]), one_map(sc[tq:]), lam, sg_ref[...], lam_init, o_ref.dtype)


def _attn_any(qkv, rel_bias, positions, lam4, sub_g3, l):
    b, s, _ = qkv.shape
    tq = 128
    lam_init = 0.8 - 0.6 * math.exp(-0.3 * l)
    lam_specs = [pl.BlockSpec((None, 1, DK), lambda bi, h, i: (l, 0, 0)) for _ in range(4)]
    return pl.pallas_call(
        functools.partial(_attn_any_kernel, lam_init=lam_init),
        grid=(b, N_HEADS, s // tq),
        in_specs=[pl.BlockSpec(memory_space=pltpu.SMEM)] + lam_specs + [
            pl.BlockSpec((None, tq, 2 * DK), lambda bi, h, i: (bi, i, h)),
            pl.BlockSpec((None, s, 2 * DK), lambda bi, h, i: (bi, 0, N_HEADS + h)),
            pl.BlockSpec((None, s, DV), lambda bi, h, i: (bi, 0, 2 * N_HEADS + h)),
            pl.BlockSpec((tq, 1), lambda bi, h, i: (i, 0)),
            pl.BlockSpec((1, s), lambda bi, h, i: (0, 0)),
            pl.BlockSpec((None, 1, DV), lambda bi, h, i: (l, 0, 0)),
        ],
        out_specs=pl.BlockSpec((None, tq, DV), lambda bi, h, i: (bi, i, h)),
        out_shape=jax.ShapeDtypeStruct((b, s, N_HEADS * DV), BF16),
        compiler_params=_cparams(("arbitrary", "arbitrary", "arbitrary"), 48),
        name="diff_attention_any",
    )(rel_bias.reshape(-1), *lam4, qkv, qkv, qkv, positions.reshape(s, 1), positions.reshape(1, s), sub_g3)


def _diff_attention(qkv, nbt, plan, rel_bias, positions, lam4, sub_g3, l):
    return lax.cond(plan[4],
                    lambda: _attn_fast(qkv, nbt, plan, lam4, sub_g3, l),
                    lambda: _attn_any(qkv, rel_bias, positions, lam4, sub_g3, l))


def _post_mix_kernel(x_ref, mo_ref, g1_ref, ng_ref, sc_ref, sh_ref, wr_ref, x1_ref, h_ref, lg_ref):
    x1 = x_ref[...] + g1_ref[...] * mo_ref[...]
    x1_ref[...] = x1
    ms = jnp.mean(x1 * x1, axis=-1, keepdims=True)
    h = x1 * lax.rsqrt(ms + EPS) * ng_ref[...]
    h = h * (1.0 + sc_ref[...]) + sh_ref[...]
    h_ref[...] = h.astype(h_ref.dtype)
    lg_ref[...] = jnp.dot(h, wr_ref[...], preferred_element_type=F32, precision=lax.Precision.HIGHEST)


def _post_mix(x, mixo, modr, norm2_g3, w_router_p, l):
    b, s, d = x.shape
    tm = 256
    mspec = lambda idx: pl.BlockSpec((None, None, None, 1, d), lambda bi, i: (l, bi, idx, 0, 0))
    return pl.pallas_call(
        _post_mix_kernel,
        grid=(b, s // tm),
        in_specs=[
            pl.BlockSpec((None, tm, d), lambda bi, i: (bi, i, 0)),
            pl.BlockSpec((None, tm, d), lambda bi, i: (bi, i, 0)),
            mspec(2),
            pl.BlockSpec((None, 1, d), lambda bi, i: (l, 0, 0)),
            mspec(4),
            mspec(3),
            pl.BlockSpec((None, d, LANES), lambda bi, i: (l, 0, 0)),
        ],
        out_specs=[
            pl.BlockSpec((None, tm, d), lambda bi, i: (bi, i, 0)),
            pl.BlockSpec((None, tm, d), lambda bi, i: (bi, i, 0)),
            pl.BlockSpec((None, tm, LANES), lambda bi, i: (bi, i, 0)),
        ],
        out_shape=[
            jax.ShapeDtypeStruct((b, s, d), F32),
            jax.ShapeDtypeStruct((b, s, d), F32),
            jax.ShapeDtypeStruct((b, s, LANES), F32),
        ],
        compiler_params=_cparams(("arbitrary", "arbitrary")),
        name="post_mix",
    )(x, mixo, modr, norm2_g3, modr, modr, w_router_p)


def _cumsum_lanes(x01):
    rows, n = x01.shape
    blk = 512
    ii = lax.broadcasted_iota(I32, (blk, blk), 0)
    jj = lax.broadcasted_iota(I32, (blk, blk), 1)
    tri = jnp.where(ii <= jj, 1.0, 0.0).astype(BF16)
    xb = x01.astype(BF16)
    carry = jnp.zeros((rows, 1), F32)
    outs = []
    for c in range(n // blk):
        part = jnp.dot(xb[:, c * blk:(c + 1) * blk], tri, preferred_element_type=F32) + carry
        outs.append(part)
        carry = part[:, blk - 1:blk]
    return jnp.concatenate(outs, axis=1)


def _route_kernel(lg_ref, idx_ref, slott_ref, afft_ref, cs_scr, *, cap):
    lg = lg_ref[...]
    lane = lax.broadcasted_iota(I32, lg.shape, 1)
    valid = lane < N_EXPERTS
    lgm = jnp.where(valid, lg, -1e30)
    m = jnp.max(lgm, axis=-1, keepdims=True)
    ex = jnp.where(valid, jnp.exp(lgm - m), 0.0)
    aff = ex / jnp.sum(ex, axis=-1, keepdims=True)
    afft_ref[...] = aff
    at = aff.T[:N_EXPERTS, :]
    keys = lax.bitcast_convert_type(at, I32)

    def body(i, prefix):
        cand = prefix | jnp.left_shift(jnp.int32(1), 30 - i)
        cnt = jnp.sum(jnp.where(keys >= cand, 1.0, 0.0), axis=1, keepdims=True)
        return jnp.where(cnt >= cap, cand, prefix)

    thr = lax.fori_loop(0, 31, body, jnp.zeros((N_EXPERTS, 1), I32))
    gt = jnp.where(keys > thr, 1.0, 0.0)
    eq = jnp.where(keys == thr, 1.0, 0.0)
    need = cap - jnp.sum(gt, axis=1, keepdims=True)
    take = eq * jnp.where(_cumsum_lanes(eq) <= need, 1.0, 0.0)
    sel = gt + take
    cs = _cumsum_lanes(sel)
    slot = jnp.where(sel > 0.5, cs - 1.0, -1.0).astype(I32)
    s = slot.shape[1]
    pad = jnp.full((LANES - N_EXPERTS, s), -1, I32)
    slott_ref[...] = jnp.concatenate([slot, pad], axis=0).T

    cs_scr[...] = cs
    lane = lax.broadcasted_iota(I32, (cap, LANES), 1)
    rows = 128

    def one_expert(e, idxt):
        row = cs_scr[pl.ds(e, 1), :]
        cols = []
        for c0 in range(0, cap, rows):
            cio = (lax.broadcasted_iota(I32, (rows, s), 0) + c0).astype(F32)
            cols.append(jnp.sum(jnp.where(row <= cio, 1.0, 0.0), axis=1, keepdims=True))
        return jnp.where(lane == e, jnp.concatenate(cols, axis=0), idxt)

    idxt = lax.fori_loop(0, N_EXPERTS, one_expert, jnp.zeros((cap, LANES), F32))
    idx_ref[...] = idxt.T[:N_EXPERTS, :].astype(I32)


def _route(logits, cap):
    b, s, _ = logits.shape
    return pl.pallas_call(
        functools.partial(_route_kernel, cap=cap),
        grid=(b,),
        in_specs=[pl.BlockSpec((None, s, LANES), lambda bi: (bi, 0, 0))],
        out_specs=[
            pl.BlockSpec((None, N_EXPERTS, cap), lambda bi: (bi, 0, 0)),
            pl.BlockSpec((None, s, LANES), lambda bi: (bi, 0, 0)),
            pl.BlockSpec((None, s, LANES), lambda bi: (bi, 0, 0)),
        ],
        out_shape=[
            jax.ShapeDtypeStruct((b, N_EXPERTS, cap), I32),
            jax.ShapeDtypeStruct((b, s, LANES), I32),
            jax.ShapeDtypeStruct((b, s, LANES), F32),
        ],
        scratch_shapes=[pltpu.VMEM((N_EXPERTS, s), F32)],
        compiler_params=_cparams(("arbitrary",), 48),
        name="route",
    )(logits)


def _gather_kernel(idx_ref, h_ref, xg_ref, sem):
    bi = pl.program_id(0)
    e = pl.program_id(1)
    cap = xg_ref.shape[2]
    base = (bi * N_EXPERTS + e) * cap

    def row_copy(c, t):
        return pltpu.make_async_copy(h_ref.at[bi, pl.ds(t, 1), :], xg_ref.at[bi, e, pl.ds(c, 1), :], sem)

    def issue(c, carry):
        row_copy(c, idx_ref[base + c]).start()
        return carry

    def drain(c, carry):
        row_copy(c, 0).wait()
        return carry

    lax.fori_loop(0, cap, issue, 0)
    lax.fori_loop(0, cap, drain, 0)


def _gather(idx, h2, cap):
    b, s, d = h2.shape
    grid_spec = pltpu.PrefetchScalarGridSpec(
        num_scalar_prefetch=1,
        grid=(b, N_EXPERTS),
        in_specs=[pl.BlockSpec(memory_space=pl.ANY)],
        out_specs=pl.BlockSpec(memory_space=pl.ANY),
        scratch_shapes=[pltpu.SemaphoreType.DMA],
    )
    return pl.pallas_call(
        _gather_kernel,
        grid_spec=grid_spec,
        out_shape=jax.ShapeDtypeStruct((b, N_EXPERTS, cap, d), h2.dtype),
        compiler_params=_cparams(("arbitrary", "arbitrary")),
        name="moe_gather",
    )(idx.reshape(-1), h2)


def _expert_kernel(xg_ref, w1_ref, w3_ref, w2_ref, y_ref, acc_ref, xb_ref):
    f = pl.program_id(1)
    nb, cap, d = xg_ref.shape

    @pl.when(f == 0)
    def _():
        xb_ref[...] = xg_ref[...].reshape(nb * cap, d).astype(BF16)

    xg = xb_ref[...]
    a = jnp.dot(xg, w1_ref[...].astype(BF16), preferred_element_type=F32)
    g = jnp.dot(xg, w3_ref[...].astype(BF16), preferred_element_type=F32)
    act = (_silu(a) * g).astype(BF16)
    part = jnp.dot(act, w2_ref[...].astype(BF16), preferred_element_type=F32)

    @pl.when(f == 0)
    def _():
        acc_ref[...] = part

    @pl.when(f > 0)
    def _():
        acc_ref[...] += part

    @pl.when(f == pl.num_programs(1) - 1)
    def _():
        y_ref[...] = acc_ref[...].reshape(nb, cap, d).astype(y_ref.dtype)


def _experts(xg, w1, w3, w2, l):
    b, e, cap, d = xg.shape
    ff = w1.shape[-1]
    fc = 256
    return pl.pallas_call(
        _expert_kernel,
        grid=(e, ff // fc),
        in_specs=[
            pl.BlockSpec((b, None, cap, d), lambda ei, f: (0, ei, 0, 0)),
            pl.BlockSpec((None, None, d, fc), lambda ei, f: (l, ei, 0, f)),
            pl.BlockSpec((None, None, d, fc), lambda ei, f: (l, ei, 0, f)),
            pl.BlockSpec((None, None, fc, d), lambda ei, f: (l, ei, f, 0)),
        ],
        out_specs=pl.BlockSpec((b, None, cap, d), lambda ei, f: (0, ei, 0, 0)),
        out_shape=jax.ShapeDtypeStruct((b, e, cap, d), BF16),
        scratch_shapes=[pltpu.VMEM((b * cap, d), F32), pltpu.VMEM((b * cap, d), BF16)],
        compiler_params=_cparams(("arbitrary", "arbitrary"), 58),
        name="moe_experts",
    )(xg, w1, w3, w2)


def _scatter_kernel(slott_ref, afft_ref, y_ref, x1_ref, g2_ref, o_ref, acc_ref):
    e = pl.program_id(2)
    ts = slott_ref.shape[0]
    cap = y_ref.shape[0]
    lane = lax.broadcasted_iota(I32, (ts, LANES), 1)
    mine = lane == e
    col = jnp.sum(jnp.where(mine, slott_ref[...].astype(F32), 0.0), axis=1, keepdims=True).astype(I32)
    gate = jnp.sum(jnp.where(mine, afft_ref[...], 0.0), axis=1, keepdims=True)
    cidx = lax.broadcasted_iota(I32, (ts, cap), 1)
    onehot = jnp.where(cidx == col, 1.0, 0.0).astype(BF16)
    contrib = jnp.dot(onehot, y_ref[...], preferred_element_type=F32) * gate

    @pl.when(e == 0)
    def _():
        acc_ref[...] = contrib

    @pl.when(e > 0)
    def _():
        acc_ref[...] += contrib

    @pl.when(e == pl.num_programs(2) - 1)
    def _():
        o_ref[...] = x1_ref[...] + g2_ref[...] * acc_ref[...]


def _scatter(slott, afft, y, x1, modr, l):
    b, s, d = x1.shape
    cap = y.shape[2]
    ts = 512
    return pl.pallas_call(
        _scatter_kernel,
        grid=(b, s // ts, N_EXPERTS),
        in_specs=[
            pl.BlockSpec((None, ts, LANES), lambda bi, i, e: (bi, i, 0)),
            pl.BlockSpec((None, ts, LANES), lambda bi, i, e: (bi, i, 0)),
            pl.BlockSpec((None, None, cap, d), lambda bi, i, e: (bi, e, 0, 0)),
            pl.BlockSpec((None, ts, d), lambda bi, i, e: (bi, i, 0)),
            pl.BlockSpec((None, None, None, 1, d), lambda bi, i, e: (l, bi, 5, 0, 0)),
        ],
        out_specs=pl.BlockSpec((None, ts, d), lambda bi, i, e: (bi, i, 0)),
        out_shape=jax.ShapeDtypeStruct((b, s, d), F32),
        scratch_shapes=[pltpu.VMEM((ts, d), F32)],
        compiler_params=_cparams(("arbitrary", "arbitrary", "arbitrary"), 48),
        name="moe_scatter",
    )(slott, afft, y, x1, modr)


def kernel(x, c, positions, w_mod, b_mod, norm1_g, norm2_g, w_in, fnet_w, pool_w, pool_scale,
           lam_q1, lam_k1, lam_q2, lam_k2, sub_g, rel_bias, w_out, w_router, w1, w3, w2, final_g):
    b, s, d = x.shape
    depth = w_mod.shape[0]
    fnet_wd = fnet_w.shape[1] * fnet_w.shape[2]
    pool_wd = pool_w.shape[1] * pool_w.shape[2]
    ab_w = fnet_wd + pool_wd
    in_w = w_in.shape[-1]
    cap = EC_CAPACITY * s // N_EXPERTS

    c8 = jnp.zeros((8, d), F32).at[:b].set(c)
    mod = _modulation(c8, w_mod, b_mod.reshape(depth, 1, N_MOD * d))
    modr = mod[:, :b].reshape(depth, b, N_MOD, 1, d)

    norm1_g3 = norm1_g.reshape(depth, 1, d)
    norm2_g3 = norm2_g.reshape(depth, 1, d)
    pool_scale3 = pool_scale.reshape(depth, 1, pool_wd)
    sub_g3 = sub_g.reshape(depth, 1, DV)
    lam4 = [a.reshape(depth, 1, DK) for a in (lam_q1, lam_k1, lam_q2, lam_k2)]
    w_router_p = jnp.zeros((depth, d, LANES), F32).at[:, :, :N_EXPERTS].set(w_router)

    wd = _dft_matrix(s)
    ab = _fnet_weights(fnet_w, s)
    plan = _attn_plan(positions, rel_bias)
    nbt = _near_bias_table(rel_bias, positions, plan[1], plan[2])

    for l in range(depth):
        h1 = _norm_mod(x, norm1_g3, modr, l, 0, 1).reshape(b * s, d)
        u_ab = _project([h1], w_in, l, 0, ab_w, F32, name="proj_in_ab").reshape(b, s, ab_w)
        qkv = _project([h1], w_in, l, ab_w, in_w - ab_w, BF16, lead_cols=N_HEADS * 2 * DK,
                       lead_scale=LOG2E * DK ** -0.5, name="proj_in_qkv").reshape(b, s, in_w - ab_w)

        za, zb = _fnet_z(u_ab, ab, l)
        ya = _dft_apply(wd, za, zb, b)
        yb = _pool_mixer(u_ab, pool_w, pool_scale3, l, fnet_wd // pool_w.shape[2])
        yc = _diff_attention(qkv, nbt, plan, rel_bias, positions, lam4, sub_g3, l)

        mixo = _project([ya.reshape(b * s, -1), yb.reshape(b * s, -1), yc.reshape(b * s, -1)],
                        w_out, l, 0, d, F32, name="proj_out").reshape(b, s, d)
        x1, h2, logits = _post_mix(x, mixo, modr, norm2_g3, w_router_p, l)

        idx, slott, afft = _route(logits, cap)
        xg = _gather(idx, h2, cap)
        y = _experts(xg, w1, w3, w2, l)
        x = _scatter(slott, afft, y, x1, modr, l)

    return _final_norm(x, final_g.reshape(1, d))
```

```python
import functools
import math

import numpy as np
import jax
import jax.numpy as jnp
from jax import lax
from jax.experimental import pallas as pl
from jax.experimental.pallas import tpu as pltpu

F32 = jnp.float32
BF16 = jnp.bfloat16
I32 = jnp.int32

FNET_GROUPS = 4
POOL_GROUPS = 4
POOL_HALO = 64
N_HEADS = 8
DK = 64
DV = 128
N_BUCKETS = 32
MAX_DISTANCE = 128
N_EXPERTS = 16
EC_CAPACITY = 2
N_MOD = 6
EPS = 1e-6
LANES = 128
LOG2E = 1.4426950408889634
ATT_TQ = 256
ATT_TK = 512
ATT_NS = 2
MOE_NF = 4


def _cparams(sem, vmem_mb=None):
    kw = dict(dimension_semantics=sem)
    if vmem_mb is not None:
        kw["vmem_limit_bytes"] = vmem_mb * 1024 * 1024
    return pltpu.CompilerParams(**kw)


def _silu(x):
    return x * jax.nn.sigmoid(x)


def _mod_kernel(c_ref, w_ref, b_ref, o_ref):
    ca = _silu(c_ref[...]).astype(BF16)
    o_ref[...] = jnp.dot(ca, w_ref[...].astype(BF16), preferred_element_type=F32) + b_ref[...]


def _modulation(c8, w_mod, b_mod3):
    depth, d, n = w_mod.shape
    tn = 1024
    return pl.pallas_call(
        _mod_kernel,
        grid=(depth, n // tn),
        in_specs=[
            pl.BlockSpec((8, d), lambda l, j: (0, 0)),
            pl.BlockSpec((None, d, tn), lambda l, j: (l, 0, j)),
            pl.BlockSpec((None, 1, tn), lambda l, j: (l, 0, j)),
        ],
        out_specs=pl.BlockSpec((None, 8, tn), lambda l, j: (l, 0, j)),
        out_shape=jax.ShapeDtypeStruct((depth, 8, n), F32),
        compiler_params=_cparams(("arbitrary", "arbitrary")),
        name="modulation",
    )(c8, w_mod, b_mod3)


def _norm_mod_kernel(x_ref, g_ref, sc_ref, sh_ref, o_ref):
    x = x_ref[...]
    ms = jnp.mean(x * x, axis=-1, keepdims=True)
    y = x * lax.rsqrt(ms + EPS) * g_ref[...]
    o_ref[...] = (y * (1.0 + sc_ref[...]) + sh_ref[...]).astype(o_ref.dtype)


def _norm_mod(x, g3, modr, l, sh_idx, sc_idx):
    b, s, d = x.shape
    tm = 512
    return pl.pallas_call(
        _norm_mod_kernel,
        grid=(b, s // tm),
        in_specs=[
            pl.BlockSpec((None, tm, d), lambda bi, i: (bi, i, 0)),
            pl.BlockSpec((None, 1, d), lambda bi, i: (l, 0, 0)),
            pl.BlockSpec((None, None, None, 1, d), lambda bi, i: (l, bi, sc_idx, 0, 0)),
            pl.BlockSpec((None, None, None, 1, d), lambda bi, i: (l, bi, sh_idx, 0, 0)),
        ],
        out_specs=pl.BlockSpec((None, tm, d), lambda bi, i: (bi, i, 0)),
        out_shape=jax.ShapeDtypeStruct((b, s, d), BF16),
        compiler_params=_cparams(("arbitrary", "arbitrary")),
        name="norm_mod",
    )(x, g3, modr, modr)


def _final_norm_kernel(x_ref, g_ref, o_ref):
    x = x_ref[...]
    ms = jnp.mean(x * x, axis=-1, keepdims=True)
    o_ref[...] = x * lax.rsqrt(ms + EPS) * g_ref[...]


def _final_norm(x, g2d):
    b, s, d = x.shape
    tm = 512
    return pl.pallas_call(
        _final_norm_kernel,
        grid=(b, s // tm),
        in_specs=[
            pl.BlockSpec((None, tm, d), lambda bi, i: (bi, i, 0)),
            pl.BlockSpec((1, d), lambda bi, i: (0, 0)),
        ],
        out_specs=pl.BlockSpec((None, tm, d), lambda bi, i: (bi, i, 0)),
        out_shape=jax.ShapeDtypeStruct((b, s, d), F32),
        compiler_params=_cparams(("arbitrary", "arbitrary")),
        name="final_norm",
    )(x, g2d)


def _mm_kernel(*refs, k_sizes, lead_blocks, lead_scale):
    n_a = len(k_sizes)
    a_refs = refs[:n_a]
    w_ref, o_ref, wb_ref = refs[n_a], refs[n_a + 1], refs[n_a + 2]

    @pl.when(pl.program_id(1) == 0)
    def _():
        wb_ref[...] = w_ref[...].astype(BF16)

    acc = None
    off = 0
    for a_ref, ks in zip(a_refs, k_sizes):
        part = jnp.dot(a_ref[...], wb_ref[off:off + ks, :], preferred_element_type=F32)
        acc = part if acc is None else acc + part
        off += ks
    if lead_blocks:
        acc = acc * jnp.where(pl.program_id(0) < lead_blocks, lead_scale, 1.0)
    o_ref[...] = acc.astype(o_ref.dtype)


def _project(a_list, w, l, col0, ncols, out_dtype, tm=1024, tn=512, lead_cols=0, lead_scale=1.0,
             name="project"):
    m = a_list[0].shape[0]
    k_sizes = tuple(a.shape[1] for a in a_list)
    k = sum(k_sizes)
    assert w.shape[1] == k and col0 % tn == 0 and ncols % tn == 0 and m % tm == 0 and lead_cols % tn == 0
    cb0 = col0 // tn
    in_specs = [pl.BlockSpec((tm, ks), lambda j, i: (i, 0)) for ks in k_sizes]
    in_specs.append(pl.BlockSpec((None, k, tn), lambda j, i: (l, 0, cb0 + j)))
    return pl.pallas_call(
        functools.partial(_mm_kernel, k_sizes=k_sizes, lead_blocks=lead_cols // tn, lead_scale=lead_scale),
        grid=(ncols // tn, m // tm),
        in_specs=in_specs,
        out_specs=pl.BlockSpec((tm, tn), lambda j, i: (i, j)),
        out_shape=jax.ShapeDtypeStruct((m, ncols), out_dtype),
        scratch_shapes=[pltpu.VMEM((k, tn), BF16)],
        compiler_params=_cparams(("arbitrary", "arbitrary"), 48),
        name=name,
    )(*a_list, w)


def _dft_tables(s):
    sp = np.arange(s, dtype=np.int64)
    a = np.arange(64, dtype=np.int64)[:, None]
    ang1 = 2.0 * np.pi * ((a * sp[None, :]) % 64) / 64.0
    ang2 = 2.0 * np.pi * ((a * sp[None, :]) % s) / float(s)
    t1c = np.cos(ang1).astype(np.float32).reshape(64, 1, s)
    t1s = np.sin(ang1).astype(np.float32).reshape(64, 1, s)
    t2c = np.cos(ang2).astype(np.float32)
    t2s = np.sin(ang2).astype(np.float32)
    return t1c, t1s, t2c, t2s


def _dftgen_kernel(t1c_ref, t1s_ref, t2c_ref, t2s_ref, o_ref):
    s = t2c_ref.shape[1]
    c1, s1 = t1c_ref[...], t1s_ref[...]
    c2, s2 = t2c_ref[...], t2s_ref[...]
    o_ref[:, :s] = (c1 * c2 - s1 * s2).astype(BF16)
    o_ref[:, s:] = (-(s1 * c2 + c1 * s2)).astype(BF16)


def _dft_matrix(s):
    assert s % 64 == 0 and s // 64 == 64
    t1c, t1s, t2c, t2s = _dft_tables(s)
    return pl.pallas_call(
        _dftgen_kernel,
        grid=(64,),
        in_specs=[
            pl.BlockSpec((None, 1, s), lambda a: (a, 0, 0)),
            pl.BlockSpec((None, 1, s), lambda a: (a, 0, 0)),
            pl.BlockSpec((64, s), lambda a: (0, 0)),
            pl.BlockSpec((64, s), lambda a: (0, 0)),
        ],
        out_specs=pl.BlockSpec((64, 2 * s), lambda a: (a, 0)),
        out_shape=jax.ShapeDtypeStruct((s, 2 * s), BF16),
        compiler_params=_cparams(("arbitrary",)),
        name="dft_matrix",
    )(t1c, t1s, t2c, t2s)


def _fnet_w_kernel(cc_ref, sc_ref, w_ref, o_ref, *, norm):
    depth, groups, cg, _ = w_ref.shape
    cc, sc = cc_ref[...], sc_ref[...]
    for l in range(depth):
        for g in range(groups):
            w = w_ref[l, g]
            a = jnp.dot(cc, w, preferred_element_type=F32, precision=lax.Precision.HIGHEST)
            b = jnp.dot(sc, w, preferred_element_type=F32, precision=lax.Precision.HIGHEST)
            o_ref[l, g, :, :cg] = (a * norm).astype(BF16)
            o_ref[l, g, :, cg:] = (b * norm).astype(BF16)


def _fnet_weights(fnet_w, s):
    depth, groups, cg, _ = fnet_w.shape
    idx = np.arange(cg, dtype=np.int64)
    ang = 2.0 * np.pi * ((idx[:, None] * idx[None, :]) % cg) / float(cg)
    cc = np.cos(ang).astype(np.float32)
    sc = np.sin(ang).astype(np.float32)
    norm = 1.0 / math.sqrt(float(s) * float(cg))
    return pl.pallas_call(
        functools.partial(_fnet_w_kernel, norm=norm),
        out_shape=jax.ShapeDtypeStruct((depth, groups, cg, 2 * cg), BF16),
        name="fnet_weights",
    )(cc, sc, fnet_w)


def _fnet_z_kernel(u_ref, ab_ref, za_ref, zb_ref):
    groups, cg = ab_ref.shape[0], ab_ref.shape[1]
    for g in range(groups):
        ug = u_ref[:, g * cg:(g + 1) * cg].astype(BF16)
        z = jnp.dot(ug, ab_ref[g], preferred_element_type=F32)
        za_ref[:, g * cg:(g + 1) * cg] = z[:, :cg].astype(BF16)
        zb_ref[:, g * cg:(g + 1) * cg] = z[:, cg:].astype(BF16)


def _fnet_z(u_ab, ab, l):
    b, s, _ = u_ab.shape
    groups, cg = ab.shape[1], ab.shape[2]
    fw = groups * cg
    ts = 512
    shp = jax.ShapeDtypeStruct((s, b * fw), BF16)
    return pl.pallas_call(
        _fnet_z_kernel,
        grid=(b, s // ts),
        in_specs=[
            pl.BlockSpec((None, ts, fw), lambda bi, i: (bi, i, 0)),
            pl.BlockSpec((None, groups, cg, 2 * cg), lambda bi, i: (l, 0, 0, 0)),
        ],
        out_specs=[pl.BlockSpec((ts, fw), lambda bi, i: (i, bi)),
                   pl.BlockSpec((ts, fw), lambda bi, i: (i, bi))],
        out_shape=[shp, shp],
        compiler_params=_cparams(("arbitrary", "arbitrary")),
        name="fnet_z",
    )(u_ab, ab)


def _dft_apply_kernel(wd_ref, za_ref, zb_ref, o_ref):
    s = za_ref.shape[0]
    acc = jnp.dot(wd_ref[:, :s], za_ref[...], preferred_element_type=F32)
    acc = acc + jnp.dot(wd_ref[:, s:], zb_ref[...], preferred_element_type=F32)
    o_ref[...] = acc.astype(o_ref.dtype)


def _dft_apply(wd, za, zb, b):
    s = wd.shape[0]
    fw = za.shape[1] // b
    tm = 512
    return pl.pallas_call(
        _dft_apply_kernel,
        grid=(b, s // tm),
        in_specs=[
            pl.BlockSpec((tm, 2 * s), lambda bi, i: (i, 0)),
            pl.BlockSpec((s, fw), lambda bi, i: (0, bi)),
            pl.BlockSpec((s, fw), lambda bi, i: (0, bi)),
        ],
        out_specs=pl.BlockSpec((None, tm, fw), lambda bi, i: (bi, i, 0)),
        out_shape=jax.ShapeDtypeStruct((b, s, fw), BF16),
        compiler_params=_cparams(("arbitrary", "arbitrary"), 48),
        name="dft_apply",
    )(wd, za, zb)


def _pool_kernel(u_ref, w_ref, sc_ref, o_ref, pad_ref):
    s, cg = u_ref.shape
    t = 256
    half = jnp.left_shift(jnp.int32(1), pl.program_id(1))
    pad_ref[0:POOL_HALO, :] = jnp.zeros((POOL_HALO, cg), F32)
    pad_ref[s + POOL_HALO:s + 2 * POOL_HALO, :] = jnp.zeros((POOL_HALO, cg), F32)
    pad_ref[POOL_HALO:s + POOL_HALO, :] = u_ref[...]
    ii = lax.broadcasted_iota(I32, (t, t + 2 * POOL_HALO), 0)
    jj = lax.broadcasted_iota(I32, (t, t + 2 * POOL_HALO), 1)
    dlt = jj - ii - POOL_HALO
    band = jnp.where(dlt >= -half, jnp.where(dlt <= half - 1, 1.0, 0.0), 0.0).astype(BF16)
    wb = w_ref[...].astype(BF16)
    scale = sc_ref[...]

    def body(ti, carry):
        r0 = pl.multiple_of(ti * t, t)
        seg = pad_ref[pl.ds(r0, t + 2 * POOL_HALO), :]
        hi = seg.astype(BF16)
        lo = (seg - hi.astype(F32)).astype(BF16)
        win = jnp.dot(band, hi, preferred_element_type=F32) + jnp.dot(band, lo, preferred_element_type=F32)
        gi = r0 + lax.broadcasted_iota(I32, (t, cg), 0)
        lo_i = jnp.maximum(gi - half, 0)
        hi_i = jnp.minimum(gi + half - 1, s - 1)
        cnt = (hi_i - lo_i + 1).astype(F32)
        dmean = win / cnt - seg[POOL_HALO:POOL_HALO + t, :]
        y = jnp.dot(dmean.astype(BF16), wb, preferred_element_type=F32) * scale
        o_ref[pl.ds(r0, t), :] = y.astype(o_ref.dtype)
        return carry

    lax.fori_loop(0, s // t, body, 0)


def _pool_mixer(u_ab, pool_w, pool_scale3, l, col_block0):
    b, s, _ = u_ab.shape
    groups, cg = pool_w.shape[1], pool_w.shape[2]
    return pl.pallas_call(
        _pool_kernel,
        grid=(b, groups),
        in_specs=[
            pl.BlockSpec((None, s, cg), lambda bi, g: (bi, 0, col_block0 + g)),
            pl.BlockSpec((None, None, cg, cg), lambda bi, g: (l, g, 0, 0)),
            pl.BlockSpec((None, 1, cg), lambda bi, g: (l, 0, g)),
        ],
        out_specs=pl.BlockSpec((None, s, cg), lambda bi, g: (bi, 0, g)),
        out_shape=jax.ShapeDtypeStruct((b, s, groups * cg), BF16),
        scratch_shapes=[pltpu.VMEM((s + 2 * POOL_HALO, cg), F32)],
        compiler_params=_cparams(("arbitrary", "arbitrary")),
        name="pool_mixer",
    )(u_ab, pool_w, pool_scale3)


def _bucket(rel):
    nb = N_BUCKETS // 2
    max_exact = nb // 2
    n = jnp.abs(rel)
    nf = jnp.maximum(n, 1).astype(F32)
    large = max_exact + (jnp.log(nf / max_exact) / math.log(MAX_DISTANCE / max_exact)
                         * (nb - max_exact)).astype(I32)
    large = jnp.minimum(large, nb - 1)
    return jnp.where(rel > 0, nb, 0) + jnp.where(n < max_exact, n, large)


def _bias_tile(tab_ref, h, rel):
    bucket = _bucket(rel)
    val = jnp.full(rel.shape, tab_ref[h], F32)
    for j in range(1, N_BUCKETS):
        val = jnp.where(bucket == j, tab_ref[j * N_HEADS + h], val)
    return val * LOG2E


def _near_bias_kernel(slotj_ref, nnear_ref, tabt_ref, pq_ref, pk_ref, o_ref):
    del slotj_ref
    used = jnp.where(pl.program_id(1) < nnear_ref[pl.program_id(0)], 1.0, 0.0)
    bucket = _bucket(pk_ref[...] - pq_ref[...])
    tq, tk = bucket.shape
    for h in range(N_HEADS):
        row = jnp.broadcast_to(tabt_ref[h:h + 1, :], (tq, LANES))
        cols = [jnp.take_along_axis(row, bucket[:, c * LANES:(c + 1) * LANES], axis=1)
                for c in range(tk // LANES)]
        o_ref[h] = jnp.concatenate(cols, axis=1) * used


def _near_bias_table(rel_bias, positions, slotj, nnear):
    s = positions.shape[0]
    nq = s // ATT_TQ
    tabt = jnp.zeros((N_HEADS, LANES), F32).at[:, :N_BUCKETS].set(rel_bias.T * LOG2E)
    grid_spec = pltpu.PrefetchScalarGridSpec(
        num_scalar_prefetch=2,
        grid=(nq, ATT_NS),
        in_specs=[
            pl.BlockSpec((N_HEADS, LANES), lambda i, n, sj, nn: (0, 0)),
            pl.BlockSpec((ATT_TQ, 1), lambda i, n, sj, nn: (i, 0)),
            pl.BlockSpec((1, ATT_TK), lambda i, n, sj, nn: (0, sj[i * ATT_NS + n])),
        ],
        out_specs=pl.BlockSpec((N_HEADS, None, None, ATT_TQ, ATT_TK), lambda i, n, sj, nn: (0, i, n, 0, 0)),
    )
    return pl.pallas_call(
        _near_bias_kernel,
        grid_spec=grid_spec,
        out_shape=jax.ShapeDtypeStruct((N_HEADS, nq, ATT_NS, ATT_TQ, ATT_TK), F32),
        compiler_params=_cparams(("arbitrary", "arbitrary")),
        name="near_bias_table",
    )(slotj, nnear, tabt, positions.reshape(s, 1), positions.reshape(1, s))


def _attn_plan(positions, rel_bias):
    s = positions.shape[0]
    nq, nk = s // ATT_TQ, s // ATT_TK
    pq = positions.reshape(nq, ATT_TQ)
    pk = positions.reshape(nk, ATT_TK)
    rel_min = pk.min(axis=1)[None, :] - pq.max(axis=1)[:, None]
    rel_max = pk.max(axis=1)[None, :] - pq.min(axis=1)[:, None]
    cls = jnp.where(rel_min >= MAX_DISTANCE, 1, jnp.where(rel_max <= -MAX_DISTANCE, 0, 2)).astype(I32)
    near = cls == 2
    nnear = near.sum(axis=1).astype(I32)
    slotj = jnp.argsort(jnp.logical_not(near), axis=1, stable=True)[:, :ATT_NS].astype(I32)
    fits = jnp.all(nnear <= ATT_NS)
    nb = N_BUCKETS // 2
    ctab = jnp.stack([rel_bias[nb - 1], rel_bias[2 * nb - 1], jnp.zeros((N_HEADS,), F32)], axis=1) * LOG2E
    return cls.reshape(-1), slotj.reshape(-1), nnear, ctab.reshape(-1).astype(F32), fits


def _lambda(lq1_ref, lk1_ref, lq2_ref, lk2_ref, lam_init):
    return (jnp.exp(jnp.sum(lq1_ref[...] * lk1_ref[...], axis=-1, keepdims=True))
            - jnp.exp(jnp.sum(lq2_ref[...] * lk2_ref[...], axis=-1, keepdims=True)) + lam_init)


def _stack_maps(q):
    lane = lax.broadcasted_iota(I32, q.shape, 1)
    zero = jnp.zeros_like(q)
    return jnp.concatenate([jnp.where(lane < DK, q, zero), jnp.where(lane >= DK, q, zero)], axis=0)


def _attn_finish(o1, o2, lam, sg, lam_init, dtype):
    o = o1 - lam * o2
    ms = jnp.mean(o * o, axis=-1, keepdims=True)
    y = o * lax.rsqrt(ms + EPS) * sg
    return (y * (1.0 - lam_init)).astype(dtype)


def _attn_kernel(cls_ref, slotj_ref, ctab_ref, lq1_ref, lk1_ref, lq2_ref, lk2_ref,
                 q_ref, k_ref, v_ref, nbp0_ref, nbp1_ref, nba0_ref, nba1_ref, nbb0_ref, nbb1_ref,
                 sg_ref, o_ref, sa_scr, sb_scr, vaug_scr, *, lam_init):
    h = pl.program_id(1)
    ip = pl.program_id(2)
    tq = ATT_TQ
    s = k_ref.shape[0]
    nk = s // ATT_TK
    nq = s // tq
    lam = _lambda(lq1_ref, lk1_ref, lq2_ref, lk2_ref, lam_init)
    sg = sg_ref[...]

    def scores(t, s_scr, nb_refs):
        qs = _stack_maps(q_ref[pl.ds(pl.multiple_of(t * tq, tq), tq), :])
        for j in range(nk):
            cval = ctab_ref[h * 3 + cls_ref[t * nk + j]]
            sc = lax.dot_general(qs, k_ref[j * ATT_TK:(j + 1) * ATT_TK, :], (((1,), (1,)), ((), ())),
                                 preferred_element_type=F32)
            s_scr[j] = sc + cval
        for n, nb_ref in enumerate(nb_refs):
            j = slotj_ref[t * ATT_NS + n]
            bias = nb_ref[...]
            s_scr[j, 0:tq, :] = s_scr[j, 0:tq, :] + bias
            s_scr[j, tq:2 * tq, :] = s_scr[j, tq:2 * tq, :] + bias

    def outputs(s_scr, row0):
        m128 = None
        for j in range(nk):
            blk = s_scr[j]
            for c in range(ATT_TK // LANES):
                part = blk[:, c * LANES:(c + 1) * LANES]
                m128 = part if m128 is None else jnp.maximum(m128, part)
        m = jnp.max(m128, axis=-1, keepdims=True)
        acc = None
        for j in range(nk):
            e = jnp.exp2(s_scr[j] - m).astype(BF16)
            part = jnp.dot(e, vaug_scr[j * ATT_TK:(j + 1) * ATT_TK, :], preferred_element_type=F32)
            acc = part if acc is None else acc + part
        o1 = acc[:tq, :DV] / acc[:tq, DV:DV + 1]
        o2 = acc[tq:, :DV] / acc[tq:, DV:DV + 1]
        o_ref[row0:row0 + tq, :] = _attn_finish(o1, o2, lam, sg, lam_init, o_ref.dtype)

    @pl.when(ip == 0)
    def _():
        vaug_scr[:, :DV] = v_ref[...]
        lane = lax.broadcasted_iota(I32, (s, DV), 1)
        vaug_scr[:, DV:] = jnp.where(lane == 0, 1.0, 0.0).astype(BF16)
        scores(0, sa_scr, (nbp0_ref, nbp1_ref))

    t_odd = 2 * ip + 1
    t_next = jnp.minimum(2 * ip + 2, nq - 1)
    scores(t_odd, sb_scr, (nbb0_ref, nbb1_ref))
    outputs(sa_scr, 0)
    scores(t_next, sa_scr, (nba0_ref, nba1_ref))
    outputs(sb_scr, tq)


def _attn_fast(qkv, nbt, plan, lam4, sub_g3, l):
    b, s, _ = qkv.shape
    cls, slotj, _, ctab, _ = plan
    nq = s // ATT_TQ
    assert nq % 2 == 0 and ATT_NS == 2
    lam_init = 0.8 - 0.6 * math.exp(-0.3 * l)
    smem = pl.BlockSpec(memory_space=pltpu.SMEM)
    lam_specs = [pl.BlockSpec((None, 1, DK), lambda bi, h, ip: (l, 0, 0)) for _ in range(4)]
    nb_tile = (None, None, None, ATT_TQ, ATT_TK)
    nb_specs = ([pl.BlockSpec(nb_tile, lambda bi, h, ip, n=n: (h, 0, n, 0, 0)) for n in range(ATT_NS)]
                + [pl.BlockSpec(nb_tile, lambda bi, h, ip, n=n: (h, jnp.minimum(2 * ip + 2, nq - 1), n, 0, 0))
                   for n in range(ATT_NS)]
                + [pl.BlockSpec(nb_tile, lambda bi, h, ip, n=n: (h, 2 * ip + 1, n, 0, 0)) for n in range(ATT_NS)])
    return pl.pallas_call(
        functools.partial(_attn_kernel, lam_init=lam_init),
        grid=(b, N_HEADS, nq // 2),
        in_specs=[smem, smem, smem] + lam_specs + [
            pl.BlockSpec((None, s, 2 * DK), lambda bi, h, ip: (bi, 0, h)),
            pl.BlockSpec((None, s, 2 * DK), lambda bi, h, ip: (bi, 0, N_HEADS + h)),
            pl.BlockSpec((None, s, DV), lambda bi, h, ip: (bi, 0, 2 * N_HEADS + h)),
        ] + nb_specs + [pl.BlockSpec((None, 1, DV), lambda bi, h, ip: (l, 0, 0))],
        out_specs=pl.BlockSpec((None, 2 * ATT_TQ, DV), lambda bi, h, ip: (bi, ip, h)),
        out_shape=jax.ShapeDtypeStruct((b, s, N_HEADS * DV), BF16),
        scratch_shapes=[pltpu.VMEM((s // ATT_TK, 2 * ATT_TQ, ATT_TK), F32),
                        pltpu.VMEM((s // ATT_TK, 2 * ATT_TQ, ATT_TK), F32),
                        pltpu.VMEM((s, 2 * DV), BF16)],
        compiler_params=_cparams(("arbitrary", "arbitrary", "arbitrary"), 56),
        name="diff_attention",
    )(cls, slotj, ctab, *lam4, qkv, qkv, qkv, nbt, nbt, nbt, nbt, nbt, nbt, sub_g3)


def _attn_any_kernel(tab_ref, lq1_ref, lk1_ref, lq2_ref, lk2_ref, q_ref, k_ref, v_ref, pq_ref, pk_ref,
                     sg_ref, o_ref, *, lam_init):
    h = pl.program_id(1)
    tq = q_ref.shape[0]
    bias = _bias_tile(tab_ref, h, pk_ref[...] - pq_ref[...])
    sc = lax.dot_general(_stack_maps(q_ref[...]), k_ref[...], (((1,), (1,)), ((), ())),
                         preferred_element_type=F32)
    v = v_ref[...]

    def one_map(sm):
        sm = sm + bias
        e = jnp.exp2(sm - jnp.max(sm, axis=-1, keepdims=True))
        den = jnp.sum(e, axis=-1, keepdims=True)
        return jnp.dot(e.astype(BF16), v, preferred_element_type=F32) / den

    lam = _lambda(lq1_ref, lk1_ref, lq2_ref, lk2_ref, lam_init)
    o_ref[...] = _attn_finish(one_map(sc[:tq]), one_map(sc[tq:]), lam, sg_ref[...], lam_init, o_ref.dtype)


def _attn_any(qkv, rel_bias, positions, lam4, sub_g3, l):
    b, s, _ = qkv.shape
    tq = 128
    lam_init = 0.8 - 0.6 * math.exp(-0.3 * l)
    lam_specs = [pl.BlockSpec((None, 1, DK), lambda bi, h, i: (l, 0, 0)) for _ in range(4)]
    return pl.pallas_call(
        functools.partial(_attn_any_kernel, lam_init=lam_init),
        grid=(b, N_HEADS, s // tq),
        in_specs=[pl.BlockSpec(memory_space=pltpu.SMEM)] + lam_specs + [
            pl.BlockSpec((None, tq, 2 * DK), lambda bi, h, i: (bi, i, h)),
            pl.BlockSpec((None, s, 2 * DK), lambda bi, h, i: (bi, 0, N_HEADS + h)),
            pl.BlockSpec((None, s, DV), lambda bi, h, i: (bi, 0, 2 * N_HEADS + h)),
            pl.BlockSpec((tq, 1), lambda bi, h, i: (i, 0)),
            pl.BlockSpec((1, s), lambda bi, h, i: (0, 0)),
            pl.BlockSpec((None, 1, DV), lambda bi, h, i: (l, 0, 0)),
        ],
        out_specs=pl.BlockSpec((None, tq, DV), lambda bi, h, i: (bi, i, h)),
        out_shape=jax.ShapeDtypeStruct((b, s, N_HEADS * DV), BF16),
        compiler_params=_cparams(("arbitrary", "arbitrary", "arbitrary"), 48),
        name="diff_attention_any",
    )(rel_bias.reshape(-1), *lam4, qkv, qkv, qkv, positions.reshape(s, 1), positions.reshape(1, s), sub_g3)


def _diff_attention(qkv, nbt, plan, rel_bias, positions, lam4, sub_g3, l):
    return lax.cond(plan[4],
                    lambda: _attn_fast(qkv, nbt, plan, lam4, sub_g3, l),
                    lambda: _attn_any(qkv, rel_bias, positions, lam4, sub_g3, l))


def _post_mix_kernel(x_ref, mo_ref, g1_ref, ng_ref, sc_ref, sh_ref, wr_ref, x1_ref, h_ref, lg_ref):
    x1 = x_ref[...] + g1_ref[...] * mo_ref[...]
    x1_ref[...] = x1
    ms = jnp.mean(x1 * x1, axis=-1, keepdims=True)
    h = x1 * lax.rsqrt(ms + EPS) * ng_ref[...]
    h = h * (1.0 + sc_ref[...]) + sh_ref[...]
    h_ref[...] = h.astype(h_ref.dtype)
    lg_ref[...] = jnp.dot(h, wr_ref[...], preferred_element_type=F32, precision=lax.Precision.HIGHEST)


def _post_mix(x, mixo, modr, norm2_g3, w_router_p, l):
    b, s, d = x.shape
    tm = 256
    mspec = lambda idx: pl.BlockSpec((None, None, None, 1, d), lambda bi, i: (l, bi, idx, 0, 0))
    return pl.pallas_call(
        _post_mix_kernel,
        grid=(b, s // tm),
        in_specs=[
            pl.BlockSpec((None, tm, d), lambda bi, i: (bi, i, 0)),
            pl.BlockSpec((None, tm, d), lambda bi, i: (bi, i, 0)),
            mspec(2),
            pl.BlockSpec((None, 1, d), lambda bi, i: (l, 0, 0)),
            mspec(4),
            mspec(3),
            pl.BlockSpec((None, d, LANES), lambda bi, i: (l, 0, 0)),
        ],
        out_specs=[
            pl.BlockSpec((None, tm, d), lambda bi, i: (bi, i, 0)),
            pl.BlockSpec((None, tm, d), lambda bi, i: (bi, i, 0)),
            pl.BlockSpec((None, tm, LANES), lambda bi, i: (bi, i, 0)),
        ],
        out_shape=[
            jax.ShapeDtypeStruct((b, s, d), F32),
            jax.ShapeDtypeStruct((b, s, d), F32),
            jax.ShapeDtypeStruct((b, s, LANES), F32),
        ],
        compiler_params=_cparams(("arbitrary", "arbitrary")),
        name="post_mix",
    )(x, mixo, modr, norm2_g3, modr, modr, w_router_p)


def _cumsum_lanes(x01):
    rows, n = x01.shape
    blk = 512
    ii = lax.broadcasted_iota(I32, (blk, blk), 0)
    jj = lax.broadcasted_iota(I32, (blk, blk), 1)
    tri = jnp.where(ii <= jj, 1.0, 0.0).astype(BF16)
    xb = x01.astype(BF16)
    carry = jnp.zeros((rows, 1), F32)
    outs = []
    for c in range(n // blk):
        part = jnp.dot(xb[:, c * blk:(c + 1) * blk], tri, preferred_element_type=F32) + carry
        outs.append(part)
        carry = part[:, blk - 1:blk]
    return jnp.concatenate(outs, axis=1)


def _route_kernel(lg_ref, idx_ref, slott_ref, afft_ref, cs_scr, *, cap):
    lg = lg_ref[...]
    lane = lax.broadcasted_iota(I32, lg.shape, 1)
    valid = lane < N_EXPERTS
    lgm = jnp.where(valid, lg, -1e30)
    m = jnp.max(lgm, axis=-1, keepdims=True)
    ex = jnp.where(valid, jnp.exp(lgm - m), 0.0)
    aff = ex / jnp.sum(ex, axis=-1, keepdims=True)
    afft_ref[...] = aff
    at = aff.T[:N_EXPERTS, :]
    keys = lax.bitcast_convert_type(at, I32)

    def body(i, prefix):
        cand = prefix | jnp.left_shift(jnp.int32(1), 30 - i)
        cnt = jnp.sum(jnp.where(keys >= cand, 1.0, 0.0), axis=1, keepdims=True)
        return jnp.where(cnt >= cap, cand, prefix)

    thr = lax.fori_loop(0, 31, body, jnp.zeros((N_EXPERTS, 1), I32))
    gt = jnp.where(keys > thr, 1.0, 0.0)
    eq = jnp.where(keys == thr, 1.0, 0.0)
    need = cap - jnp.sum(gt, axis=1, keepdims=True)
    take = eq * jnp.where(_cumsum_lanes(eq) <= need, 1.0, 0.0)
    sel = gt + take
    cs = _cumsum_lanes(sel)
    slot = jnp.where(sel > 0.5, cs - 1.0, -1.0).astype(I32)
    s = slot.shape[1]
    pad = jnp.full((LANES - N_EXPERTS, s), -1, I32)
    slott_ref[...] = jnp.concatenate([slot, pad], axis=0).T

    cs_scr[...] = cs
    lane = lax.broadcasted_iota(I32, (cap, LANES), 1)
    rows = 128

    def one_expert(e, idxt):
        row = cs_scr[pl.ds(e, 1), :]
        cols = []
        for c0 in range(0, cap, rows):
            cio = (lax.broadcasted_iota(I32, (rows, s), 0) + c0).astype(F32)
            cols.append(jnp.sum(jnp.where(row <= cio, 1.0, 0.0), axis=1, keepdims=True))
        return jnp.where(lane == e, jnp.concatenate(cols, axis=0), idxt)

    idxt = lax.fori_loop(0, N_EXPERTS, one_expert, jnp.zeros((cap, LANES), F32))
    idx_ref[...] = idxt.T[:N_EXPERTS, :].astype(I32)


def _route(logits, cap):
    b, s, _ = logits.shape
    return pl.pallas_call(
        functools.partial(_route_kernel, cap=cap),
        grid=(b,),
        in_specs=[pl.BlockSpec((None, s, LANES), lambda bi: (bi, 0, 0))],
        out_specs=[
            pl.BlockSpec((None, N_EXPERTS, cap), lambda bi: (bi, 0, 0)),
            pl.BlockSpec((None, s, LANES), lambda bi: (bi, 0, 0)),
            pl.BlockSpec((None, s, LANES), lambda bi: (bi, 0, 0)),
        ],
        out_shape=[
            jax.ShapeDtypeStruct((b, N_EXPERTS, cap), I32),
            jax.ShapeDtypeStruct((b, s, LANES), I32),
            jax.ShapeDtypeStruct((b, s, LANES), F32),
        ],
        scratch_shapes=[pltpu.VMEM((N_EXPERTS, s), F32)],
        compiler_params=_cparams(("arbitrary",), 48),
        name="route",
    )(logits)


def _expert_kernel(idx_ref, h_ref, w1_ref, w3_ref, w2_ref, y_ref, rows_ref, xb_ref, acc_ref, sem):
    e = pl.program_id(0)
    f = pl.program_id(1)
    ne = pl.num_programs(0)
    nf = pl.num_programs(1)
    nb, cap, d = y_ref.shape
    n_rows = nb * cap
    per_step = n_rows // MOE_NF

    def row_copy(ee, r, t):
        del ee
        return pltpu.make_async_copy(h_ref.at[r // cap, pl.ds(t, 1), :], rows_ref.at[pl.ds(r, 1), :], sem)

    def token(ee, r):
        return idx_ref[((r // cap) * N_EXPERTS + ee) * cap + r % cap]

    def wait_all_rows():
        def body(r, carry):
            row_copy(0, r, 0).wait()
            return carry
        lax.fori_loop(0, n_rows, body, 0)

    @pl.when(jnp.logical_and(e == 0, f == 0))
    def _():
        def body(r, carry):
            row_copy(0, r, token(0, r)).start()
            return carry
        lax.fori_loop(0, n_rows, body, 0)

    @pl.when(f == 0)
    def _():
        wait_all_rows()
        xb_ref[...] = rows_ref[...].astype(BF16)

    e_next = jnp.minimum(e + 1, ne - 1)
    for r in range(per_step):
        rr = f * per_step + r
        row_copy(e_next, rr, token(e_next, rr)).start()

    xg = xb_ref[...]
    a = jnp.dot(xg, w1_ref[...].astype(BF16), preferred_element_type=F32)
    g = jnp.dot(xg, w3_ref[...].astype(BF16), preferred_element_type=F32)
    act = (_silu(a) * g).astype(BF16)
    part = jnp.dot(act, w2_ref[...].astype(BF16), preferred_element_type=F32)

    @pl.when(f == 0)
    def _():
        acc_ref[...] = part

    @pl.when(f > 0)
    def _():
        acc_ref[...] += part

    @pl.when(f == nf - 1)
    def _():
        y_ref[...] = acc_ref[...].reshape(nb, cap, d).astype(y_ref.dtype)

    @pl.when(jnp.logical_and(e == ne - 1, f == nf - 1))
    def _():
        wait_all_rows()


def _experts(idx, h2, w1, w3, w2, l, cap):
    b, s, d = h2.shape
    e = w1.shape[1]
    ff = w1.shape[-1]
    fc = ff // MOE_NF
    grid_spec = pltpu.PrefetchScalarGridSpec(
        num_scalar_prefetch=1,
        grid=(e, MOE_NF),
        in_specs=[
            pl.BlockSpec(memory_space=pl.ANY),
            pl.BlockSpec((None, None, d, fc), lambda ei, f, ix: (l, ei, 0, f)),
            pl.BlockSpec((None, None, d, fc), lambda ei, f, ix: (l, ei, 0, f)),
            pl.BlockSpec((None, None, fc, d), lambda ei, f, ix: (l, ei, f, 0)),
        ],
        out_specs=pl.BlockSpec((b, None, cap, d), lambda ei, f, ix: (0, ei, 0, 0)),
        scratch_shapes=[pltpu.VMEM((b * cap, d), F32), pltpu.VMEM((b * cap, d), BF16),
                        pltpu.VMEM((b * cap, d), F32), pltpu.SemaphoreType.DMA],
    )
    return pl.pallas_call(
        _expert_kernel,
        grid_spec=grid_spec,
        out_shape=jax.ShapeDtypeStruct((b, e, cap, d), BF16),
        compiler_params=_cparams(("arbitrary", "arbitrary"), 56),
        name="moe_experts",
    )(idx.reshape(-1), h2, w1, w3, w2)


def _scatter_kernel(slott_ref, afft_ref, y_ref, x1_ref, g2_ref, o_ref, acc_ref):
    e = pl.program_id(2)
    ts = slott_ref.shape[0]
    cap = y_ref.shape[0]
    lane = lax.broadcasted_iota(I32, (ts, LANES), 1)
    mine = lane == e
    col = jnp.sum(jnp.where(mine, slott_ref[...].astype(F32), 0.0), axis=1, keepdims=True).astype(I32)
    gate = jnp.sum(jnp.where(mine, afft_ref[...], 0.0), axis=1, keepdims=True)
    cidx = lax.broadcasted_iota(I32, (ts, cap), 1)
    onehot = jnp.where(cidx == col, 1.0, 0.0).astype(BF16)
    contrib = jnp.dot(onehot, y_ref[...], preferred_element_type=F32) * gate

    @pl.when(e == 0)
    def _():
        acc_ref[...] = contrib

    @pl.when(e > 0)
    def _():
        acc_ref[...] += contrib

    @pl.when(e == pl.num_programs(2) - 1)
    def _():
        o_ref[...] = x1_ref[...] + g2_ref[...] * acc_ref[...]


def _scatter(slott, afft, y, x1, modr, l):
    b, s, d = x1.shape
    cap = y.shape[2]
    ts = 512
    return pl.pallas_call(
        _scatter_kernel,
        grid=(b, s // ts, N_EXPERTS),
        in_specs=[
            pl.BlockSpec((None, ts, LANES), lambda bi, i, e: (bi, i, 0)),
            pl.BlockSpec((None, ts, LANES), lambda bi, i, e: (bi, i, 0)),
            pl.BlockSpec((None, None, cap, d), lambda bi, i, e: (bi, e, 0, 0)),
            pl.BlockSpec((None, ts, d), lambda bi, i, e: (bi, i, 0)),
            pl.BlockSpec((None, None, None, 1, d), lambda bi, i, e: (l, bi, 5, 0, 0)),
        ],
        out_specs=pl.BlockSpec((None, ts, d), lambda bi, i, e: (bi, i, 0)),
        out_shape=jax.ShapeDtypeStruct((b, s, d), F32),
        scratch_shapes=[pltpu.VMEM((ts, d), F32)],
        compiler_params=_cparams(("arbitrary", "arbitrary", "arbitrary"), 48),
        name="moe_scatter",
    )(slott, afft, y, x1, modr)


def kernel(x, c, positions, w_mod, b_mod, norm1_g, norm2_g, w_in, fnet_w, pool_w, pool_scale,
           lam_q1, lam_k1, lam_q2, lam_k2, sub_g, rel_bias, w_out, w_router, w1, w3, w2, final_g):
    b, s, d = x.shape
    depth = w_mod.shape[0]
    fnet_wd = fnet_w.shape[1] * fnet_w.shape[2]
    pool_wd = pool_w.shape[1] * pool_w.shape[2]
    ab_w = fnet_wd + pool_wd
    in_w = w_in.shape[-1]
    cap = EC_CAPACITY * s // N_EXPERTS

    c8 = jnp.zeros((8, d), F32).at[:b].set(c)
    mod = _modulation(c8, w_mod, b_mod.reshape(depth, 1, N_MOD * d))
    modr = mod[:, :b].reshape(depth, b, N_MOD, 1, d)

    norm1_g3 = norm1_g.reshape(depth, 1, d)
    norm2_g3 = norm2_g.reshape(depth, 1, d)
    pool_scale3 = pool_scale.reshape(depth, 1, pool_wd)
    sub_g3 = sub_g.reshape(depth, 1, DV)
    lam4 = [a.reshape(depth, 1, DK) for a in (lam_q1, lam_k1, lam_q2, lam_k2)]
    w_router_p = jnp.zeros((depth, d, LANES), F32).at[:, :, :N_EXPERTS].set(w_router)

    wd = _dft_matrix(s)
    ab = _fnet_weights(fnet_w, s)
    plan = _attn_plan(positions, rel_bias)
    nbt = _near_bias_table(rel_bias, positions, plan[1], plan[2])

    for l in range(depth):
        h1 = _norm_mod(x, norm1_g3, modr, l, 0, 1).reshape(b * s, d)
        u_ab = _project([h1], w_in, l, 0, ab_w, F32, name="proj_in_ab").reshape(b, s, ab_w)
        qkv = _project([h1], w_in, l, ab_w, in_w - ab_w, BF16, lead_cols=N_HEADS * 2 * DK,
                       lead_scale=LOG2E * DK ** -0.5, name="proj_in_qkv").reshape(b, s, in_w - ab_w)

        za, zb = _fnet_z(u_ab, ab, l)
        ya = _dft_apply(wd, za, zb, b)
        yb = _pool_mixer(u_ab, pool_w, pool_scale3, l, fnet_wd // pool_w.shape[2])
        yc = _diff_attention(qkv, nbt, plan, rel_bias, positions, lam4, sub_g3, l)

        mixo = _project([ya.reshape(b * s, -1), yb.reshape(b * s, -1), yc.reshape(b * s, -1)],
                        w_out, l, 0, d, F32, name="proj_out").reshape(b, s, d)
        x1, h2, logits = _post_mix(x, mixo, modr, norm2_g3, w_router_p, l)

        idx, slott, afft = _route(logits, cap)
        y = _experts(idx, h2, w1, w3, w2, l, cap)
        x = _scatter(slott, afft, y, x1, modr, l)

    return _final_norm(x, final_g.reshape(1, d))
```

```python
import functools
import math

import numpy as np
import jax
import jax.numpy as jnp
from jax import lax
from jax.experimental import pallas as pl
from jax.experimental.pallas import tpu as pltpu

F32 = jnp.float32
BF16 = jnp.bfloat16
I32 = jnp.int32

FNET_GROUPS = 4
POOL_GROUPS = 4
POOL_HALO = 64
N_HEADS = 8
DK = 64
DV = 128
N_BUCKETS = 32
MAX_DISTANCE = 128
N_EXPERTS = 16
EC_CAPACITY = 2
N_MOD = 6
EPS = 1e-6
LANES = 128
LOG2E = 1.4426950408889634
ATT_TQ = 256
ATT_TK = 512
ATT_NS = 2
MOE_NF = 4
MOE_TS = 512
MOE_WIN = 128


def _cparams(sem, vmem_mb=None):
    kw = dict(dimension_semantics=sem)
    if vmem_mb is not None:
        kw["vmem_limit_bytes"] = vmem_mb * 1024 * 1024
    return pltpu.CompilerParams(**kw)


def _silu(x):
    return x * jax.nn.sigmoid(x)


def _mod_kernel(c_ref, w_ref, b_ref, o_ref):
    ca = _silu(c_ref[...]).astype(BF16)
    o_ref[...] = jnp.dot(ca, w_ref[...].astype(BF16), preferred_element_type=F32) + b_ref[...]


def _modulation(c8, w_mod, b_mod3):
    depth, d, n = w_mod.shape
    tn = 1024
    return pl.pallas_call(
        _mod_kernel,
        grid=(depth, n // tn),
        in_specs=[
            pl.BlockSpec((8, d), lambda l, j: (0, 0)),
            pl.BlockSpec((None, d, tn), lambda l, j: (l, 0, j)),
            pl.BlockSpec((None, 1, tn), lambda l, j: (l, 0, j)),
        ],
        out_specs=pl.BlockSpec((None, 8, tn), lambda l, j: (l, 0, j)),
        out_shape=jax.ShapeDtypeStruct((depth, 8, n), F32),
        compiler_params=_cparams(("arbitrary", "arbitrary")),
        name="modulation",
    )(c8, w_mod, b_mod3)


def _norm_mod_kernel(x_ref, g_ref, sc_ref, sh_ref, o_ref):
    x = x_ref[...]
    ms = jnp.mean(x * x, axis=-1, keepdims=True)
    y = x * lax.rsqrt(ms + EPS) * g_ref[...]
    o_ref[...] = (y * (1.0 + sc_ref[...]) + sh_ref[...]).astype(o_ref.dtype)


def _norm_mod(x, g3, modr, l, sh_idx, sc_idx):
    b, s, d = x.shape
    tm = 512
    return pl.pallas_call(
        _norm_mod_kernel,
        grid=(b, s // tm),
        in_specs=[
            pl.BlockSpec((None, tm, d), lambda bi, i: (bi, i, 0)),
            pl.BlockSpec((None, 1, d), lambda bi, i: (l, 0, 0)),
            pl.BlockSpec((None, None, None, 1, d), lambda bi, i: (l, bi, sc_idx, 0, 0)),
            pl.BlockSpec((None, None, None, 1, d), lambda bi, i: (l, bi, sh_idx, 0, 0)),
        ],
        out_specs=pl.BlockSpec((None, tm, d), lambda bi, i: (bi, i, 0)),
        out_shape=jax.ShapeDtypeStruct((b, s, d), BF16),
        compiler_params=_cparams(("arbitrary", "arbitrary")),
        name="norm_mod",
    )(x, g3, modr, modr)


def _final_norm_kernel(x_ref, g_ref, o_ref):
    x = x_ref[...]
    ms = jnp.mean(x * x, axis=-1, keepdims=True)
    o_ref[...] = x * lax.rsqrt(ms + EPS) * g_ref[...]


def _final_norm(x, g2d):
    b, s, d = x.shape
    tm = 512
    return pl.pallas_call(
        _final_norm_kernel,
        grid=(b, s // tm),
        in_specs=[
            pl.BlockSpec((None, tm, d), lambda bi, i: (bi, i, 0)),
            pl.BlockSpec((1, d), lambda bi, i: (0, 0)),
        ],
        out_specs=pl.BlockSpec((None, tm, d), lambda bi, i: (bi, i, 0)),
        out_shape=jax.ShapeDtypeStruct((b, s, d), F32),
        compiler_params=_cparams(("arbitrary", "arbitrary")),
        name="final_norm",
    )(x, g2d)


def _mm_kernel(*refs, k_sizes, lead_blocks, lead_scale):
    n_a = len(k_sizes)
    a_refs = refs[:n_a]
    w_ref, o_ref, wb_ref = refs[n_a], refs[n_a + 1], refs[n_a + 2]

    @pl.when(pl.program_id(1) == 0)
    def _():
        wb_ref[...] = w_ref[...].astype(BF16)

    acc = None
    off = 0
    for a_ref, ks in zip(a_refs, k_sizes):
        part = jnp.dot(a_ref[...], wb_ref[off:off + ks, :], preferred_element_type=F32)
        acc = part if acc is None else acc + part
        off += ks
    if lead_blocks:
        acc = acc * jnp.where(pl.program_id(0) < lead_blocks, lead_scale, 1.0)
    o_ref[...] = acc.astype(o_ref.dtype)


def _project(a_list, w, l, col0, ncols, out_dtype, tm=1024, tn=512, lead_cols=0, lead_scale=1.0,
             name="project"):
    m = a_list[0].shape[0]
    k_sizes = tuple(a.shape[1] for a in a_list)
    k = sum(k_sizes)
    assert w.shape[1] == k and col0 % tn == 0 and ncols % tn == 0 and m % tm == 0 and lead_cols % tn == 0
    cb0 = col0 // tn
    in_specs = [pl.BlockSpec((tm, ks), lambda j, i: (i, 0)) for ks in k_sizes]
    in_specs.append(pl.BlockSpec((None, k, tn), lambda j, i: (l, 0, cb0 + j)))
    return pl.pallas_call(
        functools.partial(_mm_kernel, k_sizes=k_sizes, lead_blocks=lead_cols // tn, lead_scale=lead_scale),
        grid=(ncols // tn, m // tm),
        in_specs=in_specs,
        out_specs=pl.BlockSpec((tm, tn), lambda j, i: (i, j)),
        out_shape=jax.ShapeDtypeStruct((m, ncols), out_dtype),
        scratch_shapes=[pltpu.VMEM((k, tn), BF16)],
        compiler_params=_cparams(("arbitrary", "arbitrary"), 48),
        name=name,
    )(*a_list, w)


def _dft_tables(s):
    sp = np.arange(s, dtype=np.int64)
    a = np.arange(64, dtype=np.int64)[:, None]
    ang1 = 2.0 * np.pi * ((a * sp[None, :]) % 64) / 64.0
    ang2 = 2.0 * np.pi * ((a * sp[None, :]) % s) / float(s)
    t1c = np.cos(ang1).astype(np.float32).reshape(64, 1, s)
    t1s = np.sin(ang1).astype(np.float32).reshape(64, 1, s)
    t2c = np.cos(ang2).astype(np.float32)
    t2s = np.sin(ang2).astype(np.float32)
    return t1c, t1s, t2c, t2s


def _dftgen_kernel(t1c_ref, t1s_ref, t2c_ref, t2s_ref, o_ref):
    s = t2c_ref.shape[1]
    c1, s1 = t1c_ref[...], t1s_ref[...]
    c2, s2 = t2c_ref[...], t2s_ref[...]
    o_ref[:, :s] = (c1 * c2 - s1 * s2).astype(BF16)
    o_ref[:, s:] = (-(s1 * c2 + c1 * s2)).astype(BF16)


def _dft_matrix(s):
    assert s % 64 == 0 and s // 64 == 64
    t1c, t1s, t2c, t2s = _dft_tables(s)
    return pl.pallas_call(
        _dftgen_kernel,
        grid=(64,),
        in_specs=[
            pl.BlockSpec((None, 1, s), lambda a: (a, 0, 0)),
            pl.BlockSpec((None, 1, s), lambda a: (a, 0, 0)),
            pl.BlockSpec((64, s), lambda a: (0, 0)),
            pl.BlockSpec((64, s), lambda a: (0, 0)),
        ],
        out_specs=pl.BlockSpec((64, 2 * s), lambda a: (a, 0)),
        out_shape=jax.ShapeDtypeStruct((s, 2 * s), BF16),
        compiler_params=_cparams(("arbitrary",)),
        name="dft_matrix",
    )(t1c, t1s, t2c, t2s)


def _fnet_w_kernel(cc_ref, sc_ref, w_ref, o_ref, *, norm):
    depth, groups, cg, _ = w_ref.shape
    cc, sc = cc_ref[...], sc_ref[...]
    for l in range(depth):
        for g in range(groups):
            w = w_ref[l, g]
            a = jnp.dot(cc, w, preferred_element_type=F32, precision=lax.Precision.HIGHEST)
            b = jnp.dot(sc, w, preferred_element_type=F32, precision=lax.Precision.HIGHEST)
            o_ref[l, g, :, :cg] = (a * norm).astype(BF16)
            o_ref[l, g, :, cg:] = (b * norm).astype(BF16)


def _fnet_weights(fnet_w, s):
    depth, groups, cg, _ = fnet_w.shape
    idx = np.arange(cg, dtype=np.int64)
    ang = 2.0 * np.pi * ((idx[:, None] * idx[None, :]) % cg) / float(cg)
    cc = np.cos(ang).astype(np.float32)
    sc = np.sin(ang).astype(np.float32)
    norm = 1.0 / math.sqrt(float(s) * float(cg))
    return pl.pallas_call(
        functools.partial(_fnet_w_kernel, norm=norm),
        out_shape=jax.ShapeDtypeStruct((depth, groups, cg, 2 * cg), BF16),
        name="fnet_weights",
    )(cc, sc, fnet_w)


def _fnet_z_kernel(u_ref, ab_ref, za_ref, zb_ref):
    groups, cg = ab_ref.shape[0], ab_ref.shape[1]
    for g in range(groups):
        ug = u_ref[:, g * cg:(g + 1) * cg].astype(BF16)
        z = jnp.dot(ug, ab_ref[g], preferred_element_type=F32)
        za_ref[:, g * cg:(g + 1) * cg] = z[:, :cg].astype(BF16)
        zb_ref[:, g * cg:(g + 1) * cg] = z[:, cg:].astype(BF16)


def _fnet_z(u_ab, ab, l):
    b, s, _ = u_ab.shape
    groups, cg = ab.shape[1], ab.shape[2]
    fw = groups * cg
    ts = 512
    shp = jax.ShapeDtypeStruct((s, b * fw), BF16)
    return pl.pallas_call(
        _fnet_z_kernel,
        grid=(b, s // ts),
        in_specs=[
            pl.BlockSpec((None, ts, fw), lambda bi, i: (bi, i, 0)),
            pl.BlockSpec((None, groups, cg, 2 * cg), lambda bi, i: (l, 0, 0, 0)),
        ],
        out_specs=[pl.BlockSpec((ts, fw), lambda bi, i: (i, bi)),
                   pl.BlockSpec((ts, fw), lambda bi, i: (i, bi))],
        out_shape=[shp, shp],
        compiler_params=_cparams(("arbitrary", "arbitrary")),
        name="fnet_z",
    )(u_ab, ab)


def _dft_apply_kernel(wd_ref, za_ref, zb_ref, o_ref):
    s = za_ref.shape[0]
    acc = jnp.dot(wd_ref[:, :s], za_ref[...], preferred_element_type=F32)
    acc = acc + jnp.dot(wd_ref[:, s:], zb_ref[...], preferred_element_type=F32)
    o_ref[...] = acc.astype(o_ref.dtype)


def _dft_apply(wd, za, zb, b):
    s = wd.shape[0]
    fw = za.shape[1] // b
    tm = 512
    return pl.pallas_call(
        _dft_apply_kernel,
        grid=(b, s // tm),
        in_specs=[
            pl.BlockSpec((tm, 2 * s), lambda bi, i: (i, 0)),
            pl.BlockSpec((s, fw), lambda bi, i: (0, bi)),
            pl.BlockSpec((s, fw), lambda bi, i: (0, bi)),
        ],
        out_specs=pl.BlockSpec((None, tm, fw), lambda bi, i: (bi, i, 0)),
        out_shape=jax.ShapeDtypeStruct((b, s, fw), BF16),
        compiler_params=_cparams(("arbitrary", "arbitrary"), 48),
        name="dft_apply",
    )(wd, za, zb)


def _pool_kernel(u_ref, w_ref, sc_ref, o_ref, pad_ref):
    s, cg = u_ref.shape
    t = 256
    half = jnp.left_shift(jnp.int32(1), pl.program_id(1))
    pad_ref[0:POOL_HALO, :] = jnp.zeros((POOL_HALO, cg), F32)
    pad_ref[s + POOL_HALO:s + 2 * POOL_HALO, :] = jnp.zeros((POOL_HALO, cg), F32)
    pad_ref[POOL_HALO:s + POOL_HALO, :] = u_ref[...]
    ii = lax.broadcasted_iota(I32, (t, t + 2 * POOL_HALO), 0)
    jj = lax.broadcasted_iota(I32, (t, t + 2 * POOL_HALO), 1)
    dlt = jj - ii - POOL_HALO
    band = jnp.where(dlt >= -half, jnp.where(dlt <= half - 1, 1.0, 0.0), 0.0).astype(BF16)
    wb = w_ref[...].astype(BF16)
    scale = sc_ref[...]

    def body(ti, carry):
        r0 = pl.multiple_of(ti * t, t)
        seg = pad_ref[pl.ds(r0, t + 2 * POOL_HALO), :]
        hi = seg.astype(BF16)
        lo = (seg - hi.astype(F32)).astype(BF16)
        win = jnp.dot(band, hi, preferred_element_type=F32) + jnp.dot(band, lo, preferred_element_type=F32)
        gi = r0 + lax.broadcasted_iota(I32, (t, cg), 0)
        lo_i = jnp.maximum(gi - half, 0)
        hi_i = jnp.minimum(gi + half - 1, s - 1)
        cnt = (hi_i - lo_i + 1).astype(F32)
        dmean = win / cnt - seg[POOL_HALO:POOL_HALO + t, :]
        y = jnp.dot(dmean.astype(BF16), wb, preferred_element_type=F32) * scale
        o_ref[pl.ds(r0, t), :] = y.astype(o_ref.dtype)
        return carry

    lax.fori_loop(0, s // t, body, 0)


def _pool_mixer(u_ab, pool_w, pool_scale3, l, col_block0):
    b, s, _ = u_ab.shape
    groups, cg = pool_w.shape[1], pool_w.shape[2]
    return pl.pallas_call(
        _pool_kernel,
        grid=(b, groups),
        in_specs=[
            pl.BlockSpec((None, s, cg), lambda bi, g: (bi, 0, col_block0 + g)),
            pl.BlockSpec((None, None, cg, cg), lambda bi, g: (l, g, 0, 0)),
            pl.BlockSpec((None, 1, cg), lambda bi, g: (l, 0, g)),
        ],
        out_specs=pl.BlockSpec((None, s, cg), lambda bi, g: (bi, 0, g)),
        out_shape=jax.ShapeDtypeStruct((b, s, groups * cg), BF16),
        scratch_shapes=[pltpu.VMEM((s + 2 * POOL_HALO, cg), F32)],
        compiler_params=_cparams(("arbitrary", "arbitrary")),
        name="pool_mixer",
    )(u_ab, pool_w, pool_scale3)


def _bucket(rel):
    nb = N_BUCKETS // 2
    max_exact = nb // 2
    n = jnp.abs(rel)
    nf = jnp.maximum(n, 1).astype(F32)
    large = max_exact + (jnp.log(nf / max_exact) / math.log(MAX_DISTANCE / max_exact)
                         * (nb - max_exact)).astype(I32)
    large = jnp.minimum(large, nb - 1)
    return jnp.where(rel > 0, nb, 0) + jnp.where(n < max_exact, n, large)


def _bias_tile(tab_ref, h, rel):
    bucket = _bucket(rel)
    val = jnp.full(rel.shape, tab_ref[h], F32)
    for j in range(1, N_BUCKETS):
        val = jnp.where(bucket == j, tab_ref[j * N_HEADS + h], val)
    return val * LOG2E


def _near_bias_kernel(slotj_ref, nnear_ref, tabt_ref, pq_ref, pk_ref, o_ref):
    del slotj_ref
    used = jnp.where(pl.program_id(1) < nnear_ref[pl.program_id(0)], 1.0, 0.0)
    bucket = _bucket(pk_ref[...] - pq_ref[...])
    tq, tk = bucket.shape
    for h in range(N_HEADS):
        row = jnp.broadcast_to(tabt_ref[h:h + 1, :], (tq, LANES))
        cols = [jnp.take_along_axis(row, bucket[:, c * LANES:(c + 1) * LANES], axis=1)
                for c in range(tk // LANES)]
        o_ref[h] = jnp.concatenate(cols, axis=1) * used


def _near_bias_table(rel_bias, positions, slotj, nnear):
    s = positions.shape[0]
    nq = s // ATT_TQ
    tabt = jnp.zeros((N_HEADS, LANES), F32).at[:, :N_BUCKETS].set(rel_bias.T * LOG2E)
    grid_spec = pltpu.PrefetchScalarGridSpec(
        num_scalar_prefetch=2,
        grid=(nq, ATT_NS),
        in_specs=[
            pl.BlockSpec((N_HEADS, LANES), lambda i, n, sj, nn: (0, 0)),
            pl.BlockSpec((ATT_TQ, 1), lambda i, n, sj, nn: (i, 0)),
            pl.BlockSpec((1, ATT_TK), lambda i, n, sj, nn: (0, sj[i * ATT_NS + n])),
        ],
        out_specs=pl.BlockSpec((N_HEADS, None, None, ATT_TQ, ATT_TK), lambda i, n, sj, nn: (0, i, n, 0, 0)),
    )
    return pl.pallas_call(
        _near_bias_kernel,
        grid_spec=grid_spec,
        out_shape=jax.ShapeDtypeStruct((N_HEADS, nq, ATT_NS, ATT_TQ, ATT_TK), F32),
        compiler_params=_cparams(("arbitrary", "arbitrary")),
        name="near_bias_table",
    )(slotj, nnear, tabt, positions.reshape(s, 1), positions.reshape(1, s))


def _attn_plan(positions, rel_bias):
    s = positions.shape[0]
    nq, nk = s // ATT_TQ, s // ATT_TK
    pq = positions.reshape(nq, ATT_TQ)
    pk = positions.reshape(nk, ATT_TK)
    rel_min = pk.min(axis=1)[None, :] - pq.max(axis=1)[:, None]
    rel_max = pk.max(axis=1)[None, :] - pq.min(axis=1)[:, None]
    cls = jnp.where(rel_min >= MAX_DISTANCE, 1, jnp.where(rel_max <= -MAX_DISTANCE, 0, 2)).astype(I32)
    near = cls == 2
    nnear = near.sum(axis=1).astype(I32)
    slotj = jnp.argsort(jnp.logical_not(near), axis=1, stable=True)[:, :ATT_NS].astype(I32)
    fits = jnp.all(nnear <= ATT_NS)
    nb = N_BUCKETS // 2
    ctab = jnp.stack([rel_bias[nb - 1], rel_bias[2 * nb - 1], jnp.zeros((N_HEADS,), F32)], axis=1) * LOG2E
    return cls.reshape(-1), slotj.reshape(-1), nnear, ctab.reshape(-1).astype(F32), fits


def _lambda(lq1_ref, lk1_ref, lq2_ref, lk2_ref, lam_init):
    return (jnp.exp(jnp.sum(lq1_ref[...] * lk1_ref[...], axis=-1, keepdims=True))
            - jnp.exp(jnp.sum(lq2_ref[...] * lk2_ref[...], axis=-1, keepdims=True)) + lam_init)


def _stack_maps(q):
    lane = lax.broadcasted_iota(I32, q.shape, 1)
    zero = jnp.zeros_like(q)
    return jnp.concatenate([jnp.where(lane < DK, q, zero), jnp.where(lane >= DK, q, zero)], axis=0)


def _attn_finish(o1, o2, lam, sg, lam_init, dtype):
    o = o1 - lam * o2
    ms = jnp.mean(o * o, axis=-1, keepdims=True)
    y = o * lax.rsqrt(ms + EPS) * sg
    return (y * (1.0 - lam_init)).astype(dtype)


def _attn_kernel(cls_ref, slotj_ref, ctab_ref, lq1_ref, lk1_ref, lq2_ref, lk2_ref,
                 q_ref, k_ref, v_ref, nbp0_ref, nbp1_ref, nba0_ref, nba1_ref, nbb0_ref, nbb1_ref,
                 sg_ref, o_ref, sa_scr, sb_scr, vaug_scr, *, lam_init):
    h = pl.program_id(1)
    ip = pl.program_id(2)
    tq = ATT_TQ
    s = k_ref.shape[0]
    nk = s // ATT_TK
    nq = s // tq
    lam = _lambda(lq1_ref, lk1_ref, lq2_ref, lk2_ref, lam_init)
    sg = sg_ref[...]

    def scores(t, s_scr, nb_refs):
        qs = _stack_maps(q_ref[pl.ds(pl.multiple_of(t * tq, tq), tq), :])
        for j in range(nk):
            cval = ctab_ref[h * 3 + cls_ref[t * nk + j]]
            sc = lax.dot_general(qs, k_ref[j * ATT_TK:(j + 1) * ATT_TK, :], (((1,), (1,)), ((), ())),
                                 preferred_element_type=F32)
            s_scr[j] = sc + cval
        for n, nb_ref in enumerate(nb_refs):
            j = slotj_ref[t * ATT_NS + n]
            bias = nb_ref[...]
            s_scr[j, 0:tq, :] = s_scr[j, 0:tq, :] + bias
            s_scr[j, tq:2 * tq, :] = s_scr[j, tq:2 * tq, :] + bias

    def outputs(s_scr, row0):
        m128 = None
        for j in range(nk):
            blk = s_scr[j]
            for c in range(ATT_TK // LANES):
                part = blk[:, c * LANES:(c + 1) * LANES]
                m128 = part if m128 is None else jnp.maximum(m128, part)
        m = jnp.max(m128, axis=-1, keepdims=True)
        acc = None
        for j in range(nk):
            e = jnp.exp2(s_scr[j] - m).astype(BF16)
            part = jnp.dot(e, vaug_scr[j * ATT_TK:(j + 1) * ATT_TK, :], preferred_element_type=F32)
            acc = part if acc is None else acc + part
        o1 = acc[:tq, :DV] / acc[:tq, DV:DV + 1]
        o2 = acc[tq:, :DV] / acc[tq:, DV:DV + 1]
        o_ref[row0:row0 + tq, :] = _attn_finish(o1, o2, lam, sg, lam_init, o_ref.dtype)

    @pl.when(ip == 0)
    def _():
        vaug_scr[:, :DV] = v_ref[...]
        lane = lax.broadcasted_iota(I32, (s, DV), 1)
        vaug_scr[:, DV:] = jnp.where(lane == 0, 1.0, 0.0).astype(BF16)
        scores(0, sa_scr, (nbp0_ref, nbp1_ref))

    t_odd = 2 * ip + 1
    t_next = jnp.minimum(2 * ip + 2, nq - 1)
    scores(t_odd, sb_scr, (nbb0_ref, nbb1_ref))
    outputs(sa_scr, 0)
    scores(t_next, sa_scr, (nba0_ref, nba1_ref))
    outputs(sb_scr, tq)


def _attn_fast(qkv, nbt, plan, lam4, sub_g3, l):
    b, s, _ = qkv.shape
    cls, slotj, _, ctab, _ = plan
    nq = s // ATT_TQ
    assert nq % 2 == 0 and ATT_NS == 2
    lam_init = 0.8 - 0.6 * math.exp(-0.3 * l)
    smem = pl.BlockSpec(memory_space=pltpu.SMEM)
    lam_specs = [pl.BlockSpec((None, 1, DK), lambda bi, h, ip: (l, 0, 0)) for _ in range(4)]
    nb_tile = (None, None, None, ATT_TQ, ATT_TK)
    nb_specs = ([pl.BlockSpec(nb_tile, lambda bi, h, ip, n=n: (h, 0, n, 0, 0)) for n in range(ATT_NS)]
                + [pl.BlockSpec(nb_tile, lambda bi, h, ip, n=n: (h, jnp.minimum(2 * ip + 2, nq - 1), n, 0, 0))
                   for n in range(ATT_NS)]
                + [pl.BlockSpec(nb_tile, lambda bi, h, ip, n=n: (h, 2 * ip + 1, n, 0, 0)) for n in range(ATT_NS)])
    return pl.pallas_call(
        functools.partial(_attn_kernel, lam_init=lam_init),
        grid=(b, N_HEADS, nq // 2),
        in_specs=[smem, smem, smem] + lam_specs + [
            pl.BlockSpec((None, s, 2 * DK), lambda bi, h, ip: (bi, 0, h)),
            pl.BlockSpec((None, s, 2 * DK), lambda bi, h, ip: (bi, 0, N_HEADS + h)),
            pl.BlockSpec((None, s, DV), lambda bi, h, ip: (bi, 0, 2 * N_HEADS + h)),
        ] + nb_specs + [pl.BlockSpec((None, 1, DV), lambda bi, h, ip: (l, 0, 0))],
        out_specs=pl.BlockSpec((None, 2 * ATT_TQ, DV), lambda bi, h, ip: (bi, ip, h)),
        out_shape=jax.ShapeDtypeStruct((b, s, N_HEADS * DV), BF16),
        scratch_shapes=[pltpu.VMEM((s // ATT_TK, 2 * ATT_TQ, ATT_TK), F32),
                        pltpu.VMEM((s // ATT_TK, 2 * ATT_TQ, ATT_TK), F32),
                        pltpu.VMEM((s, 2 * DV), BF16)],
        compiler_params=_cparams(("arbitrary", "arbitrary", "arbitrary"), 56),
        name="diff_attention",
    )(cls, slotj, ctab, *lam4, qkv, qkv, qkv, nbt, nbt, nbt, nbt, nbt, nbt, sub_g3)


def _attn_any_kernel(tab_ref, lq1_ref, lk1_ref, lq2_ref, lk2_ref, q_ref, k_ref, v_ref, pq_ref, pk_ref,
                     sg_ref, o_ref, *, lam_init):
    h = pl.program_id(1)
    tq = q_ref.shape[0]
    bias = _bias_tile(tab_ref, h, pk_ref[...] - pq_ref[...])
    sc = lax.dot_general(_stack_maps(q_ref[...]), k_ref[...], (((1,), (1,)), ((), ())),
                         preferred_element_type=F32)
    v = v_ref[...]

    def one_map(sm):
        sm = sm + bias
        e = jnp.exp2(sm - jnp.max(sm, axis=-1, keepdims=True))
        den = jnp.sum(e, axis=-1, keepdims=True)
        return jnp.dot(e.astype(BF16), v, preferred_element_type=F32) / den

    lam = _lambda(lq1_ref, lk1_ref, lq2_ref, lk2_ref, lam_init)
    o_ref[...] = _attn_finish(one_map(sc[:tq]), one_map(sc[tq:]), lam, sg_ref[...], lam_init, o_ref.dtype)


def _attn_any(qkv, rel_bias, positions, lam4, sub_g3, l):
    b, s, _ = qkv.shape
    tq = 128
    lam_init = 0.8 - 0.6 * math.exp(-0.3 * l)
    lam_specs = [pl.BlockSpec((None, 1, DK), lambda bi, h, i: (l, 0, 0)) for _ in range(4)]
    return pl.pallas_call(
        functools.partial(_attn_any_kernel, lam_init=lam_init),
        grid=(b, N_HEADS, s // tq),
        in_specs=[pl.BlockSpec(memory_space=pltpu.SMEM)] + lam_specs + [
            pl.BlockSpec((None, tq, 2 * DK), lambda bi, h, i: (bi, i, h)),
            pl.BlockSpec((None, s, 2 * DK), lambda bi, h, i: (bi, 0, N_HEADS + h)),
            pl.BlockSpec((None, s, DV), lambda bi, h, i: (bi, 0, 2 * N_HEADS + h)),
            pl.BlockSpec((tq, 1), lambda bi, h, i: (i, 0)),
            pl.BlockSpec((1, s), lambda bi, h, i: (0, 0)),
            pl.BlockSpec((None, 1, DV), lambda bi, h, i: (l, 0, 0)),
        ],
        out_specs=pl.BlockSpec((None, tq, DV), lambda bi, h, i: (bi, i, h)),
        out_shape=jax.ShapeDtypeStruct((b, s, N_HEADS * DV), BF16),
        compiler_params=_cparams(("arbitrary", "arbitrary", "arbitrary"), 48),
        name="diff_attention_any",
    )(rel_bias.reshape(-1), *lam4, qkv, qkv, qkv, positions.reshape(s, 1), positions.reshape(1, s), sub_g3)


def _diff_attention(qkv, nbt, plan, rel_bias, positions, lam4, sub_g3, l):
    return lax.cond(plan[4],
                    lambda: _attn_fast(qkv, nbt, plan, lam4, sub_g3, l),
                    lambda: _attn_any(qkv, rel_bias, positions, lam4, sub_g3, l))


def _post_mix_kernel(x_ref, mo_ref, g1_ref, ng_ref, sc_ref, sh_ref, wr_ref, x1_ref, h_ref, lg_ref):
    x1 = x_ref[...] + g1_ref[...] * mo_ref[...]
    x1_ref[...] = x1
    ms = jnp.mean(x1 * x1, axis=-1, keepdims=True)
    h = x1 * lax.rsqrt(ms + EPS) * ng_ref[...]
    h = h * (1.0 + sc_ref[...]) + sh_ref[...]
    d = h.shape[1]
    lg = jnp.dot(h, wr_ref[...], preferred_element_type=F32, precision=lax.Precision.HIGHEST)
    lane = lax.broadcasted_iota(I32, lg.shape, 1)
    valid = lane < N_EXPERTS
    lgm = jnp.where(valid, lg, -1e30)
    ex = jnp.where(valid, jnp.exp(lgm - jnp.max(lgm, axis=-1, keepdims=True)), 0.0)
    aff = ex / jnp.sum(ex, axis=-1, keepdims=True)
    h_ref[:, :d] = h
    h_ref[:, d:] = aff
    lg_ref[...] = aff


def _post_mix(x, mixo, modr, norm2_g3, w_router_p, l):
    b, s, d = x.shape
    tm = 256
    mspec = lambda idx: pl.BlockSpec((None, None, None, 1, d), lambda bi, i: (l, bi, idx, 0, 0))
    return pl.pallas_call(
        _post_mix_kernel,
        grid=(b, s // tm),
        in_specs=[
            pl.BlockSpec((None, tm, d), lambda bi, i: (bi, i, 0)),
            pl.BlockSpec((None, tm, d), lambda bi, i: (bi, i, 0)),
            mspec(2),
            pl.BlockSpec((None, 1, d), lambda bi, i: (l, 0, 0)),
            mspec(4),
            mspec(3),
            pl.BlockSpec((None, d, LANES), lambda bi, i: (l, 0, 0)),
        ],
        out_specs=[
            pl.BlockSpec((None, tm, d), lambda bi, i: (bi, i, 0)),
            pl.BlockSpec((None, tm, d + LANES), lambda bi, i: (bi, i, 0)),
            pl.BlockSpec((None, tm, LANES), lambda bi, i: (bi, i, 0)),
        ],
        out_shape=[
            jax.ShapeDtypeStruct((b, s, d), F32),
            jax.ShapeDtypeStruct((b, s, d + LANES), F32),
            jax.ShapeDtypeStruct((b, s, LANES), F32),
        ],
        compiler_params=_cparams(("arbitrary", "arbitrary")),
        name="post_mix",
    )(x, mixo, modr, norm2_g3, modr, modr, w_router_p)


def _cumsum_lanes(x01):
    rows, n = x01.shape
    blk = MOE_TS
    ii = lax.broadcasted_iota(I32, (blk, blk), 0)
    jj = lax.broadcasted_iota(I32, (blk, blk), 1)
    tri = jnp.where(ii <= jj, 1.0, 0.0).astype(BF16)
    xb = x01.astype(BF16)
    carry = jnp.zeros((rows, 1), F32)
    outs, totals = [], [carry]
    for c in range(n // blk):
        part = jnp.dot(xb[:, c * blk:(c + 1) * blk], tri, preferred_element_type=F32) + carry
        outs.append(part)
        carry = part[:, blk - 1:blk]
        totals.append(carry)
    return jnp.concatenate(outs, axis=1), totals


def _route_kernel(aff_ref, idx_ref, slott_ref, lo_ref, cs_scr, *, cap):
    at = aff_ref[...].T[:N_EXPERTS, :]
    keys = lax.bitcast_convert_type(at, I32)

    def body(i, prefix):
        cand = prefix | jnp.left_shift(jnp.int32(1), 30 - i)
        cnt = jnp.sum(jnp.where(keys >= cand, 1.0, 0.0), axis=1, keepdims=True)
        return jnp.where(cnt >= cap, cand, prefix)

    thr = lax.fori_loop(0, 31, body, jnp.zeros((N_EXPERTS, 1), I32))
    gt = jnp.where(keys > thr, 1.0, 0.0)
    eq = jnp.where(keys == thr, 1.0, 0.0)
    need = cap - jnp.sum(gt, axis=1, keepdims=True)
    take = eq * jnp.where(_cumsum_lanes(eq)[0] <= need, 1.0, 0.0)
    sel = gt + take
    cs, totals = _cumsum_lanes(sel)
    slot = jnp.where(sel > 0.5, cs - 1.0, -1.0).astype(I32)
    s = slot.shape[1]
    pad = jnp.full((LANES - N_EXPERTS, s), -1, I32)
    slott_ref[...] = jnp.concatenate([slot, pad], axis=0).T

    lane_e = lax.broadcasted_iota(I32, (N_EXPERTS, LANES), 1)
    lo = jnp.zeros((N_EXPERTS, LANES), F32)
    for j, tot in enumerate(totals):
        lo = jnp.where(lane_e == j, tot, lo)
    lo_ref[...] = lo.astype(I32)

    cs_scr[...] = cs
    lane = lax.broadcasted_iota(I32, (cap, LANES), 1)
    rows = 128

    def one_expert(e, idxt):
        row = cs_scr[pl.ds(e, 1), :]
        cols = []
        for c0 in range(0, cap, rows):
            cio = (lax.broadcasted_iota(I32, (rows, s), 0) + c0).astype(F32)
            cols.append(jnp.sum(jnp.where(row <= cio, 1.0, 0.0), axis=1, keepdims=True))
        return jnp.where(lane == e, jnp.concatenate(cols, axis=0), idxt)

    idxt = lax.fori_loop(0, N_EXPERTS, one_expert, jnp.zeros((cap, LANES), F32))
    idx_ref[...] = idxt.T[:N_EXPERTS, :].astype(I32)


def _route(aff, cap):
    b, s, _ = aff.shape
    return pl.pallas_call(
        functools.partial(_route_kernel, cap=cap),
        grid=(b,),
        in_specs=[pl.BlockSpec((None, s, LANES), lambda bi: (bi, 0, 0))],
        out_specs=[
            pl.BlockSpec((None, N_EXPERTS, cap), lambda bi: (bi, 0, 0)),
            pl.BlockSpec((None, s, LANES), lambda bi: (bi, 0, 0)),
            pl.BlockSpec((None, N_EXPERTS, LANES), lambda bi: (bi, 0, 0)),
        ],
        out_shape=[
            jax.ShapeDtypeStruct((b, N_EXPERTS, cap), I32),
            jax.ShapeDtypeStruct((b, s, LANES), I32),
            jax.ShapeDtypeStruct((b, N_EXPERTS, LANES), I32),
        ],
        scratch_shapes=[pltpu.VMEM((N_EXPERTS, s), F32)],
        compiler_params=_cparams(("arbitrary",), 48),
        name="route",
    )(aff)


def _expert_kernel(idx_ref, h_ref, w1_ref, w3_ref, w2_ref, y_ref, rows_ref, xb_ref, acc_ref, gate_ref, sem):
    e = pl.program_id(0)
    f = pl.program_id(1)
    ne = pl.num_programs(0)
    nf = pl.num_programs(1)
    nb, cap, d = y_ref.shape
    n_rows = nb * cap
    per_step = n_rows // MOE_NF

    steps_per_batch = cap // per_step
    assert steps_per_batch & (steps_per_batch - 1) == 0 and steps_per_batch * per_step == cap

    def row_copy(bi, r, t):
        return pltpu.make_async_copy(h_ref.at[bi, pl.ds(t, 1), :], rows_ref.at[pl.ds(r, 1), :], sem)

    def wait_all_rows():
        pltpu.make_async_copy(h_ref.at[0, pl.ds(0, n_rows), :], rows_ref, sem).wait()

    @pl.when(jnp.logical_and(e == 0, f == 0))
    def _():
        for bi in range(nb):
            def body(c, carry, bi=bi):
                row_copy(bi, bi * cap + c, idx_ref[bi * N_EXPERTS * cap + c]).start()
                return carry
            lax.fori_loop(0, cap, body, 0)

    @pl.when(f == 0)
    def _():
        wait_all_rows()
        xb_ref[...] = rows_ref[:, :d].astype(BF16)
        lane = lax.broadcasted_iota(I32, (n_rows, LANES), 1)
        gate_ref[...] = jnp.sum(jnp.where(lane == e, rows_ref[:, d:], 0.0), axis=1, keepdims=True)

    e_next = jnp.minimum(e + 1, ne - 1)
    bi = lax.shift_right_logical(f, steps_per_batch.bit_length() - 1)
    c0 = (f & (steps_per_batch - 1)) * per_step
    src0 = (bi * N_EXPERTS + e_next) * cap + c0
    dst0 = f * per_step
    for r in range(per_step):
        row_copy(bi, dst0 + r, idx_ref[src0 + r]).start()

    xg = xb_ref[...]
    a = jnp.dot(xg, w1_ref[...].astype(BF16), preferred_element_type=F32)
    g = jnp.dot(xg, w3_ref[...].astype(BF16), preferred_element_type=F32)
    act = (_silu(a) * g).astype(BF16)
    part = jnp.dot(act, w2_ref[...].astype(BF16), preferred_element_type=F32)

    @pl.when(f == 0)
    def _():
        acc_ref[...] = part

    @pl.when(f > 0)
    def _():
        acc_ref[...] += part

    @pl.when(f == nf - 1)
    def _():
        y_ref[...] = (acc_ref[...] * gate_ref[...]).reshape(nb, cap, d).astype(y_ref.dtype)

    @pl.when(jnp.logical_and(e == ne - 1, f == nf - 1))
    def _():
        wait_all_rows()


def _experts(idx, h2, w1, w3, w2, l, cap):
    b, s, dext = h2.shape
    d = dext - LANES
    e = w1.shape[1]
    ff = w1.shape[-1]
    fc = ff // MOE_NF
    grid_spec = pltpu.PrefetchScalarGridSpec(
        num_scalar_prefetch=1,
        grid=(e, MOE_NF),
        in_specs=[
            pl.BlockSpec(memory_space=pl.ANY),
            pl.BlockSpec((None, None, d, fc), lambda ei, f, ix: (l, ei, 0, f)),
            pl.BlockSpec((None, None, d, fc), lambda ei, f, ix: (l, ei, 0, f)),
            pl.BlockSpec((None, None, fc, d), lambda ei, f, ix: (l, ei, f, 0)),
        ],
        out_specs=pl.BlockSpec((b, None, cap, d), lambda ei, f, ix: (0, ei, 0, 0)),
        scratch_shapes=[pltpu.VMEM((b * cap, dext), F32), pltpu.VMEM((b * cap, d), BF16),
                        pltpu.VMEM((b * cap, d), F32), pltpu.VMEM((b * cap, 1), F32),
                        pltpu.SemaphoreType.DMA],
    )
    return pl.pallas_call(
        _expert_kernel,
        grid_spec=grid_spec,
        out_shape=jax.ShapeDtypeStruct((b, e, cap, d), BF16),
        compiler_params=_cparams(("arbitrary", "arbitrary"), 56),
        name="moe_experts",
    )(idx.reshape(-1), h2, w1, w3, w2)


def _scatter_kernel(lo_ref, slott_ref, y_ref, x1_ref, g2_ref, o_ref, win_ref, full_ref, oh_ref, wsem, fsem):
    bi = pl.program_id(0)
    ti = pl.program_id(1)
    nb = pl.num_programs(0)
    nt = pl.num_programs(1)
    ts = slott_ref.shape[0]
    cap = y_ref.shape[2]
    step = bi * nt + ti
    buf = step & 1

    def lo_at(b_, t_, e):
        return lo_ref[(b_ * N_EXPERTS + e) * LANES + t_]

    def win_start(b_, t_, e):
        start = lax.shift_left(lax.shift_right_logical(lo_at(b_, t_, e), 4), 4)
        return pl.multiple_of(jnp.minimum(start, cap - MOE_WIN), 16)

    def win_copy(b_, t_, e, slot):
        return pltpu.make_async_copy(y_ref.at[b_, e, pl.ds(win_start(b_, t_, e), MOE_WIN), :],
                                     win_ref.at[slot, pl.ds(e * MOE_WIN, MOE_WIN), :], wsem.at[slot])

    @pl.when(step == 0)
    def _():
        for e in range(N_EXPERTS):
            win_copy(0, 0, e, 0).start()

    @pl.when(step + 1 < nb * nt)
    def _():
        wrap = ti + 1 == nt
        b_n = jnp.where(wrap, bi + 1, bi)
        t_n = jnp.where(wrap, 0, ti + 1)
        for e in range(N_EXPERTS):
            win_copy(b_n, t_n, e, 1 - buf).start()

    for e in range(N_EXPERTS):
        win_copy(bi, ti, e, buf).wait()

    lane = lax.broadcasted_iota(I32, (ts, LANES), 1)

    def slot_col(e):
        return jnp.sum(jnp.where(lane == e, slott_ref[...].astype(F32), 0.0), axis=1, keepdims=True).astype(I32)

    fits = None
    for e in range(N_EXPERTS):
        ok = lo_at(bi, ti + 1, e) - win_start(bi, ti, e) <= MOE_WIN
        fits = ok if fits is None else jnp.logical_and(fits, ok)

    @pl.when(fits)
    def _():
        widx = lax.broadcasted_iota(I32, (ts, MOE_WIN), 1)
        for e in range(N_EXPERTS):
            oh_ref[:, e * MOE_WIN:(e + 1) * MOE_WIN] = jnp.where(
                widx == slot_col(e) - win_start(bi, ti, e), 1.0, 0.0).astype(BF16)
        moe = jnp.dot(oh_ref[...], win_ref[buf], preferred_element_type=F32)
        o_ref[...] = x1_ref[...] + g2_ref[...] * moe

    @pl.when(jnp.logical_not(fits))
    def _():
        o_ref[...] = x1_ref[...]
        cidx = lax.broadcasted_iota(I32, (ts, cap), 1)

        def one_expert(e, carry):
            cp = pltpu.make_async_copy(y_ref.at[bi, e], full_ref, fsem)
            cp.start()
            cp.wait()
            onehot = jnp.where(cidx == slot_col(e), 1.0, 0.0).astype(BF16)
            o_ref[...] += g2_ref[...] * jnp.dot(onehot, full_ref[...], preferred_element_type=F32)
            return carry

        lax.fori_loop(0, N_EXPERTS, one_expert, 0)


def _scatter(lo, slott, y, x1, modr, l):
    b, s, d = x1.shape
    cap = y.shape[2]
    assert cap >= MOE_WIN and s // MOE_TS < LANES
    grid_spec = pltpu.PrefetchScalarGridSpec(
        num_scalar_prefetch=1,
        grid=(b, s // MOE_TS),
        in_specs=[
            pl.BlockSpec((None, MOE_TS, LANES), lambda bi, i, lo_: (bi, i, 0)),
            pl.BlockSpec(memory_space=pl.ANY),
            pl.BlockSpec((None, MOE_TS, d), lambda bi, i, lo_: (bi, i, 0)),
            pl.BlockSpec((None, None, None, 1, d), lambda bi, i, lo_: (l, bi, 5, 0, 0)),
        ],
        out_specs=pl.BlockSpec((None, MOE_TS, d), lambda bi, i, lo_: (bi, i, 0)),
        scratch_shapes=[pltpu.VMEM((2, N_EXPERTS * MOE_WIN, d), BF16), pltpu.VMEM((cap, d), BF16),
                        pltpu.VMEM((MOE_TS, N_EXPERTS * MOE_WIN), BF16),
                        pltpu.SemaphoreType.DMA((2,)), pltpu.SemaphoreType.DMA],
    )
    return pl.pallas_call(
        _scatter_kernel,
        grid_spec=grid_spec,
        out_shape=jax.ShapeDtypeStruct((b, s, d), F32),
        compiler_params=_cparams(("arbitrary", "arbitrary"), 56),
        name="moe_scatter",
    )(lo.reshape(-1), slott, y, x1, modr)


def kernel(x, c, positions, w_mod, b_mod, norm1_g, norm2_g, w_in, fnet_w, pool_w, pool_scale,
           lam_q1, lam_k1, lam_q2, lam_k2, sub_g, rel_bias, w_out, w_router, w1, w3, w2, final_g):
    b, s, d = x.shape
    depth = w_mod.shape[0]
    fnet_wd = fnet_w.shape[1] * fnet_w.shape[2]
    pool_wd = pool_w.shape[1] * pool_w.shape[2]
    ab_w = fnet_wd + pool_wd
    in_w = w_in.shape[-1]
    cap = EC_CAPACITY * s // N_EXPERTS

    c8 = jnp.zeros((8, d), F32).at[:b].set(c)
    mod = _modulation(c8, w_mod, b_mod.reshape(depth, 1, N_MOD * d))
    modr = mod[:, :b].reshape(depth, b, N_MOD, 1, d)

    norm1_g3 = norm1_g.reshape(depth, 1, d)
    norm2_g3 = norm2_g.reshape(depth, 1, d)
    pool_scale3 = pool_scale.reshape(depth, 1, pool_wd)
    sub_g3 = sub_g.reshape(depth, 1, DV)
    lam4 = [a.reshape(depth, 1, DK) for a in (lam_q1, lam_k1, lam_q2, lam_k2)]
    w_router_p = jnp.zeros((depth, d, LANES), F32).at[:, :, :N_EXPERTS].set(w_router)

    wd = _dft_matrix(s)
    ab = _fnet_weights(fnet_w, s)
    plan = _attn_plan(positions, rel_bias)
    nbt = _near_bias_table(rel_bias, positions, plan[1], plan[2])

    for l in range(depth):
        h1 = _norm_mod(x, norm1_g3, modr, l, 0, 1).reshape(b * s, d)
        u_ab = _project([h1], w_in, l, 0, ab_w, F32, name="proj_in_ab").reshape(b, s, ab_w)
        qkv = _project([h1], w_in, l, ab_w, in_w - ab_w, BF16, lead_cols=N_HEADS * 2 * DK,
                       lead_scale=LOG2E * DK ** -0.5, name="proj_in_qkv").reshape(b, s, in_w - ab_w)

        za, zb = _fnet_z(u_ab, ab, l)
        ya = _dft_apply(wd, za, zb, b)
        yb = _pool_mixer(u_ab, pool_w, pool_scale3, l, fnet_wd // pool_w.shape[2])
        yc = _diff_attention(qkv, nbt, plan, rel_bias, positions, lam4, sub_g3, l)

        mixo = _project([ya.reshape(b * s, -1), yb.reshape(b * s, -1), yc.reshape(b * s, -1)],
                        w_out, l, 0, d, F32, name="proj_out").reshape(b, s, d)
        x1, h2, aff = _post_mix(x, mixo, modr, norm2_g3, w_router_p, l)

        idx, slott, lo = _route(aff, cap)
        y = _experts(idx, h2, w1, w3, w2, l, cap)
        x = _scatter(lo, slott, y, x1, modr, l)

    return _final_norm(x, final_g.reshape(1, d))
```

```python
import functools
import math

import numpy as np
import jax
import jax.numpy as jnp
from jax import lax
from jax.experimental import pallas as pl
from jax.experimental.pallas import tpu as pltpu

F32 = jnp.float32
BF16 = jnp.bfloat16
I32 = jnp.int32

FNET_GROUPS = 4
POOL_GROUPS = 4
POOL_HALO = 64
N_HEADS = 8
DK = 64
DV = 128
N_BUCKETS = 32
MAX_DISTANCE = 128
N_EXPERTS = 16
EC_CAPACITY = 2
N_MOD = 6
EPS = 1e-6
LANES = 128
LOG2E = 1.4426950408889634
ATT_TQ = 256
ATT_TK = 512
ATT_NS = 2
MOE_NF = 4
MOE_TS = 512
MOE_WIN = 128


def _cparams(sem, vmem_mb=None):
    kw = dict(dimension_semantics=sem)
    if vmem_mb is not None:
        kw["vmem_limit_bytes"] = vmem_mb * 1024 * 1024
    return pltpu.CompilerParams(**kw)


def _silu(x):
    return x * jax.nn.sigmoid(x)


def _mod_kernel(c_ref, w_ref, b_ref, o_ref):
    ca = _silu(c_ref[...]).astype(BF16)
    o_ref[...] = jnp.dot(ca, w_ref[...].astype(BF16), preferred_element_type=F32) + b_ref[...]


def _modulation(c8, w_mod, b_mod3):
    depth, d, n = w_mod.shape
    tn = 1024
    return pl.pallas_call(
        _mod_kernel,
        grid=(depth, n // tn),
        in_specs=[
            pl.BlockSpec((8, d), lambda l, j: (0, 0)),
            pl.BlockSpec((None, d, tn), lambda l, j: (l, 0, j)),
            pl.BlockSpec((None, 1, tn), lambda l, j: (l, 0, j)),
        ],
        out_specs=pl.BlockSpec((None, 8, tn), lambda l, j: (l, 0, j)),
        out_shape=jax.ShapeDtypeStruct((depth, 8, n), F32),
        compiler_params=_cparams(("arbitrary", "arbitrary")),
        name="modulation",
    )(c8, w_mod, b_mod3)


def _norm_mod_kernel(x_ref, g_ref, sc_ref, sh_ref, o_ref):
    x = x_ref[...]
    ms = jnp.mean(x * x, axis=-1, keepdims=True)
    y = x * lax.rsqrt(ms + EPS) * g_ref[...]
    o_ref[...] = (y * (1.0 + sc_ref[...]) + sh_ref[...]).astype(o_ref.dtype)


def _norm_mod(x, g3, modr, l, sh_idx, sc_idx):
    b, s, d = x.shape
    tm = 512
    return pl.pallas_call(
        _norm_mod_kernel,
        grid=(b, s // tm),
        in_specs=[
            pl.BlockSpec((None, tm, d), lambda bi, i: (bi, i, 0)),
            pl.BlockSpec((None, 1, d), lambda bi, i: (l, 0, 0)),
            pl.BlockSpec((None, None, None, 1, d), lambda bi, i: (l, bi, sc_idx, 0, 0)),
            pl.BlockSpec((None, None, None, 1, d), lambda bi, i: (l, bi, sh_idx, 0, 0)),
        ],
        out_specs=pl.BlockSpec((None, tm, d), lambda bi, i: (bi, i, 0)),
        out_shape=jax.ShapeDtypeStruct((b, s, d), BF16),
        compiler_params=_cparams(("arbitrary", "arbitrary")),
        name="norm_mod",
    )(x, g3, modr, modr)


def _mm_kernel(*refs, k_sizes, lead_blocks, lead_scale):
    n_a = len(k_sizes)
    a_refs = refs[:n_a]
    w_ref, o_ref, wb_ref = refs[n_a], refs[n_a + 1], refs[n_a + 2]

    @pl.when(pl.program_id(1) == 0)
    def _():
        wb_ref[...] = w_ref[...].astype(BF16)

    acc = None
    off = 0
    for a_ref, ks in zip(a_refs, k_sizes):
        part = jnp.dot(a_ref[...], wb_ref[off:off + ks, :], preferred_element_type=F32)
        acc = part if acc is None else acc + part
        off += ks
    if lead_blocks:
        acc = acc * jnp.where(pl.program_id(0) < lead_blocks, lead_scale, 1.0)
    o_ref[...] = acc.astype(o_ref.dtype)


def _project(a_list, w, l, col0, ncols, out_dtype, tm=1024, tn=512, lead_cols=0, lead_scale=1.0,
             name="project"):
    m = a_list[0].shape[0]
    k_sizes = tuple(a.shape[1] for a in a_list)
    k = sum(k_sizes)
    assert w.shape[1] == k and col0 % tn == 0 and ncols % tn == 0 and m % tm == 0 and lead_cols % tn == 0
    cb0 = col0 // tn
    in_specs = [pl.BlockSpec((tm, ks), lambda j, i: (i, 0)) for ks in k_sizes]
    in_specs.append(pl.BlockSpec((None, k, tn), lambda j, i: (l, 0, cb0 + j)))
    return pl.pallas_call(
        functools.partial(_mm_kernel, k_sizes=k_sizes, lead_blocks=lead_cols // tn, lead_scale=lead_scale),
        grid=(ncols // tn, m // tm),
        in_specs=in_specs,
        out_specs=pl.BlockSpec((tm, tn), lambda j, i: (i, j)),
        out_shape=jax.ShapeDtypeStruct((m, ncols), out_dtype),
        scratch_shapes=[pltpu.VMEM((k, tn), BF16)],
        compiler_params=_cparams(("arbitrary", "arbitrary"), 48),
        name=name,
    )(*a_list, w)


def _dft_tables(s):
    sp = np.arange(s, dtype=np.int64)
    a = np.arange(64, dtype=np.int64)[:, None]
    ang1 = 2.0 * np.pi * ((a * sp[None, :]) % 64) / 64.0
    ang2 = 2.0 * np.pi * ((a * sp[None, :]) % s) / float(s)
    t1c = np.cos(ang1).astype(np.float32).reshape(64, 1, s)
    t1s = np.sin(ang1).astype(np.float32).reshape(64, 1, s)
    t2c = np.cos(ang2).astype(np.float32)
    t2s = np.sin(ang2).astype(np.float32)
    return t1c, t1s, t2c, t2s


def _dftgen_kernel(t1c_ref, t1s_ref, t2c_ref, t2s_ref, o_ref):
    s = t2c_ref.shape[1]
    c1, s1 = t1c_ref[...], t1s_ref[...]
    c2, s2 = t2c_ref[...], t2s_ref[...]
    o_ref[:, :s] = (c1 * c2 - s1 * s2).astype(BF16)
    o_ref[:, s:] = (-(s1 * c2 + c1 * s2)).astype(BF16)


def _dft_matrix(s):
    assert s % 64 == 0 and s // 64 == 64
    t1c, t1s, t2c, t2s = _dft_tables(s)
    return pl.pallas_call(
        _dftgen_kernel,
        grid=(64,),
        in_specs=[
            pl.BlockSpec((None, 1, s), lambda a: (a, 0, 0)),
            pl.BlockSpec((None, 1, s), lambda a: (a, 0, 0)),
            pl.BlockSpec((64, s), lambda a: (0, 0)),
            pl.BlockSpec((64, s), lambda a: (0, 0)),
        ],
        out_specs=pl.BlockSpec((64, 2 * s), lambda a: (a, 0)),
        out_shape=jax.ShapeDtypeStruct((s, 2 * s), BF16),
        compiler_params=_cparams(("arbitrary",)),
        name="dft_matrix",
    )(t1c, t1s, t2c, t2s)


def _fnet_w_kernel(cc_ref, sc_ref, w_ref, o_ref, *, norm):
    depth, groups, cg, _ = w_ref.shape
    cc, sc = cc_ref[...], sc_ref[...]
    for l in range(depth):
        for g in range(groups):
            w = w_ref[l, g]
            a = jnp.dot(cc, w, preferred_element_type=F32, precision=lax.Precision.HIGHEST)
            b = jnp.dot(sc, w, preferred_element_type=F32, precision=lax.Precision.HIGHEST)
            o_ref[l, g, :, :cg] = (a * norm).astype(BF16)
            o_ref[l, g, :, cg:] = (b * norm).astype(BF16)


def _fnet_weights(fnet_w, s):
    depth, groups, cg, _ = fnet_w.shape
    idx = np.arange(cg, dtype=np.int64)
    ang = 2.0 * np.pi * ((idx[:, None] * idx[None, :]) % cg) / float(cg)
    cc = np.cos(ang).astype(np.float32)
    sc = np.sin(ang).astype(np.float32)
    norm = 1.0 / math.sqrt(float(s) * float(cg))
    return pl.pallas_call(
        functools.partial(_fnet_w_kernel, norm=norm),
        out_shape=jax.ShapeDtypeStruct((depth, groups, cg, 2 * cg), BF16),
        name="fnet_weights",
    )(cc, sc, fnet_w)


def _fnet_z_kernel(u_ref, ab_ref, za_ref, zb_ref):
    groups, cg = ab_ref.shape[0], ab_ref.shape[1]
    for g in range(groups):
        ug = u_ref[:, g * cg:(g + 1) * cg].astype(BF16)
        z = jnp.dot(ug, ab_ref[g], preferred_element_type=F32)
        za_ref[:, g * cg:(g + 1) * cg] = z[:, :cg].astype(BF16)
        zb_ref[:, g * cg:(g + 1) * cg] = z[:, cg:].astype(BF16)


def _fnet_z(u_ab, ab, l):
    b, s, _ = u_ab.shape
    groups, cg = ab.shape[1], ab.shape[2]
    fw = groups * cg
    ts = 512
    shp = jax.ShapeDtypeStruct((s, b * fw), BF16)
    return pl.pallas_call(
        _fnet_z_kernel,
        grid=(b, s // ts),
        in_specs=[
            pl.BlockSpec((None, ts, fw), lambda bi, i: (bi, i, 0)),
            pl.BlockSpec((None, groups, cg, 2 * cg), lambda bi, i: (l, 0, 0, 0)),
        ],
        out_specs=[pl.BlockSpec((ts, fw), lambda bi, i: (i, bi)),
                   pl.BlockSpec((ts, fw), lambda bi, i: (i, bi))],
        out_shape=[shp, shp],
        compiler_params=_cparams(("arbitrary", "arbitrary")),
        name="fnet_z",
    )(u_ab, ab)


def _dft_apply_kernel(wd_ref, za_ref, zb_ref, o_ref):
    s = za_ref.shape[0]
    acc = jnp.dot(wd_ref[:, :s], za_ref[...], preferred_element_type=F32)
    acc = acc + jnp.dot(wd_ref[:, s:], zb_ref[...], preferred_element_type=F32)
    o_ref[...] = acc.astype(o_ref.dtype)


def _dft_apply(wd, za, zb, b):
    s = wd.shape[0]
    fw = za.shape[1] // b
    tm = 512
    return pl.pallas_call(
        _dft_apply_kernel,
        grid=(b, s // tm),
        in_specs=[
            pl.BlockSpec((tm, 2 * s), lambda bi, i: (i, 0)),
            pl.BlockSpec((s, fw), lambda bi, i: (0, bi)),
            pl.BlockSpec((s, fw), lambda bi, i: (0, bi)),
        ],
        out_specs=pl.BlockSpec((None, tm, fw), lambda bi, i: (bi, i, 0)),
        out_shape=jax.ShapeDtypeStruct((b, s, fw), BF16),
        compiler_params=_cparams(("arbitrary", "arbitrary"), 48),
        name="dft_apply",
    )(wd, za, zb)


def _pool_kernel(u_ref, w_ref, sc_ref, o_ref, pad_ref):
    s, cg = u_ref.shape
    t = 256
    half = jnp.left_shift(jnp.int32(1), pl.program_id(1))
    pad_ref[0:POOL_HALO, :] = jnp.zeros((POOL_HALO, cg), F32)
    pad_ref[s + POOL_HALO:s + 2 * POOL_HALO, :] = jnp.zeros((POOL_HALO, cg), F32)
    pad_ref[POOL_HALO:s + POOL_HALO, :] = u_ref[...]
    ii = lax.broadcasted_iota(I32, (t, t + 2 * POOL_HALO), 0)
    jj = lax.broadcasted_iota(I32, (t, t + 2 * POOL_HALO), 1)
    dlt = jj - ii - POOL_HALO
    band = jnp.where(dlt >= -half, jnp.where(dlt <= half - 1, 1.0, 0.0), 0.0).astype(BF16)
    wb = w_ref[...].astype(BF16)
    scale = sc_ref[...]

    def body(ti, carry):
        r0 = pl.multiple_of(ti * t, t)
        seg = pad_ref[pl.ds(r0, t + 2 * POOL_HALO), :]
        hi = seg.astype(BF16)
        lo = (seg - hi.astype(F32)).astype(BF16)
        win = jnp.dot(band, hi, preferred_element_type=F32) + jnp.dot(band, lo, preferred_element_type=F32)
        gi = r0 + lax.broadcasted_iota(I32, (t, cg), 0)
        lo_i = jnp.maximum(gi - half, 0)
        hi_i = jnp.minimum(gi + half - 1, s - 1)
        cnt = (hi_i - lo_i + 1).astype(F32)
        dmean = win / cnt - seg[POOL_HALO:POOL_HALO + t, :]
        y = jnp.dot(dmean.astype(BF16), wb, preferred_element_type=F32) * scale
        o_ref[pl.ds(r0, t), :] = y.astype(o_ref.dtype)
        return carry

    lax.fori_loop(0, s // t, body, 0)


def _pool_mixer(u_ab, pool_w, pool_scale3, l, col_block0):
    b, s, _ = u_ab.shape
    groups, cg = pool_w.shape[1], pool_w.shape[2]
    return pl.pallas_call(
        _pool_kernel,
        grid=(b, groups),
        in_specs=[
            pl.BlockSpec((None, s, cg), lambda bi, g: (bi, 0, col_block0 + g)),
            pl.BlockSpec((None, None, cg, cg), lambda bi, g: (l, g, 0, 0)),
            pl.BlockSpec((None, 1, cg), lambda bi, g: (l, 0, g)),
        ],
        out_specs=pl.BlockSpec((None, s, cg), lambda bi, g: (bi, 0, g)),
        out_shape=jax.ShapeDtypeStruct((b, s, groups * cg), BF16),
        scratch_shapes=[pltpu.VMEM((s + 2 * POOL_HALO, cg), F32)],
        compiler_params=_cparams(("arbitrary", "arbitrary")),
        name="pool_mixer",
    )(u_ab, pool_w, pool_scale3)


def _bucket(rel):
    nb = N_BUCKETS // 2
    max_exact = nb // 2
    n = jnp.abs(rel)
    nf = jnp.maximum(n, 1).astype(F32)
    large = max_exact + (jnp.log(nf / max_exact) / math.log(MAX_DISTANCE / max_exact)
                         * (nb - max_exact)).astype(I32)
    large = jnp.minimum(large, nb - 1)
    return jnp.where(rel > 0, nb, 0) + jnp.where(n < max_exact, n, large)


def _bias_tile(tab_ref, h, rel):
    bucket = _bucket(rel)
    val = jnp.full(rel.shape, tab_ref[h], F32)
    for j in range(1, N_BUCKETS):
        val = jnp.where(bucket == j, tab_ref[j * N_HEADS + h], val)
    return val * LOG2E


def _near_bias_kernel(slotj_ref, nnear_ref, tabt_ref, pq_ref, pk_ref, o_ref):
    del slotj_ref
    used = jnp.where(pl.program_id(1) < nnear_ref[pl.program_id(0)], 1.0, 0.0)
    bucket = _bucket(pk_ref[...] - pq_ref[...])
    tq, tk = bucket.shape
    for h in range(N_HEADS):
        row = jnp.broadcast_to(tabt_ref[h:h + 1, :], (tq, LANES))
        cols = [jnp.take_along_axis(row, bucket[:, c * LANES:(c + 1) * LANES], axis=1)
                for c in range(tk // LANES)]
        o_ref[h] = jnp.concatenate(cols, axis=1) * used


def _near_bias_table(rel_bias, positions, slotj, nnear):
    s = positions.shape[0]
    nq = s // ATT_TQ
    tabt = jnp.zeros((N_HEADS, LANES), F32).at[:, :N_BUCKETS].set(rel_bias.T * LOG2E)
    grid_spec = pltpu.PrefetchScalarGridSpec(
        num_scalar_prefetch=2,
        grid=(nq, ATT_NS),
        in_specs=[
            pl.BlockSpec((N_HEADS, LANES), lambda i, n, sj, nn: (0, 0)),
            pl.BlockSpec((ATT_TQ, 1), lambda i, n, sj, nn: (i, 0)),
            pl.BlockSpec((1, ATT_TK), lambda i, n, sj, nn: (0, sj[i * ATT_NS + n])),
        ],
        out_specs=pl.BlockSpec((N_HEADS, None, None, ATT_TQ, ATT_TK), lambda i, n, sj, nn: (0, i, n, 0, 0)),
    )
    return pl.pallas_call(
        _near_bias_kernel,
        grid_spec=grid_spec,
        out_shape=jax.ShapeDtypeStruct((N_HEADS, nq, ATT_NS, ATT_TQ, ATT_TK), F32),
        compiler_params=_cparams(("arbitrary", "arbitrary")),
        name="near_bias_table",
    )(slotj, nnear, tabt, positions.reshape(s, 1), positions.reshape(1, s))


def _attn_plan(positions, rel_bias):
    s = positions.shape[0]
    nq, nk = s // ATT_TQ, s // ATT_TK
    pq = positions.reshape(nq, ATT_TQ)
    pk = positions.reshape(nk, ATT_TK)
    rel_min = pk.min(axis=1)[None, :] - pq.max(axis=1)[:, None]
    rel_max = pk.max(axis=1)[None, :] - pq.min(axis=1)[:, None]
    cls = jnp.where(rel_min >= MAX_DISTANCE, 1, jnp.where(rel_max <= -MAX_DISTANCE, 0, 2)).astype(I32)
    near = cls == 2
    nnear = near.sum(axis=1).astype(I32)
    slotj = jnp.argsort(jnp.logical_not(near), axis=1, stable=True)[:, :ATT_NS].astype(I32)
    fits = jnp.all(nnear <= ATT_NS)
    nb = N_BUCKETS // 2
    ctab = jnp.stack([rel_bias[nb - 1], rel_bias[2 * nb - 1], jnp.zeros((N_HEADS,), F32)], axis=1) * LOG2E
    return cls.reshape(-1), slotj.reshape(-1), nnear, ctab.reshape(-1).astype(F32), fits


def _lambda(lq1_ref, lk1_ref, lq2_ref, lk2_ref, lam_init):
    return (jnp.exp(jnp.sum(lq1_ref[...] * lk1_ref[...], axis=-1, keepdims=True))
            - jnp.exp(jnp.sum(lq2_ref[...] * lk2_ref[...], axis=-1, keepdims=True)) + lam_init)


def _stack_maps(q):
    lane = lax.broadcasted_iota(I32, q.shape, 1)
    zero = jnp.zeros_like(q)
    return jnp.concatenate([jnp.where(lane < DK, q, zero), jnp.where(lane >= DK, q, zero)], axis=0)


def _attn_finish(o1, o2, lam, sg, lam_init, dtype):
    o = o1 - lam * o2
    ms = jnp.mean(o * o, axis=-1, keepdims=True)
    y = o * lax.rsqrt(ms + EPS) * sg
    return (y * (1.0 - lam_init)).astype(dtype)


def _attn_kernel(cls_ref, slotj_ref, ctab_ref, lq1_ref, lk1_ref, lq2_ref, lk2_ref,
                 q_ref, k_ref, v_ref, nbp0_ref, nbp1_ref, nba0_ref, nba1_ref, nbb0_ref, nbb1_ref,
                 sg_ref, o_ref, sa_scr, sb_scr, ma_scr, mb_scr, vaug_scr, *, lam_init):
    h = pl.program_id(1)
    ip = pl.program_id(2)
    tq = ATT_TQ
    s = k_ref.shape[0]
    nk = s // ATT_TK
    nq = s // tq
    lam = _lambda(lq1_ref, lk1_ref, lq2_ref, lk2_ref, lam_init)
    sg = sg_ref[...]

    def lane_tile_max(blk):
        out = blk[:, :LANES]
        for c in range(1, blk.shape[1] // LANES):
            out = jnp.maximum(out, blk[:, c * LANES:(c + 1) * LANES])
        return out

    def scores(t, s_scr, m_scr, nb_refs):
        qs = _stack_maps(q_ref[pl.ds(pl.multiple_of(t * tq, tq), tq), :])
        mrow = None
        for j in range(nk):
            c = cls_ref[t * nk + j]
            sc = lax.dot_general(qs, k_ref[j * ATT_TK:(j + 1) * ATT_TK, :], (((1,), (1,)), ((), ())),
                                 preferred_element_type=F32) + ctab_ref[h * 3 + c]
            s_scr[j] = sc
            cm = lane_tile_max(sc) + jnp.where(c == 2, -1e30, 0.0)
            mrow = cm if mrow is None else jnp.maximum(mrow, cm)
        for n, nb_ref in enumerate(nb_refs):
            j = slotj_ref[t * ATT_NS + n]
            bias = nb_ref[...]
            top = s_scr[j, 0:tq, :] + bias
            bot = s_scr[j, tq:2 * tq, :] + bias
            s_scr[j, 0:tq, :] = top
            s_scr[j, tq:2 * tq, :] = bot
            mrow = jnp.maximum(mrow, jnp.concatenate([lane_tile_max(top), lane_tile_max(bot)], axis=0))
        m_scr[...] = mrow

    def outputs(s_scr, m_scr, row0):
        m = jnp.max(m_scr[...], axis=-1, keepdims=True)
        acc = None
        for j in range(nk):
            e = jnp.exp2(s_scr[j] - m).astype(BF16)
            part = jnp.dot(e, vaug_scr[j * ATT_TK:(j + 1) * ATT_TK, :], preferred_element_type=F32)
            acc = part if acc is None else acc + part
        o1 = acc[:tq, :DV] / acc[:tq, DV:DV + 1]
        o2 = acc[tq:, :DV] / acc[tq:, DV:DV + 1]
        o_ref[row0:row0 + tq, :] = _attn_finish(o1, o2, lam, sg, lam_init, o_ref.dtype)

    @pl.when(ip == 0)
    def _():
        vaug_scr[:, :DV] = v_ref[...]
        lane = lax.broadcasted_iota(I32, (s, DV), 1)
        vaug_scr[:, DV:] = jnp.where(lane == 0, 1.0, 0.0).astype(BF16)
        scores(0, sa_scr, ma_scr, (nbp0_ref, nbp1_ref))

    t_odd = 2 * ip + 1
    t_next = jnp.minimum(2 * ip + 2, nq - 1)
    scores(t_odd, sb_scr, mb_scr, (nbb0_ref, nbb1_ref))
    outputs(sa_scr, ma_scr, 0)
    scores(t_next, sa_scr, ma_scr, (nba0_ref, nba1_ref))
    outputs(sb_scr, mb_scr, tq)


def _attn_fast(qkv, nbt, plan, lam4, sub_g3, l):
    b, s, _ = qkv.shape
    cls, slotj, _, ctab, _ = plan
    nq = s // ATT_TQ
    assert nq % 2 == 0 and ATT_NS == 2
    lam_init = 0.8 - 0.6 * math.exp(-0.3 * l)
    smem = pl.BlockSpec(memory_space=pltpu.SMEM)
    lam_specs = [pl.BlockSpec((None, 1, DK), lambda bi, h, ip: (l, 0, 0)) for _ in range(4)]
    nb_tile = (None, None, None, ATT_TQ, ATT_TK)
    nb_specs = ([pl.BlockSpec(nb_tile, lambda bi, h, ip, n=n: (h, 0, n, 0, 0)) for n in range(ATT_NS)]
                + [pl.BlockSpec(nb_tile, lambda bi, h, ip, n=n: (h, jnp.minimum(2 * ip + 2, nq - 1), n, 0, 0))
                   for n in range(ATT_NS)]
                + [pl.BlockSpec(nb_tile, lambda bi, h, ip, n=n: (h, 2 * ip + 1, n, 0, 0)) for n in range(ATT_NS)])
    return pl.pallas_call(
        functools.partial(_attn_kernel, lam_init=lam_init),
        grid=(b, N_HEADS, nq // 2),
        in_specs=[smem, smem, smem] + lam_specs + [
            pl.BlockSpec((None, s, 2 * DK), lambda bi, h, ip: (bi, 0, h)),
            pl.BlockSpec((None, s, 2 * DK), lambda bi, h, ip: (bi, 0, N_HEADS + h)),
            pl.BlockSpec((None, s, DV), lambda bi, h, ip: (bi, 0, 2 * N_HEADS + h)),
        ] + nb_specs + [pl.BlockSpec((None, 1, DV), lambda bi, h, ip: (l, 0, 0))],
        out_specs=pl.BlockSpec((None, 2 * ATT_TQ, DV), lambda bi, h, ip: (bi, ip, h)),
        out_shape=jax.ShapeDtypeStruct((b, s, N_HEADS * DV), BF16),
        scratch_shapes=[pltpu.VMEM((s // ATT_TK, 2 * ATT_TQ, ATT_TK), F32),
                        pltpu.VMEM((s // ATT_TK, 2 * ATT_TQ, ATT_TK), F32),
                        pltpu.VMEM((2 * ATT_TQ, LANES), F32),
                        pltpu.VMEM((2 * ATT_TQ, LANES), F32),
                        pltpu.VMEM((s, 2 * DV), BF16)],
        compiler_params=_cparams(("arbitrary", "arbitrary", "arbitrary"), 56),
        name="diff_attention",
    )(cls, slotj, ctab, *lam4, qkv, qkv, qkv, nbt, nbt, nbt, nbt, nbt, nbt, sub_g3)


def _attn_any_kernel(tab_ref, lq1_ref, lk1_ref, lq2_ref, lk2_ref, q_ref, k_ref, v_ref, pq_ref, pk_ref,
                     sg_ref, o_ref, *, lam_init):
    h = pl.program_id(1)
    tq = q_ref.shape[0]
    bias = _bias_tile(tab_ref, h, pk_ref[...] - pq_ref[...])
    sc = lax.dot_general(_stack_maps(q_ref[...]), k_ref[...], (((1,), (1,)), ((), ())),
                         preferred_element_type=F32)
    v = v_ref[...]

    def one_map(sm):
        sm = sm + bias
        e = jnp.exp2(sm - jnp.max(sm, axis=-1, keepdims=True))
        den = jnp.sum(e, axis=-1, keepdims=True)
        return jnp.dot(e.astype(BF16), v, preferred_element_type=F32) / den

    lam = _lambda(lq1_ref, lk1_ref, lq2_ref, lk2_ref, lam_init)
    o_ref[...] = _attn_finish(one_map(sc[:tq]), one_map(sc[tq:]), lam, sg_ref[...], lam_init, o_ref.dtype)


def _attn_any(qkv, rel_bias, positions, lam4, sub_g3, l):
    b, s, _ = qkv.shape
    tq = 128
    lam_init = 0.8 - 0.6 * math.exp(-0.3 * l)
    lam_specs = [pl.BlockSpec((None, 1, DK), lambda bi, h, i: (l, 0, 0)) for _ in range(4)]
    return pl.pallas_call(
        functools.partial(_attn_any_kernel, lam_init=lam_init),
        grid=(b, N_HEADS, s // tq),
        in_specs=[pl.BlockSpec(memory_space=pltpu.SMEM)] + lam_specs + [
            pl.BlockSpec((None, tq, 2 * DK), lambda bi, h, i: (bi, i, h)),
            pl.BlockSpec((None, s, 2 * DK), lambda bi, h, i: (bi, 0, N_HEADS + h)),
            pl.BlockSpec((None, s, DV), lambda bi, h, i: (bi, 0, 2 * N_HEADS + h)),
            pl.BlockSpec((tq, 1), lambda bi, h, i: (i, 0)),
            pl.BlockSpec((1, s), lambda bi, h, i: (0, 0)),
            pl.BlockSpec((None, 1, DV), lambda bi, h, i: (l, 0, 0)),
        ],
        out_specs=pl.BlockSpec((None, tq, DV), lambda bi, h, i: (bi, i, h)),
        out_shape=jax.ShapeDtypeStruct((b, s, N_HEADS * DV), BF16),
        compiler_params=_cparams(("arbitrary", "arbitrary", "arbitrary"), 48),
        name="diff_attention_any",
    )(rel_bias.reshape(-1), *lam4, qkv, qkv, qkv, positions.reshape(s, 1), positions.reshape(1, s), sub_g3)


def _diff_attention(qkv, nbt, plan, rel_bias, positions, lam4, sub_g3, l):
    return lax.cond(plan[4],
                    lambda: _attn_fast(qkv, nbt, plan, lam4, sub_g3, l),
                    lambda: _attn_any(qkv, rel_bias, positions, lam4, sub_g3, l))


def _post_mix_kernel(x_ref, mo_ref, g1_ref, ng_ref, sc_ref, sh_ref, wr_ref, x1_ref, h_ref, lg_ref):
    x1 = x_ref[...] + g1_ref[...] * mo_ref[...].astype(F32)
    x1_ref[...] = x1
    ms = jnp.mean(x1 * x1, axis=-1, keepdims=True)
    h = x1 * lax.rsqrt(ms + EPS) * ng_ref[...]
    h = h * (1.0 + sc_ref[...]) + sh_ref[...]
    d = h.shape[1]
    lg = jnp.dot(h, wr_ref[...], preferred_element_type=F32, precision=lax.Precision.HIGHEST)
    lane = lax.broadcasted_iota(I32, lg.shape, 1)
    valid = lane < N_EXPERTS
    lgm = jnp.where(valid, lg, -1e30)
    ex = jnp.where(valid, jnp.exp(lgm - jnp.max(lgm, axis=-1, keepdims=True)), 0.0)
    aff = ex / jnp.sum(ex, axis=-1, keepdims=True)
    h_ref[:, :d] = h
    h_ref[:, d:] = aff
    lg_ref[...] = aff


def _post_mix(x, mixo, modr, norm2_g3, w_router_p, l):
    b, s, d = x.shape
    tm = 256
    mspec = lambda idx: pl.BlockSpec((None, None, None, 1, d), lambda bi, i: (l, bi, idx, 0, 0))
    return pl.pallas_call(
        _post_mix_kernel,
        grid=(b, s // tm),
        in_specs=[
            pl.BlockSpec((None, tm, d), lambda bi, i: (bi, i, 0)),
            pl.BlockSpec((None, tm, d), lambda bi, i: (bi, i, 0)),
            mspec(2),
            pl.BlockSpec((None, 1, d), lambda bi, i: (l, 0, 0)),
            mspec(4),
            mspec(3),
            pl.BlockSpec((None, d, LANES), lambda bi, i: (l, 0, 0)),
        ],
        out_specs=[
            pl.BlockSpec((None, tm, d), lambda bi, i: (bi, i, 0)),
            pl.BlockSpec((None, tm, d + LANES), lambda bi, i: (bi, i, 0)),
            pl.BlockSpec((None, tm, LANES), lambda bi, i: (bi, i, 0)),
        ],
        out_shape=[
            jax.ShapeDtypeStruct((b, s, d), F32),
            jax.ShapeDtypeStruct((b, s, d + LANES), F32),
            jax.ShapeDtypeStruct((b, s, LANES), F32),
        ],
        compiler_params=_cparams(("arbitrary", "arbitrary")),
        name="post_mix",
    )(x, mixo, modr, norm2_g3, modr, modr, w_router_p)


def _cumsum_lanes(x01):
    rows, n = x01.shape
    blk = MOE_TS
    ii = lax.broadcasted_iota(I32, (blk, blk), 0)
    jj = lax.broadcasted_iota(I32, (blk, blk), 1)
    tri = jnp.where(ii <= jj, 1.0, 0.0).astype(BF16)
    xb = x01.astype(BF16)
    carry = jnp.zeros((rows, 1), F32)
    outs, totals = [], [carry]
    for c in range(n // blk):
        part = jnp.dot(xb[:, c * blk:(c + 1) * blk], tri, preferred_element_type=F32) + carry
        outs.append(part)
        carry = part[:, blk - 1:blk]
        totals.append(carry)
    return jnp.concatenate(outs, axis=1), totals


def _route_kernel(aff_ref, idx_ref, slott_ref, lo_ref, cs_scr, *, cap):
    at = aff_ref[...].T[:N_EXPERTS, :]
    keys = lax.bitcast_convert_type(at, I32)

    def body(i, prefix):
        cand = prefix | jnp.left_shift(jnp.int32(1), 30 - i)
        cnt = jnp.sum(jnp.where(keys >= cand, 1.0, 0.0), axis=1, keepdims=True)
        return jnp.where(cnt >= cap, cand, prefix)

    thr = lax.fori_loop(0, 31, body, jnp.zeros((N_EXPERTS, 1), I32))
    gt = jnp.where(keys > thr, 1.0, 0.0)
    eq = jnp.where(keys == thr, 1.0, 0.0)
    need = cap - jnp.sum(gt, axis=1, keepdims=True)
    take = eq * jnp.where(_cumsum_lanes(eq)[0] <= need, 1.0, 0.0)
    sel = gt + take
    cs, totals = _cumsum_lanes(sel)
    slot = jnp.where(sel > 0.5, cs - 1.0, -1.0).astype(I32)
    s = slot.shape[1]
    pad = jnp.full((LANES - N_EXPERTS, s), -1, I32)
    slott_ref[...] = jnp.concatenate([slot, pad], axis=0).T

    lane_e = lax.broadcasted_iota(I32, (N_EXPERTS, LANES), 1)
    lo = jnp.zeros((N_EXPERTS, LANES), F32)
    for j, tot in enumerate(totals):
        lo = jnp.where(lane_e == j, tot, lo)
    lo_ref[...] = lo.astype(I32)

    cs_scr[...] = cs
    lane = lax.broadcasted_iota(I32, (cap, LANES), 1)
    rows = 128

    def one_expert(e, idxt):
        row = cs_scr[pl.ds(e, 1), :]
        cols = []
        for c0 in range(0, cap, rows):
            cio = (lax.broadcasted_iota(I32, (rows, s), 0) + c0).astype(F32)
            cols.append(jnp.sum(jnp.where(row <= cio, 1.0, 0.0), axis=1, keepdims=True))
        return jnp.where(lane == e, jnp.concatenate(cols, axis=0), idxt)

    idxt = lax.fori_loop(0, N_EXPERTS, one_expert, jnp.zeros((cap, LANES), F32))
    idx_ref[...] = idxt.T[:N_EXPERTS, :].astype(I32)


def _route(aff, cap):
    b, s, _ = aff.shape
    return pl.pallas_call(
        functools.partial(_route_kernel, cap=cap),
        grid=(b,),
        in_specs=[pl.BlockSpec((None, s, LANES), lambda bi: (bi, 0, 0))],
        out_specs=[
            pl.BlockSpec((None, N_EXPERTS, cap), lambda bi: (bi, 0, 0)),
            pl.BlockSpec((None, s, LANES), lambda bi: (bi, 0, 0)),
            pl.BlockSpec((None, N_EXPERTS, LANES), lambda bi: (bi, 0, 0)),
        ],
        out_shape=[
            jax.ShapeDtypeStruct((b, N_EXPERTS, cap), I32),
            jax.ShapeDtypeStruct((b, s, LANES), I32),
            jax.ShapeDtypeStruct((b, N_EXPERTS, LANES), I32),
        ],
        scratch_shapes=[pltpu.VMEM((N_EXPERTS, s), F32)],
        compiler_params=_cparams(("arbitrary",), 48),
        name="route",
    )(aff)


def _expert_kernel(idx_ref, h_ref, w1_ref, w3_ref, w2_ref, y_ref, rows_ref, xb_ref, acc_ref, gate_ref, sem):
    e = pl.program_id(0)
    f = pl.program_id(1)
    ne = pl.num_programs(0)
    nf = pl.num_programs(1)
    nb, cap, d = y_ref.shape
    n_rows = nb * cap
    per_step = n_rows // MOE_NF

    steps_per_batch = cap // per_step
    assert steps_per_batch & (steps_per_batch - 1) == 0 and steps_per_batch * per_step == cap

    def wait_all_rows():
        for ff in range(MOE_NF):
            pltpu.make_async_copy(h_ref.at[0, pl.ds(0, per_step), :], rows_ref.at[ff], sem).wait()

    @pl.when(jnp.logical_and(e == 0, f == 0))
    def _():
        for ff in range(MOE_NF):
            src = (ff // steps_per_batch) * N_EXPERTS * cap + (ff % steps_per_batch) * per_step

            def body(r, carry, ff=ff, src=src):
                pltpu.make_async_copy(h_ref.at[ff // steps_per_batch, pl.ds(idx_ref[src + r], 1), :],
                                      rows_ref.at[ff, pl.ds(r, 1), :], sem).start()
                return carry
            lax.fori_loop(0, per_step, body, 0)

    @pl.when(f == 0)
    def _():
        wait_all_rows()
        lane = lax.broadcasted_iota(I32, (per_step, LANES), 1)
        for ff in range(MOE_NF):
            rs = slice(ff * per_step, (ff + 1) * per_step)
            xb_ref[rs, :] = rows_ref[ff, :, :d].astype(BF16)
            gate_ref[rs, :] = jnp.sum(jnp.where(lane == e, rows_ref[ff, :, d:], 0.0), axis=1, keepdims=True)
        acc_ref[...] = jnp.zeros(acc_ref.shape, F32)

    e_next = jnp.minimum(e + 1, ne - 1)
    bi = lax.shift_right_logical(f, steps_per_batch.bit_length() - 1)
    c0 = (f & (steps_per_batch - 1)) * per_step
    src0 = (bi * N_EXPERTS + e_next) * cap + c0
    for r in range(per_step):
        pltpu.make_async_copy(h_ref.at[bi, pl.ds(idx_ref[src0 + r], 1), :],
                              rows_ref.at[f, pl.ds(r, 1), :], sem).start()

    xg = xb_ref[...]
    a = jnp.dot(xg, w1_ref[...].astype(BF16), preferred_element_type=F32)
    g = jnp.dot(xg, w3_ref[...].astype(BF16), preferred_element_type=F32)
    act = (_silu(a) * g).astype(BF16)
    acc_ref[...] += jnp.dot(act, w2_ref[...].astype(BF16), preferred_element_type=F32)

    @pl.when(f == nf - 1)
    def _():
        y_ref[...] = (acc_ref[...] * gate_ref[...]).reshape(nb, cap, d).astype(y_ref.dtype)

    @pl.when(jnp.logical_and(e == ne - 1, f == nf - 1))
    def _():
        wait_all_rows()


def _experts(idx, h2, w1, w3, w2, l, cap):
    b, s, dext = h2.shape
    d = dext - LANES
    e = w1.shape[1]
    ff = w1.shape[-1]
    fc = ff // MOE_NF
    grid_spec = pltpu.PrefetchScalarGridSpec(
        num_scalar_prefetch=1,
        grid=(e, MOE_NF),
        in_specs=[
            pl.BlockSpec(memory_space=pl.ANY),
            pl.BlockSpec((None, None, d, fc), lambda ei, f, ix: (l, ei, 0, f)),
            pl.BlockSpec((None, None, d, fc), lambda ei, f, ix: (l, ei, 0, f)),
            pl.BlockSpec((None, None, fc, d), lambda ei, f, ix: (l, ei, f, 0)),
        ],
        out_specs=pl.BlockSpec((b, None, cap, d), lambda ei, f, ix: (0, ei, 0, 0)),
        scratch_shapes=[pltpu.VMEM((MOE_NF, b * cap // MOE_NF, dext), F32), pltpu.VMEM((b * cap, d), BF16),
                        pltpu.VMEM((b * cap, d), F32), pltpu.VMEM((b * cap, 1), F32),
                        pltpu.SemaphoreType.DMA],
    )
    return pl.pallas_call(
        _expert_kernel,
        grid_spec=grid_spec,
        out_shape=jax.ShapeDtypeStruct((b, e, cap, d), BF16),
        compiler_params=_cparams(("arbitrary", "arbitrary"), 56),
        name="moe_experts",
    )(idx.reshape(-1), h2, w1, w3, w2)


def _scatter_kernel(lo_ref, slott_ref, y_ref, x1_ref, g2_ref, fg_ref, o_ref, win_ref, full_ref, oh_ref,
                    wsem, fsem, *, final_norm):
    bi = pl.program_id(0)
    ti = pl.program_id(1)
    nb = pl.num_programs(0)
    nt = pl.num_programs(1)
    ts = slott_ref.shape[0]
    cap = y_ref.shape[2]
    step = bi * nt + ti
    buf = step & 1

    def lo_at(b_, t_, e):
        return lo_ref[(b_ * N_EXPERTS + e) * LANES + t_]

    def win_start(b_, t_, e):
        start = lax.shift_left(lax.shift_right_logical(lo_at(b_, t_, e), 4), 4)
        return pl.multiple_of(jnp.minimum(start, cap - MOE_WIN), 16)

    def win_copy(b_, t_, e, slot):
        return pltpu.make_async_copy(y_ref.at[b_, e, pl.ds(win_start(b_, t_, e), MOE_WIN), :],
                                     win_ref.at[slot, pl.ds(e * MOE_WIN, MOE_WIN), :], wsem.at[slot])

    @pl.when(step == 0)
    def _():
        for e in range(N_EXPERTS):
            win_copy(0, 0, e, 0).start()

    @pl.when(step + 1 < nb * nt)
    def _():
        wrap = ti + 1 == nt
        b_n = jnp.where(wrap, bi + 1, bi)
        t_n = jnp.where(wrap, 0, ti + 1)
        for e in range(N_EXPERTS):
            win_copy(b_n, t_n, e, 1 - buf).start()

    for e in range(N_EXPERTS):
        win_copy(bi, ti, e, buf).wait()

    lane = lax.broadcasted_iota(I32, (ts, LANES), 1)

    def slot_col(e):
        return jnp.sum(jnp.where(lane == e, slott_ref[...].astype(F32), 0.0), axis=1, keepdims=True).astype(I32)

    fits = None
    for e in range(N_EXPERTS):
        ok = lo_at(bi, ti + 1, e) - win_start(bi, ti, e) <= MOE_WIN
        fits = ok if fits is None else jnp.logical_and(fits, ok)

    @pl.when(fits)
    def _():
        widx = lax.broadcasted_iota(I32, (ts, MOE_WIN), 1)
        for e in range(N_EXPERTS):
            oh_ref[:, e * MOE_WIN:(e + 1) * MOE_WIN] = jnp.where(
                widx == slot_col(e) - win_start(bi, ti, e), 1.0, 0.0).astype(BF16)
        moe = jnp.dot(oh_ref[...], win_ref[buf], preferred_element_type=F32)
        o_ref[...] = x1_ref[...] + g2_ref[...] * moe

    @pl.when(jnp.logical_not(fits))
    def _():
        o_ref[...] = x1_ref[...]
        cidx = lax.broadcasted_iota(I32, (ts, cap), 1)

        def one_expert(e, carry):
            cp = pltpu.make_async_copy(y_ref.at[bi, e], full_ref, fsem)
            cp.start()
            cp.wait()
            onehot = jnp.where(cidx == slot_col(e), 1.0, 0.0).astype(BF16)
            o_ref[...] += g2_ref[...] * jnp.dot(onehot, full_ref[...], preferred_element_type=F32)
            return carry

        lax.fori_loop(0, N_EXPERTS, one_expert, 0)

    if final_norm:
        xo = o_ref[...]
        ms = jnp.mean(xo * xo, axis=-1, keepdims=True)
        o_ref[...] = xo * lax.rsqrt(ms + EPS) * fg_ref[...]


def _scatter(lo, slott, y, x1, modr, l, final_g2d, final_norm):
    b, s, d = x1.shape
    cap = y.shape[2]
    assert cap >= MOE_WIN and s // MOE_TS < LANES
    grid_spec = pltpu.PrefetchScalarGridSpec(
        num_scalar_prefetch=1,
        grid=(b, s // MOE_TS),
        in_specs=[
            pl.BlockSpec((None, MOE_TS, LANES), lambda bi, i, lo_: (bi, i, 0)),
            pl.BlockSpec(memory_space=pl.ANY),
            pl.BlockSpec((None, MOE_TS, d), lambda bi, i, lo_: (bi, i, 0)),
            pl.BlockSpec((None, None, None, 1, d), lambda bi, i, lo_: (l, bi, 5, 0, 0)),
            pl.BlockSpec((1, d), lambda bi, i, lo_: (0, 0)),
        ],
        out_specs=pl.BlockSpec((None, MOE_TS, d), lambda bi, i, lo_: (bi, i, 0)),
        scratch_shapes=[pltpu.VMEM((2, N_EXPERTS * MOE_WIN, d), BF16), pltpu.VMEM((cap, d), BF16),
                        pltpu.VMEM((MOE_TS, N_EXPERTS * MOE_WIN), BF16),
                        pltpu.SemaphoreType.DMA((2,)), pltpu.SemaphoreType.DMA],
    )
    return pl.pallas_call(
        functools.partial(_scatter_kernel, final_norm=final_norm),
        grid_spec=grid_spec,
        out_shape=jax.ShapeDtypeStruct((b, s, d), F32),
        compiler_params=_cparams(("arbitrary", "arbitrary"), 56),
        name="moe_scatter",
    )(lo.reshape(-1), slott, y, x1, modr, final_g2d)


def kernel(x, c, positions, w_mod, b_mod, norm1_g, norm2_g, w_in, fnet_w, pool_w, pool_scale,
           lam_q1, lam_k1, lam_q2, lam_k2, sub_g, rel_bias, w_out, w_router, w1, w3, w2, final_g):
    b, s, d = x.shape
    depth = w_mod.shape[0]
    fnet_wd = fnet_w.shape[1] * fnet_w.shape[2]
    pool_wd = pool_w.shape[1] * pool_w.shape[2]
    ab_w = fnet_wd + pool_wd
    in_w = w_in.shape[-1]
    cap = EC_CAPACITY * s // N_EXPERTS

    c8 = jnp.zeros((8, d), F32).at[:b].set(c)
    mod = _modulation(c8, w_mod, b_mod.reshape(depth, 1, N_MOD * d))
    modr = mod[:, :b].reshape(depth, b, N_MOD, 1, d)

    norm1_g3 = norm1_g.reshape(depth, 1, d)
    norm2_g3 = norm2_g.reshape(depth, 1, d)
    pool_scale3 = pool_scale.reshape(depth, 1, pool_wd)
    sub_g3 = sub_g.reshape(depth, 1, DV)
    lam4 = [a.reshape(depth, 1, DK) for a in (lam_q1, lam_k1, lam_q2, lam_k2)]
    w_router_p = jnp.zeros((depth, d, LANES), F32).at[:, :, :N_EXPERTS].set(w_router)

    wd = _dft_matrix(s)
    ab = _fnet_weights(fnet_w, s)
    plan = _attn_plan(positions, rel_bias)
    nbt = _near_bias_table(rel_bias, positions, plan[1], plan[2])

    for l in range(depth):
        h1 = _norm_mod(x, norm1_g3, modr, l, 0, 1).reshape(b * s, d)
        u_ab = _project([h1], w_in, l, 0, ab_w, F32, name="proj_in_ab").reshape(b, s, ab_w)
        qkv = _project([h1], w_in, l, ab_w, in_w - ab_w, BF16, lead_cols=N_HEADS * 2 * DK,
                       lead_scale=LOG2E * DK ** -0.5, name="proj_in_qkv").reshape(b, s, in_w - ab_w)

        za, zb = _fnet_z(u_ab, ab, l)
        ya = _dft_apply(wd, za, zb, b)
        yb = _pool_mixer(u_ab, pool_w, pool_scale3, l, fnet_wd // pool_w.shape[2])
        yc = _diff_attention(qkv, nbt, plan, rel_bias, positions, lam4, sub_g3, l)

        mixo = _project([ya.reshape(b * s, -1), yb.reshape(b * s, -1), yc.reshape(b * s, -1)],
                        w_out, l, 0, d, BF16, name="proj_out").reshape(b, s, d)
        x1, h2, aff = _post_mix(x, mixo, modr, norm2_g3, w_router_p, l)

        idx, slott, lo = _route(aff, cap)
        y = _experts(idx, h2, w1, w3, w2, l, cap)
        x = _scatter(lo, slott, y, x1, modr, l, final_g.reshape(1, d), final_norm=(l == depth - 1))

    return x
```

```python
import functools
import math

import numpy as np
import jax
import jax.numpy as jnp
from jax import lax
from jax.experimental import pallas as pl
from jax.experimental.pallas import tpu as pltpu

F32 = jnp.float32
BF16 = jnp.bfloat16
I32 = jnp.int32

FNET_GROUPS = 4
POOL_GROUPS = 4
POOL_HALO = 64
N_HEADS = 8
DK = 64
DV = 128
N_BUCKETS = 32
MAX_DISTANCE = 128
N_EXPERTS = 16
EC_CAPACITY = 2
N_MOD = 6
EPS = 1e-6
LANES = 128
LOG2E = 1.4426950408889634
ATT_TQ = 256
ATT_TK = 512
ATT_NS = 2
MOE_NF = 4
MOE_TS = 512
MOE_WIN = 128


def _cparams(sem, vmem_mb=None):
    kw = dict(dimension_semantics=sem)
    if vmem_mb is not None:
        kw["vmem_limit_bytes"] = vmem_mb * 1024 * 1024
    return pltpu.CompilerParams(**kw)


def _silu(x):
    return x * jax.nn.sigmoid(x)


def _mod_kernel(c_ref, w_ref, b_ref, o_ref):
    ca = _silu(c_ref[...]).astype(BF16)
    o_ref[...] = jnp.dot(ca, w_ref[...].astype(BF16), preferred_element_type=F32) + b_ref[...]


def _modulation(c8, w_mod, b_mod3):
    depth, d, n = w_mod.shape
    tn = 1024
    return pl.pallas_call(
        _mod_kernel,
        grid=(depth, n // tn),
        in_specs=[
            pl.BlockSpec((8, d), lambda l, j: (0, 0)),
            pl.BlockSpec((None, d, tn), lambda l, j: (l, 0, j)),
            pl.BlockSpec((None, 1, tn), lambda l, j: (l, 0, j)),
        ],
        out_specs=pl.BlockSpec((None, 8, tn), lambda l, j: (l, 0, j)),
        out_shape=jax.ShapeDtypeStruct((depth, 8, n), F32),
        compiler_params=_cparams(("arbitrary", "arbitrary")),
        name="modulation",
    )(c8, w_mod, b_mod3)


def _norm_mod_kernel(x_ref, g_ref, sc_ref, sh_ref, o_ref):
    x = x_ref[...]
    ms = jnp.mean(x * x, axis=-1, keepdims=True)
    y = x * lax.rsqrt(ms + EPS) * g_ref[...]
    o_ref[...] = (y * (1.0 + sc_ref[...]) + sh_ref[...]).astype(o_ref.dtype)


def _norm_mod(x, g3, modr, l, sh_idx, sc_idx):
    b, s, d = x.shape
    tm = 512
    return pl.pallas_call(
        _norm_mod_kernel,
        grid=(b, s // tm),
        in_specs=[
            pl.BlockSpec((None, tm, d), lambda bi, i: (bi, i, 0)),
            pl.BlockSpec((None, 1, d), lambda bi, i: (l, 0, 0)),
            pl.BlockSpec((None, None, None, 1, d), lambda bi, i: (l, bi, sc_idx, 0, 0)),
            pl.BlockSpec((None, None, None, 1, d), lambda bi, i: (l, bi, sh_idx, 0, 0)),
        ],
        out_specs=pl.BlockSpec((None, tm, d), lambda bi, i: (bi, i, 0)),
        out_shape=jax.ShapeDtypeStruct((b, s, d), BF16),
        compiler_params=_cparams(("arbitrary", "arbitrary")),
        name="norm_mod",
    )(x, g3, modr, modr)


def _mm_kernel(*refs, k_sizes, lead_blocks, lead_scale):
    n_a = len(k_sizes)
    a_refs = refs[:n_a]
    w_ref, o_ref, wb_ref = refs[n_a], refs[n_a + 1], refs[n_a + 2]

    @pl.when(pl.program_id(1) == 0)
    def _():
        wb_ref[...] = w_ref[...].astype(BF16)

    acc = None
    off = 0
    for a_ref, ks in zip(a_refs, k_sizes):
        part = jnp.dot(a_ref[...], wb_ref[off:off + ks, :], preferred_element_type=F32)
        acc = part if acc is None else acc + part
        off += ks
    if lead_blocks:
        acc = acc * jnp.where(pl.program_id(0) < lead_blocks, lead_scale, 1.0)
    o_ref[...] = acc.astype(o_ref.dtype)


def _project(a_list, w, l, col0, ncols, out_dtype, tm=1024, tn=512, lead_cols=0, lead_scale=1.0,
             name="project"):
    m = a_list[0].shape[0]
    k_sizes = tuple(a.shape[1] for a in a_list)
    k = sum(k_sizes)
    assert w.shape[1] == k and col0 % tn == 0 and ncols % tn == 0 and m % tm == 0 and lead_cols % tn == 0
    cb0 = col0 // tn
    in_specs = [pl.BlockSpec((tm, ks), lambda j, i: (i, 0)) for ks in k_sizes]
    in_specs.append(pl.BlockSpec((None, k, tn), lambda j, i: (l, 0, cb0 + j)))
    return pl.pallas_call(
        functools.partial(_mm_kernel, k_sizes=k_sizes, lead_blocks=lead_cols // tn, lead_scale=lead_scale),
        grid=(ncols // tn, m // tm),
        in_specs=in_specs,
        out_specs=pl.BlockSpec((tm, tn), lambda j, i: (i, j)),
        out_shape=jax.ShapeDtypeStruct((m, ncols), out_dtype),
        scratch_shapes=[pltpu.VMEM((k, tn), BF16)],
        compiler_params=_cparams(("arbitrary", "arbitrary"), 48),
        name=name,
    )(*a_list, w)


def _dft_tables(s):
    sp = np.arange(s, dtype=np.int64)
    a = np.arange(64, dtype=np.int64)[:, None]
    ang1 = 2.0 * np.pi * ((a * sp[None, :]) % 64) / 64.0
    ang2 = 2.0 * np.pi * ((a * sp[None, :]) % s) / float(s)
    t1c = np.cos(ang1).astype(np.float32).reshape(64, 1, s)
    t1s = np.sin(ang1).astype(np.float32).reshape(64, 1, s)
    t2c = np.cos(ang2).astype(np.float32)
    t2s = np.sin(ang2).astype(np.float32)
    return t1c, t1s, t2c, t2s


def _dftgen_kernel(t1c_ref, t1s_ref, t2c_ref, t2s_ref, o_ref):
    s = t2c_ref.shape[1]
    c1, s1 = t1c_ref[...], t1s_ref[...]
    c2, s2 = t2c_ref[...], t2s_ref[...]
    o_ref[:, :s] = (c1 * c2 - s1 * s2).astype(BF16)
    o_ref[:, s:] = (-(s1 * c2 + c1 * s2)).astype(BF16)


def _dft_matrix(s):
    assert s % 64 == 0 and s // 64 == 64
    t1c, t1s, t2c, t2s = _dft_tables(s)
    return pl.pallas_call(
        _dftgen_kernel,
        grid=(64,),
        in_specs=[
            pl.BlockSpec((None, 1, s), lambda a: (a, 0, 0)),
            pl.BlockSpec((None, 1, s), lambda a: (a, 0, 0)),
            pl.BlockSpec((64, s), lambda a: (0, 0)),
            pl.BlockSpec((64, s), lambda a: (0, 0)),
        ],
        out_specs=pl.BlockSpec((64, 2 * s), lambda a: (a, 0)),
        out_shape=jax.ShapeDtypeStruct((s, 2 * s), BF16),
        compiler_params=_cparams(("arbitrary",)),
        name="dft_matrix",
    )(t1c, t1s, t2c, t2s)


def _fnet_w_kernel(cc_ref, sc_ref, w_ref, o_ref, *, norm):
    depth, groups, cg, _ = w_ref.shape
    cc, sc = cc_ref[...], sc_ref[...]
    for l in range(depth):
        for g in range(groups):
            w = w_ref[l, g]
            a = jnp.dot(cc, w, preferred_element_type=F32, precision=lax.Precision.HIGHEST)
            b = jnp.dot(sc, w, preferred_element_type=F32, precision=lax.Precision.HIGHEST)
            o_ref[l, g, :, :cg] = (a * norm).astype(BF16)
            o_ref[l, g, :, cg:] = (b * norm).astype(BF16)


def _fnet_weights(fnet_w, s):
    depth, groups, cg, _ = fnet_w.shape
    idx = np.arange(cg, dtype=np.int64)
    ang = 2.0 * np.pi * ((idx[:, None] * idx[None, :]) % cg) / float(cg)
    cc = np.cos(ang).astype(np.float32)
    sc = np.sin(ang).astype(np.float32)
    norm = 1.0 / math.sqrt(float(s) * float(cg))
    return pl.pallas_call(
        functools.partial(_fnet_w_kernel, norm=norm),
        out_shape=jax.ShapeDtypeStruct((depth, groups, cg, 2 * cg), BF16),
        name="fnet_weights",
    )(cc, sc, fnet_w)


def _fnet_z_kernel(u_ref, ab_ref, za_ref, zb_ref):
    groups, cg = ab_ref.shape[0], ab_ref.shape[1]
    for g in range(groups):
        ug = u_ref[:, g * cg:(g + 1) * cg].astype(BF16)
        z = jnp.dot(ug, ab_ref[g], preferred_element_type=F32)
        za_ref[:, g * cg:(g + 1) * cg] = z[:, :cg].astype(BF16)
        zb_ref[:, g * cg:(g + 1) * cg] = z[:, cg:].astype(BF16)


def _fnet_z(u_ab, ab, l):
    b, s, _ = u_ab.shape
    groups, cg = ab.shape[1], ab.shape[2]
    fw = groups * cg
    ts = 512
    shp = jax.ShapeDtypeStruct((s, b * fw), BF16)
    return pl.pallas_call(
        _fnet_z_kernel,
        grid=(b, s // ts),
        in_specs=[
            pl.BlockSpec((None, ts, fw), lambda bi, i: (bi, i, 0)),
            pl.BlockSpec((None, groups, cg, 2 * cg), lambda bi, i: (l, 0, 0, 0)),
        ],
        out_specs=[pl.BlockSpec((ts, fw), lambda bi, i: (i, bi)),
                   pl.BlockSpec((ts, fw), lambda bi, i: (i, bi))],
        out_shape=[shp, shp],
        compiler_params=_cparams(("arbitrary", "arbitrary")),
        name="fnet_z",
    )(u_ab, ab)


def _dft_apply_kernel(wd_ref, za_ref, zb_ref, o_ref):
    s = za_ref.shape[0]
    acc = jnp.dot(wd_ref[:, :s], za_ref[...], preferred_element_type=F32)
    acc = acc + jnp.dot(wd_ref[:, s:], zb_ref[...], preferred_element_type=F32)
    o_ref[...] = acc.astype(o_ref.dtype)


def _dft_apply(wd, za, zb, b):
    s = wd.shape[0]
    fw = za.shape[1] // b
    tm = 512
    return pl.pallas_call(
        _dft_apply_kernel,
        grid=(b, s // tm),
        in_specs=[
            pl.BlockSpec((tm, 2 * s), lambda bi, i: (i, 0)),
            pl.BlockSpec((s, fw), lambda bi, i: (0, bi)),
            pl.BlockSpec((s, fw), lambda bi, i: (0, bi)),
        ],
        out_specs=pl.BlockSpec((None, tm, fw), lambda bi, i: (bi, i, 0)),
        out_shape=jax.ShapeDtypeStruct((b, s, fw), BF16),
        compiler_params=_cparams(("arbitrary", "arbitrary"), 48),
        name="dft_apply",
    )(wd, za, zb)


def _pool_kernel(u_ref, w_ref, sc_ref, o_ref, pad_ref):
    s, cg = u_ref.shape
    t = 256
    half = jnp.left_shift(jnp.int32(1), pl.program_id(1))
    pad_ref[0:POOL_HALO, :] = jnp.zeros((POOL_HALO, cg), F32)
    pad_ref[s + POOL_HALO:s + 2 * POOL_HALO, :] = jnp.zeros((POOL_HALO, cg), F32)
    pad_ref[POOL_HALO:s + POOL_HALO, :] = u_ref[...]
    ii = lax.broadcasted_iota(I32, (t, t + 2 * POOL_HALO), 0)
    jj = lax.broadcasted_iota(I32, (t, t + 2 * POOL_HALO), 1)
    dlt = jj - ii - POOL_HALO
    band = jnp.where(dlt >= -half, jnp.where(dlt <= half - 1, 1.0, 0.0), 0.0).astype(BF16)
    wb = w_ref[...].astype(BF16)
    scale = sc_ref[...]

    def body(ti, carry):
        r0 = pl.multiple_of(ti * t, t)
        seg = pad_ref[pl.ds(r0, t + 2 * POOL_HALO), :]
        hi = seg.astype(BF16)
        lo = (seg - hi.astype(F32)).astype(BF16)
        win = jnp.dot(band, hi, preferred_element_type=F32) + jnp.dot(band, lo, preferred_element_type=F32)
        gi = r0 + lax.broadcasted_iota(I32, (t, cg), 0)
        lo_i = jnp.maximum(gi - half, 0)
        hi_i = jnp.minimum(gi + half - 1, s - 1)
        cnt = (hi_i - lo_i + 1).astype(F32)
        dmean = win / cnt - seg[POOL_HALO:POOL_HALO + t, :]
        y = jnp.dot(dmean.astype(BF16), wb, preferred_element_type=F32) * scale
        o_ref[pl.ds(r0, t), :] = y.astype(o_ref.dtype)
        return carry

    lax.fori_loop(0, s // t, body, 0)


def _pool_mixer(u_ab, pool_w, pool_scale3, l, col_block0):
    b, s, _ = u_ab.shape
    groups, cg = pool_w.shape[1], pool_w.shape[2]
    return pl.pallas_call(
        _pool_kernel,
        grid=(b, groups),
        in_specs=[
            pl.BlockSpec((None, s, cg), lambda bi, g: (bi, 0, col_block0 + g)),
            pl.BlockSpec((None, None, cg, cg), lambda bi, g: (l, g, 0, 0)),
            pl.BlockSpec((None, 1, cg), lambda bi, g: (l, 0, g)),
        ],
        out_specs=pl.BlockSpec((None, s, cg), lambda bi, g: (bi, 0, g)),
        out_shape=jax.ShapeDtypeStruct((b, s, groups * cg), BF16),
        scratch_shapes=[pltpu.VMEM((s + 2 * POOL_HALO, cg), F32)],
        compiler_params=_cparams(("arbitrary", "arbitrary")),
        name="pool_mixer",
    )(u_ab, pool_w, pool_scale3)


def _bucket(rel):
    nb = N_BUCKETS // 2
    max_exact = nb // 2
    n = jnp.abs(rel)
    nf = jnp.maximum(n, 1).astype(F32)
    large = max_exact + (jnp.log(nf / max_exact) / math.log(MAX_DISTANCE / max_exact)
                         * (nb - max_exact)).astype(I32)
    large = jnp.minimum(large, nb - 1)
    return jnp.where(rel > 0, nb, 0) + jnp.where(n < max_exact, n, large)


def _bias_tile(tab_ref, h, rel):
    bucket = _bucket(rel)
    val = jnp.full(rel.shape, tab_ref[h], F32)
    for j in range(1, N_BUCKETS):
        val = jnp.where(bucket == j, tab_ref[j * N_HEADS + h], val)
    return val * LOG2E


def _near_bias_kernel(slotj_ref, nnear_ref, tabt_ref, pq_ref, pk_ref, o_ref):
    del slotj_ref
    used = jnp.where(pl.program_id(1) < nnear_ref[pl.program_id(0)], 1.0, 0.0)
    bucket = _bucket(pk_ref[...] - pq_ref[...])
    tq, tk = bucket.shape
    for h in range(N_HEADS):
        row = jnp.broadcast_to(tabt_ref[h:h + 1, :], (tq, LANES))
        cols = [jnp.take_along_axis(row, bucket[:, c * LANES:(c + 1) * LANES], axis=1)
                for c in range(tk // LANES)]
        o_ref[h] = jnp.concatenate(cols, axis=1) * used


def _near_bias_table(rel_bias, positions, slotj, nnear):
    s = positions.shape[0]
    nq = s // ATT_TQ
    tabt = jnp.zeros((N_HEADS, LANES), F32).at[:, :N_BUCKETS].set(rel_bias.T * LOG2E)
    grid_spec = pltpu.PrefetchScalarGridSpec(
        num_scalar_prefetch=2,
        grid=(nq, ATT_NS),
        in_specs=[
            pl.BlockSpec((N_HEADS, LANES), lambda i, n, sj, nn: (0, 0)),
            pl.BlockSpec((ATT_TQ, 1), lambda i, n, sj, nn: (i, 0)),
            pl.BlockSpec((1, ATT_TK), lambda i, n, sj, nn: (0, sj[i * ATT_NS + n])),
        ],
        out_specs=pl.BlockSpec((N_HEADS, None, None, ATT_TQ, ATT_TK), lambda i, n, sj, nn: (0, i, n, 0, 0)),
    )
    return pl.pallas_call(
        _near_bias_kernel,
        grid_spec=grid_spec,
        out_shape=jax.ShapeDtypeStruct((N_HEADS, nq, ATT_NS, ATT_TQ, ATT_TK), F32),
        compiler_params=_cparams(("arbitrary", "arbitrary")),
        name="near_bias_table",
    )(slotj, nnear, tabt, positions.reshape(s, 1), positions.reshape(1, s))


def _attn_plan(positions, rel_bias):
    s = positions.shape[0]
    nq, nk = s // ATT_TQ, s // ATT_TK
    pq = positions.reshape(nq, ATT_TQ)
    pk = positions.reshape(nk, ATT_TK)
    rel_min = pk.min(axis=1)[None, :] - pq.max(axis=1)[:, None]
    rel_max = pk.max(axis=1)[None, :] - pq.min(axis=1)[:, None]
    cls = jnp.where(rel_min >= MAX_DISTANCE, 1, jnp.where(rel_max <= -MAX_DISTANCE, 0, 2)).astype(I32)
    near = cls == 2
    nnear = near.sum(axis=1).astype(I32)
    slotj = jnp.argsort(jnp.logical_not(near), axis=1, stable=True)[:, :ATT_NS].astype(I32)
    fits = jnp.all(nnear <= ATT_NS)
    nb = N_BUCKETS // 2
    ctab = jnp.stack([rel_bias[nb - 1], rel_bias[2 * nb - 1], jnp.zeros((N_HEADS,), F32)], axis=1) * LOG2E
    return cls.reshape(-1), slotj.reshape(-1), nnear, ctab.reshape(-1).astype(F32), fits


def _lambda(lq1_ref, lk1_ref, lq2_ref, lk2_ref, lam_init):
    return (jnp.exp(jnp.sum(lq1_ref[...] * lk1_ref[...], axis=-1, keepdims=True))
            - jnp.exp(jnp.sum(lq2_ref[...] * lk2_ref[...], axis=-1, keepdims=True)) + lam_init)


def _stack_maps(q):
    lane = lax.broadcasted_iota(I32, q.shape, 1)
    zero = jnp.zeros_like(q)
    return jnp.concatenate([jnp.where(lane < DK, q, zero), jnp.where(lane >= DK, q, zero)], axis=0)


def _attn_finish(o1, o2, lam, sg, lam_init, dtype):
    o = o1 - lam * o2
    ms = jnp.mean(o * o, axis=-1, keepdims=True)
    y = o * lax.rsqrt(ms + EPS) * sg
    return (y * (1.0 - lam_init)).astype(dtype)


def _attn_kernel(cls_ref, slotj_ref, ctab_ref, lq1_ref, lk1_ref, lq2_ref, lk2_ref,
                 q_ref, k_ref, v_ref, nbp0_ref, nbp1_ref, nba0_ref, nba1_ref, nbb0_ref, nbb1_ref,
                 sg_ref, o_ref, sa_scr, sb_scr, ma_scr, mb_scr, vaug_scr, *, lam_init):
    h = pl.program_id(1)
    ip = pl.program_id(2)
    tq = ATT_TQ
    s = k_ref.shape[0]
    nk = s // ATT_TK
    nq = s // tq
    lam = _lambda(lq1_ref, lk1_ref, lq2_ref, lk2_ref, lam_init)
    sg = sg_ref[...]

    def lane_tile_max(blk):
        out = blk[:, :LANES]
        for c in range(1, blk.shape[1] // LANES):
            out = jnp.maximum(out, blk[:, c * LANES:(c + 1) * LANES])
        return out

    def scores(t, s_scr, m_scr, nb_refs):
        qs = _stack_maps(q_ref[pl.ds(pl.multiple_of(t * tq, tq), tq), :])
        mrow = None
        for j in range(nk):
            c = cls_ref[t * nk + j]
            sc = lax.dot_general(qs, k_ref[j * ATT_TK:(j + 1) * ATT_TK, :], (((1,), (1,)), ((), ())),
                                 preferred_element_type=F32) + ctab_ref[h * 3 + c]
            s_scr[j] = sc
            cm = lane_tile_max(sc) + jnp.where(c == 2, -1e30, 0.0)
            mrow = cm if mrow is None else jnp.maximum(mrow, cm)
        for n, nb_ref in enumerate(nb_refs):
            j = slotj_ref[t * ATT_NS + n]
            bias = nb_ref[...]
            top = s_scr[j, 0:tq, :] + bias
            bot = s_scr[j, tq:2 * tq, :] + bias
            s_scr[j, 0:tq, :] = top
            s_scr[j, tq:2 * tq, :] = bot
            mrow = jnp.maximum(mrow, jnp.concatenate([lane_tile_max(top), lane_tile_max(bot)], axis=0))
        m_scr[...] = mrow

    def outputs(s_scr, m_scr, row0):
        m = jnp.max(m_scr[...], axis=-1, keepdims=True)
        acc = None
        for j in range(nk):
            e = jnp.exp2(s_scr[j] - m).astype(BF16)
            part = jnp.dot(e, vaug_scr[j * ATT_TK:(j + 1) * ATT_TK, :], preferred_element_type=F32)
            acc = part if acc is None else acc + part
        o1 = acc[:tq, :DV] / acc[:tq, DV:DV + 1]
        o2 = acc[tq:, :DV] / acc[tq:, DV:DV + 1]
        o_ref[row0:row0 + tq, :] = _attn_finish(o1, o2, lam, sg, lam_init, o_ref.dtype)

    @pl.when(ip == 0)
    def _():
        vaug_scr[:, :DV] = v_ref[...]
        lane = lax.broadcasted_iota(I32, (s, DV), 1)
        vaug_scr[:, DV:] = jnp.where(lane == 0, 1.0, 0.0).astype(BF16)
        scores(0, sa_scr, ma_scr, (nbp0_ref, nbp1_ref))

    t_odd = 2 * ip + 1
    t_next = jnp.minimum(2 * ip + 2, nq - 1)
    scores(t_odd, sb_scr, mb_scr, (nbb0_ref, nbb1_ref))
    outputs(sa_scr, ma_scr, 0)
    scores(t_next, sa_scr, ma_scr, (nba0_ref, nba1_ref))
    outputs(sb_scr, mb_scr, tq)


def _attn_fast(qkv, nbt, plan, lam4, sub_g3, l):
    b, s, _ = qkv.shape
    cls, slotj, _, ctab, _ = plan
    nq = s // ATT_TQ
    assert nq % 2 == 0 and ATT_NS == 2
    lam_init = 0.8 - 0.6 * math.exp(-0.3 * l)
    smem = pl.BlockSpec(memory_space=pltpu.SMEM)
    lam_specs = [pl.BlockSpec((None, 1, DK), lambda bi, h, ip: (l, 0, 0)) for _ in range(4)]
    nb_tile = (None, None, None, ATT_TQ, ATT_TK)
    nb_specs = ([pl.BlockSpec(nb_tile, lambda bi, h, ip, n=n: (h, 0, n, 0, 0)) for n in range(ATT_NS)]
                + [pl.BlockSpec(nb_tile, lambda bi, h, ip, n=n: (h, jnp.minimum(2 * ip + 2, nq - 1), n, 0, 0))
                   for n in range(ATT_NS)]
                + [pl.BlockSpec(nb_tile, lambda bi, h, ip, n=n: (h, 2 * ip + 1, n, 0, 0)) for n in range(ATT_NS)])
    return pl.pallas_call(
        functools.partial(_attn_kernel, lam_init=lam_init),
        grid=(b, N_HEADS, nq // 2),
        in_specs=[smem, smem, smem] + lam_specs + [
            pl.BlockSpec((None, s, 2 * DK), lambda bi, h, ip: (bi, 0, h)),
            pl.BlockSpec((None, s, 2 * DK), lambda bi, h, ip: (bi, 0, N_HEADS + h)),
            pl.BlockSpec((None, s, DV), lambda bi, h, ip: (bi, 0, 2 * N_HEADS + h)),
        ] + nb_specs + [pl.BlockSpec((None, 1, DV), lambda bi, h, ip: (l, 0, 0))],
        out_specs=pl.BlockSpec((None, 2 * ATT_TQ, DV), lambda bi, h, ip: (bi, ip, h)),
        out_shape=jax.ShapeDtypeStruct((b, s, N_HEADS * DV), BF16),
        scratch_shapes=[pltpu.VMEM((s // ATT_TK, 2 * ATT_TQ, ATT_TK), F32),
                        pltpu.VMEM((s // ATT_TK, 2 * ATT_TQ, ATT_TK), F32),
                        pltpu.VMEM((2 * ATT_TQ, LANES), F32),
                        pltpu.VMEM((2 * ATT_TQ, LANES), F32),
                        pltpu.VMEM((s, 2 * DV), BF16)],
        compiler_params=_cparams(("arbitrary", "arbitrary", "arbitrary"), 56),
        name="diff_attention",
    )(cls, slotj, ctab, *lam4, qkv, qkv, qkv, nbt, nbt, nbt, nbt, nbt, nbt, sub_g3)


def _attn_any_kernel(tab_ref, lq1_ref, lk1_ref, lq2_ref, lk2_ref, q_ref, k_ref, v_ref, pq_ref, pk_ref,
                     sg_ref, o_ref, *, lam_init):
    h = pl.program_id(1)
    tq = q_ref.shape[0]
    bias = _bias_tile(tab_ref, h, pk_ref[...] - pq_ref[...])
    sc = lax.dot_general(_stack_maps(q_ref[...]), k_ref[...], (((1,), (1,)), ((), ())),
                         preferred_element_type=F32)
    v = v_ref[...]

    def one_map(sm):
        sm = sm + bias
        e = jnp.exp2(sm - jnp.max(sm, axis=-1, keepdims=True))
        den = jnp.sum(e, axis=-1, keepdims=True)
        return jnp.dot(e.astype(BF16), v, preferred_element_type=F32) / den

    lam = _lambda(lq1_ref, lk1_ref, lq2_ref, lk2_ref, lam_init)
    o_ref[...] = _attn_finish(one_map(sc[:tq]), one_map(sc[tq:]), lam, sg_ref[...], lam_init, o_ref.dtype)


def _attn_any(qkv, rel_bias, positions, lam4, sub_g3, l):
    b, s, _ = qkv.shape
    tq = 128
    lam_init = 0.8 - 0.6 * math.exp(-0.3 * l)
    lam_specs = [pl.BlockSpec((None, 1, DK), lambda bi, h, i: (l, 0, 0)) for _ in range(4)]
    return pl.pallas_call(
        functools.partial(_attn_any_kernel, lam_init=lam_init),
        grid=(b, N_HEADS, s // tq),
        in_specs=[pl.BlockSpec(memory_space=pltpu.SMEM)] + lam_specs + [
            pl.BlockSpec((None, tq, 2 * DK), lambda bi, h, i: (bi, i, h)),
            pl.BlockSpec((None, s, 2 * DK), lambda bi, h, i: (bi, 0, N_HEADS + h)),
            pl.BlockSpec((None, s, DV), lambda bi, h, i: (bi, 0, 2 * N_HEADS + h)),
            pl.BlockSpec((tq, 1), lambda bi, h, i: (i, 0)),
            pl.BlockSpec((1, s), lambda bi, h, i: (0, 0)),
            pl.BlockSpec((None, 1, DV), lambda bi, h, i: (l, 0, 0)),
        ],
        out_specs=pl.BlockSpec((None, tq, DV), lambda bi, h, i: (bi, i, h)),
        out_shape=jax.ShapeDtypeStruct((b, s, N_HEADS * DV), BF16),
        compiler_params=_cparams(("arbitrary", "arbitrary", "arbitrary"), 48),
        name="diff_attention_any",
    )(rel_bias.reshape(-1), *lam4, qkv, qkv, qkv, positions.reshape(s, 1), positions.reshape(1, s), sub_g3)


def _diff_attention(qkv, nbt, plan, rel_bias, positions, lam4, sub_g3, l):
    return lax.cond(plan[4],
                    lambda: _attn_fast(qkv, nbt, plan, lam4, sub_g3, l),
                    lambda: _attn_any(qkv, rel_bias, positions, lam4, sub_g3, l))


def _post_mix_kernel(x_ref, mo_ref, g1_ref, ng_ref, sc_ref, sh_ref, wr_ref, x1_ref, h_ref, lg_ref):
    x1 = x_ref[...] + g1_ref[...] * mo_ref[...].astype(F32)
    x1_ref[...] = x1
    ms = jnp.mean(x1 * x1, axis=-1, keepdims=True)
    h = x1 * lax.rsqrt(ms + EPS) * ng_ref[...]
    h = h * (1.0 + sc_ref[...]) + sh_ref[...]
    d = h.shape[1]
    wr = wr_ref[...]
    h_hi = h.astype(BF16)
    h_lo = (h - h_hi.astype(F32)).astype(BF16)
    w_hi = wr.astype(BF16)
    w_lo = (wr - w_hi.astype(F32)).astype(BF16)
    lg = (jnp.dot(h_hi, w_hi, preferred_element_type=F32) + jnp.dot(h_lo, w_hi, preferred_element_type=F32)
          + jnp.dot(h_hi, w_lo, preferred_element_type=F32))
    lane = lax.broadcasted_iota(I32, lg.shape, 1)
    valid = lane < N_EXPERTS
    lgm = jnp.where(valid, lg, -1e30)
    ex = jnp.where(valid, jnp.exp(lgm - jnp.max(lgm, axis=-1, keepdims=True)), 0.0)
    aff = ex / jnp.sum(ex, axis=-1, keepdims=True)
    h_ref[:, :d] = h
    h_ref[:, d:] = aff
    lg_ref[...] = aff


def _post_mix(x, mixo, modr, norm2_g3, w_router_p, l):
    b, s, d = x.shape
    tm = 256
    mspec = lambda idx: pl.BlockSpec((None, None, None, 1, d), lambda bi, i: (l, bi, idx, 0, 0))
    return pl.pallas_call(
        _post_mix_kernel,
        grid=(b, s // tm),
        in_specs=[
            pl.BlockSpec((None, tm, d), lambda bi, i: (bi, i, 0)),
            pl.BlockSpec((None, tm, d), lambda bi, i: (bi, i, 0)),
            mspec(2),
            pl.BlockSpec((None, 1, d), lambda bi, i: (l, 0, 0)),
            mspec(4),
            mspec(3),
            pl.BlockSpec((None, d, LANES), lambda bi, i: (l, 0, 0)),
        ],
        out_specs=[
            pl.BlockSpec((None, tm, d), lambda bi, i: (bi, i, 0)),
            pl.BlockSpec((None, tm, d + LANES), lambda bi, i: (bi, i, 0)),
            pl.BlockSpec((None, tm, LANES), lambda bi, i: (bi, i, 0)),
        ],
        out_shape=[
            jax.ShapeDtypeStruct((b, s, d), F32),
            jax.ShapeDtypeStruct((b, s, d + LANES), F32),
            jax.ShapeDtypeStruct((b, s, LANES), F32),
        ],
        compiler_params=_cparams(("arbitrary", "arbitrary")),
        name="post_mix",
    )(x, mixo, modr, norm2_g3, modr, modr, w_router_p)


def _cumsum_lanes(x01):
    rows, n = x01.shape
    blk = MOE_TS
    ii = lax.broadcasted_iota(I32, (blk, blk), 0)
    jj = lax.broadcasted_iota(I32, (blk, blk), 1)
    tri = jnp.where(ii <= jj, 1.0, 0.0).astype(BF16)
    xb = x01.astype(BF16)
    carry = jnp.zeros((rows, 1), F32)
    outs, totals = [], [carry]
    for c in range(n // blk):
        part = jnp.dot(xb[:, c * blk:(c + 1) * blk], tri, preferred_element_type=F32) + carry
        outs.append(part)
        carry = part[:, blk - 1:blk]
        totals.append(carry)
    return jnp.concatenate(outs, axis=1), totals


def _route_kernel(aff_ref, idx_ref, slott_ref, lo_ref, cs_scr, *, cap):
    at = aff_ref[...].T[:N_EXPERTS, :]
    keys = lax.bitcast_convert_type(at, I32)

    def body(i, prefix):
        cand = prefix | jnp.left_shift(jnp.int32(1), 30 - i)
        cnt = jnp.sum(jnp.where(keys >= cand, 1.0, 0.0), axis=1, keepdims=True)
        return jnp.where(cnt >= cap, cand, prefix)

    thr = lax.fori_loop(0, 31, body, jnp.zeros((N_EXPERTS, 1), I32))
    gt = jnp.where(keys > thr, 1.0, 0.0)
    eq = jnp.where(keys == thr, 1.0, 0.0)
    need = cap - jnp.sum(gt, axis=1, keepdims=True)
    take = eq * jnp.where(_cumsum_lanes(eq)[0] <= need, 1.0, 0.0)
    sel = gt + take
    cs, totals = _cumsum_lanes(sel)
    slot = jnp.where(sel > 0.5, cs - 1.0, -1.0).astype(I32)
    s = slot.shape[1]
    pad = jnp.full((LANES - N_EXPERTS, s), -1, I32)
    slott_ref[...] = jnp.concatenate([slot, pad], axis=0).T

    lane_e = lax.broadcasted_iota(I32, (N_EXPERTS, LANES), 1)
    lo = jnp.zeros((N_EXPERTS, LANES), F32)
    for j, tot in enumerate(totals):
        lo = jnp.where(lane_e == j, tot, lo)
    lo_ref[...] = lo.astype(I32)

    cs_scr[...] = cs
    lane = lax.broadcasted_iota(I32, (cap, LANES), 1)
    rows = 128

    def one_expert(e, idxt):
        row = cs_scr[pl.ds(e, 1), :]
        cols = []
        for c0 in range(0, cap, rows):
            cio = (lax.broadcasted_iota(I32, (rows, s), 0) + c0).astype(F32)
            cols.append(jnp.sum(jnp.where(row <= cio, 1.0, 0.0), axis=1, keepdims=True))
        return jnp.where(lane == e, jnp.concatenate(cols, axis=0), idxt)

    idxt = lax.fori_loop(0, N_EXPERTS, one_expert, jnp.zeros((cap, LANES), F32))
    idx_ref[...] = idxt.T[:N_EXPERTS, :].astype(I32)


def _route(aff, cap):
    b, s, _ = aff.shape
    return pl.pallas_call(
        functools.partial(_route_kernel, cap=cap),
        grid=(b,),
        in_specs=[pl.BlockSpec((None, s, LANES), lambda bi: (bi, 0, 0))],
        out_specs=[
            pl.BlockSpec((None, N_EXPERTS, cap), lambda bi: (bi, 0, 0)),
            pl.BlockSpec((None, s, LANES), lambda bi: (bi, 0, 0)),
            pl.BlockSpec((None, N_EXPERTS, LANES), lambda bi: (bi, 0, 0)),
        ],
        out_shape=[
            jax.ShapeDtypeStruct((b, N_EXPERTS, cap), I32),
            jax.ShapeDtypeStruct((b, s, LANES), I32),
            jax.ShapeDtypeStruct((b, N_EXPERTS, LANES), I32),
        ],
        scratch_shapes=[pltpu.VMEM((N_EXPERTS, s), F32)],
        compiler_params=_cparams(("arbitrary",), 48),
        name="route",
    )(aff)


def _expert_kernel(idx_ref, h_ref, w1_ref, w3_ref, w2_ref, y_ref, rows_ref, xb_ref, acc_ref, gate_ref, sem):
    e = pl.program_id(0)
    f = pl.program_id(1)
    ne = pl.num_programs(0)
    nf = pl.num_programs(1)
    nb, cap, d = y_ref.shape
    n_rows = nb * cap
    per_step = n_rows // MOE_NF

    steps_per_batch = cap // per_step
    assert steps_per_batch & (steps_per_batch - 1) == 0 and steps_per_batch * per_step == cap

    def wait_all_rows():
        for ff in range(MOE_NF):
            pltpu.make_async_copy(h_ref.at[0, pl.ds(0, per_step), :], rows_ref.at[ff], sem).wait()

    @pl.when(jnp.logical_and(e == 0, f == 0))
    def _():
        for ff in range(MOE_NF):
            src = (ff // steps_per_batch) * N_EXPERTS * cap + (ff % steps_per_batch) * per_step

            def body(r, carry, ff=ff, src=src):
                pltpu.make_async_copy(h_ref.at[ff // steps_per_batch, pl.ds(idx_ref[src + r], 1), :],
                                      rows_ref.at[ff, pl.ds(r, 1), :], sem).start()
                return carry
            lax.fori_loop(0, per_step, body, 0)

    @pl.when(f == 0)
    def _():
        wait_all_rows()
        lane = lax.broadcasted_iota(I32, (per_step, LANES), 1)
        for ff in range(MOE_NF):
            rs = slice(ff * per_step, (ff + 1) * per_step)
            xb_ref[rs, :] = rows_ref[ff, :, :d].astype(BF16)
            gate_ref[rs, :] = jnp.sum(jnp.where(lane == e, rows_ref[ff, :, d:], 0.0), axis=1, keepdims=True)
        acc_ref[...] = jnp.zeros(acc_ref.shape, F32)

    e_next = jnp.minimum(e + 1, ne - 1)
    bi = lax.shift_right_logical(f, steps_per_batch.bit_length() - 1)
    c0 = (f & (steps_per_batch - 1)) * per_step
    src0 = (bi * N_EXPERTS + e_next) * cap + c0
    for r in range(per_step):
        pltpu.make_async_copy(h_ref.at[bi, pl.ds(idx_ref[src0 + r], 1), :],
                              rows_ref.at[f, pl.ds(r, 1), :], sem).start()

    xg = xb_ref[...]
    a = jnp.dot(xg, w1_ref[...].astype(BF16), preferred_element_type=F32)
    g = jnp.dot(xg, w3_ref[...].astype(BF16), preferred_element_type=F32)
    act = (_silu(a) * g).astype(BF16)
    acc_ref[...] += jnp.dot(act, w2_ref[...].astype(BF16), preferred_element_type=F32)

    @pl.when(f == nf - 1)
    def _():
        y_ref[...] = (acc_ref[...] * gate_ref[...]).reshape(nb, cap, d).astype(y_ref.dtype)

    @pl.when(jnp.logical_and(e == ne - 1, f == nf - 1))
    def _():
        wait_all_rows()


def _experts(idx, h2, w1, w3, w2, l, cap):
    b, s, dext = h2.shape
    d = dext - LANES
    e = w1.shape[1]
    ff = w1.shape[-1]
    fc = ff // MOE_NF
    grid_spec = pltpu.PrefetchScalarGridSpec(
        num_scalar_prefetch=1,
        grid=(e, MOE_NF),
        in_specs=[
            pl.BlockSpec(memory_space=pl.ANY),
            pl.BlockSpec((None, None, d, fc), lambda ei, f, ix: (l, ei, 0, f)),
            pl.BlockSpec((None, None, d, fc), lambda ei, f, ix: (l, ei, 0, f)),
            pl.BlockSpec((None, None, fc, d), lambda ei, f, ix: (l, ei, f, 0)),
        ],
        out_specs=pl.BlockSpec((b, None, cap, d), lambda ei, f, ix: (0, ei, 0, 0)),
        scratch_shapes=[pltpu.VMEM((MOE_NF, b * cap // MOE_NF, dext), F32), pltpu.VMEM((b * cap, d), BF16),
                        pltpu.VMEM((b * cap, d), F32), pltpu.VMEM((b * cap, 1), F32),
                        pltpu.SemaphoreType.DMA],
    )
    return pl.pallas_call(
        _expert_kernel,
        grid_spec=grid_spec,
        out_shape=jax.ShapeDtypeStruct((b, e, cap, d), BF16),
        compiler_params=_cparams(("arbitrary", "arbitrary"), 56),
        name="moe_experts",
    )(idx.reshape(-1), h2, w1, w3, w2)


def _scatter_kernel(lo_ref, slott_ref, y_ref, x1_ref, g2_ref, ng_ref, *rest, final_norm):
    if final_norm:
        o_ref, win_ref, full_ref, oh_ref, wsem, fsem = rest
    else:
        sc_ref, sh_ref, o_ref, h_ref, win_ref, full_ref, oh_ref, wsem, fsem = rest
    _scatter_body(lo_ref, slott_ref, y_ref, x1_ref, g2_ref, o_ref, win_ref, full_ref, oh_ref, wsem, fsem)
    xo = o_ref[...]
    xn = xo * lax.rsqrt(jnp.mean(xo * xo, axis=-1, keepdims=True) + EPS) * ng_ref[...]
    if final_norm:
        o_ref[...] = xn
    else:
        h_ref[...] = (xn * (1.0 + sc_ref[...]) + sh_ref[...]).astype(h_ref.dtype)


def _scatter_body(lo_ref, slott_ref, y_ref, x1_ref, g2_ref, o_ref, win_ref, full_ref, oh_ref, wsem, fsem):
    bi = pl.program_id(0)
    ti = pl.program_id(1)
    nb = pl.num_programs(0)
    nt = pl.num_programs(1)
    ts = slott_ref.shape[0]
    cap = y_ref.shape[2]
    step = bi * nt + ti
    buf = step & 1

    def lo_at(b_, t_, e):
        return lo_ref[(b_ * N_EXPERTS + e) * LANES + t_]

    def win_start(b_, t_, e):
        start = lax.shift_left(lax.shift_right_logical(lo_at(b_, t_, e), 4), 4)
        return pl.multiple_of(jnp.minimum(start, cap - MOE_WIN), 16)

    def win_copy(b_, t_, e, slot):
        return pltpu.make_async_copy(y_ref.at[b_, e, pl.ds(win_start(b_, t_, e), MOE_WIN), :],
                                     win_ref.at[slot, pl.ds(e * MOE_WIN, MOE_WIN), :], wsem.at[slot])

    @pl.when(step == 0)
    def _():
        for e in range(N_EXPERTS):
            win_copy(0, 0, e, 0).start()

    @pl.when(step + 1 < nb * nt)
    def _():
        wrap = ti + 1 == nt
        b_n = jnp.where(wrap, bi + 1, bi)
        t_n = jnp.where(wrap, 0, ti + 1)
        for e in range(N_EXPERTS):
            win_copy(b_n, t_n, e, 1 - buf).start()

    for e in range(N_EXPERTS):
        win_copy(bi, ti, e, buf).wait()

    lane = lax.broadcasted_iota(I32, (ts, LANES), 1)

    def slot_col(e):
        return jnp.sum(jnp.where(lane == e, slott_ref[...].astype(F32), 0.0), axis=1, keepdims=True).astype(I32)

    fits = None
    for e in range(N_EXPERTS):
        ok = lo_at(bi, ti + 1, e) - win_start(bi, ti, e) <= MOE_WIN
        fits = ok if fits is None else jnp.logical_and(fits, ok)

    @pl.when(fits)
    def _():
        widx = lax.broadcasted_iota(I32, (ts, MOE_WIN), 1)
        for e in range(N_EXPERTS):
            oh_ref[:, e * MOE_WIN:(e + 1) * MOE_WIN] = jnp.where(
                widx == slot_col(e) - win_start(bi, ti, e), 1.0, 0.0).astype(BF16)
        moe = jnp.dot(oh_ref[...], win_ref[buf], preferred_element_type=F32)
        o_ref[...] = x1_ref[...] + g2_ref[...] * moe

    @pl.when(jnp.logical_not(fits))
    def _():
        o_ref[...] = x1_ref[...]
        cidx = lax.broadcasted_iota(I32, (ts, cap), 1)

        def one_expert(e, carry):
            cp = pltpu.make_async_copy(y_ref.at[bi, e], full_ref, fsem)
            cp.start()
            cp.wait()
            onehot = jnp.where(cidx == slot_col(e), 1.0, 0.0).astype(BF16)
            o_ref[...] += g2_ref[...] * jnp.dot(onehot, full_ref[...], preferred_element_type=F32)
            return carry

        lax.fori_loop(0, N_EXPERTS, one_expert, 0)


def _scatter(lo, slott, y, x1, modr, l, next_g3, final_norm):
    b, s, d = x1.shape
    cap = y.shape[2]
    assert cap >= MOE_WIN and s // MOE_TS < LANES
    row_tile = pl.BlockSpec((None, MOE_TS, d), lambda bi, i, lo_: (bi, i, 0))
    ln = 0 if final_norm else l + 1
    in_specs = [
        pl.BlockSpec((None, MOE_TS, LANES), lambda bi, i, lo_: (bi, i, 0)),
        pl.BlockSpec(memory_space=pl.ANY),
        row_tile,
        pl.BlockSpec((None, None, None, 1, d), lambda bi, i, lo_: (l, bi, 5, 0, 0)),
        pl.BlockSpec((None, 1, d), lambda bi, i, lo_: (ln, 0, 0)),
    ]
    args = [lo.reshape(-1), slott, y, x1, modr, next_g3]
    out_specs = [row_tile]
    out_shape = [jax.ShapeDtypeStruct((b, s, d), F32)]
    if not final_norm:
        in_specs += [pl.BlockSpec((None, None, None, 1, d), lambda bi, i, lo_: (ln, bi, 1, 0, 0)),
                     pl.BlockSpec((None, None, None, 1, d), lambda bi, i, lo_: (ln, bi, 0, 0, 0))]
        args += [modr, modr]
        out_specs.append(row_tile)
        out_shape.append(jax.ShapeDtypeStruct((b, s, d), BF16))
    grid_spec = pltpu.PrefetchScalarGridSpec(
        num_scalar_prefetch=1,
        grid=(b, s // MOE_TS),
        in_specs=in_specs,
        out_specs=out_specs,
        scratch_shapes=[pltpu.VMEM((2, N_EXPERTS * MOE_WIN, d), BF16), pltpu.VMEM((cap, d), BF16),
                        pltpu.VMEM((MOE_TS, N_EXPERTS * MOE_WIN), BF16),
                        pltpu.SemaphoreType.DMA((2,)), pltpu.SemaphoreType.DMA],
    )
    outs = pl.pallas_call(
        functools.partial(_scatter_kernel, final_norm=final_norm),
        grid_spec=grid_spec,
        out_shape=out_shape,
        compiler_params=_cparams(("arbitrary", "arbitrary"), 58),
        name="moe_scatter",
    )(*args)
    return (outs[0], None) if final_norm else (outs[0], outs[1])


def kernel(x, c, positions, w_mod, b_mod, norm1_g, norm2_g, w_in, fnet_w, pool_w, pool_scale,
           lam_q1, lam_k1, lam_q2, lam_k2, sub_g, rel_bias, w_out, w_router, w1, w3, w2, final_g):
    b, s, d = x.shape
    depth = w_mod.shape[0]
    fnet_wd = fnet_w.shape[1] * fnet_w.shape[2]
    pool_wd = pool_w.shape[1] * pool_w.shape[2]
    ab_w = fnet_wd + pool_wd
    in_w = w_in.shape[-1]
    cap = EC_CAPACITY * s // N_EXPERTS

    c8 = jnp.zeros((8, d), F32).at[:b].set(c)
    mod = _modulation(c8, w_mod, b_mod.reshape(depth, 1, N_MOD * d))
    modr = mod[:, :b].reshape(depth, b, N_MOD, 1, d)

    norm1_g3 = norm1_g.reshape(depth, 1, d)
    norm2_g3 = norm2_g.reshape(depth, 1, d)
    pool_scale3 = pool_scale.reshape(depth, 1, pool_wd)
    sub_g3 = sub_g.reshape(depth, 1, DV)
    lam4 = [a.reshape(depth, 1, DK) for a in (lam_q1, lam_k1, lam_q2, lam_k2)]
    w_router_p = jnp.zeros((depth, d, LANES), F32).at[:, :, :N_EXPERTS].set(w_router)

    wd = _dft_matrix(s)
    ab = _fnet_weights(fnet_w, s)
    plan = _attn_plan(positions, rel_bias)
    nbt = _near_bias_table(rel_bias, positions, plan[1], plan[2])

    h1 = _norm_mod(x, norm1_g3, modr, 0, 0, 1)
    for l in range(depth):
        last = l == depth - 1
        h1 = h1.reshape(b * s, d)
        u_ab = _project([h1], w_in, l, 0, ab_w, F32, tn=1024, name="proj_in_ab").reshape(b, s, ab_w)
        qkv = _project([h1], w_in, l, ab_w, in_w - ab_w, BF16, tn=1024, lead_cols=N_HEADS * 2 * DK,
                       lead_scale=LOG2E * DK ** -0.5, name="proj_in_qkv").reshape(b, s, in_w - ab_w)

        za, zb = _fnet_z(u_ab, ab, l)
        ya = _dft_apply(wd, za, zb, b)
        yb = _pool_mixer(u_ab, pool_w, pool_scale3, l, fnet_wd // pool_w.shape[2])
        yc = _diff_attention(qkv, nbt, plan, rel_bias, positions, lam4, sub_g3, l)

        mixo = _project([ya.reshape(b * s, -1), yb.reshape(b * s, -1), yc.reshape(b * s, -1)],
                        w_out, l, 0, d, BF16, tn=1024, name="proj_out").reshape(b, s, d)
        x1, h2, aff = _post_mix(x, mixo, modr, norm2_g3, w_router_p, l)

        idx, slott, lo = _route(aff, cap)
        y = _experts(idx, h2, w1, w3, w2, l, cap)
        x, h1 = _scatter(lo, slott, y, x1, modr, l, final_g.reshape(1, 1, d) if last else norm1_g3,
                         final_norm=last)

    return x
```

```python
import functools
import math

import numpy as np
import jax
import jax.numpy as jnp
from jax import lax
from jax.experimental import pallas as pl
from jax.experimental.pallas import tpu as pltpu

F32 = jnp.float32
BF16 = jnp.bfloat16
I32 = jnp.int32

FNET_GROUPS = 4
POOL_GROUPS = 4
POOL_HALO = 64
N_HEADS = 8
DK = 64
DV = 128
N_BUCKETS = 32
MAX_DISTANCE = 128
N_EXPERTS = 16
EC_CAPACITY = 2
N_MOD = 6
EPS = 1e-6
LANES = 128
LOG2E = 1.4426950408889634
ATT_TQ = 256
ATT_TK = 512
ATT_NS = 2
MOE_NF = 4
MOE_TS = 512
MOE_WIN = 128


def _cparams(sem, vmem_mb=None):
    kw = dict(dimension_semantics=sem)
    if vmem_mb is not None:
        kw["vmem_limit_bytes"] = vmem_mb * 1024 * 1024
    return pltpu.CompilerParams(**kw)


def _silu(x):
    return x * jax.nn.sigmoid(x)


def _mod_kernel(c_ref, w_ref, b_ref, o_ref):
    ca = _silu(c_ref[...]).astype(BF16)
    o_ref[...] = jnp.dot(ca, w_ref[...].astype(BF16), preferred_element_type=F32) + b_ref[...]


def _modulation(c8, w_mod, b_mod3):
    depth, d, n = w_mod.shape
    tn = 1024
    return pl.pallas_call(
        _mod_kernel,
        grid=(depth, n // tn),
        in_specs=[
            pl.BlockSpec((8, d), lambda l, j: (0, 0)),
            pl.BlockSpec((None, d, tn), lambda l, j: (l, 0, j)),
            pl.BlockSpec((None, 1, tn), lambda l, j: (l, 0, j)),
        ],
        out_specs=pl.BlockSpec((None, 8, tn), lambda l, j: (l, 0, j)),
        out_shape=jax.ShapeDtypeStruct((depth, 8, n), F32),
        compiler_params=_cparams(("arbitrary", "arbitrary")),
        name="modulation",
    )(c8, w_mod, b_mod3)


def _norm_mod_kernel(x_ref, g_ref, sc_ref, sh_ref, o_ref):
    x = x_ref[...]
    ms = jnp.mean(x * x, axis=-1, keepdims=True)
    y = x * lax.rsqrt(ms + EPS) * g_ref[...]
    o_ref[...] = (y * (1.0 + sc_ref[...]) + sh_ref[...]).astype(o_ref.dtype)


def _norm_mod(x, g3, modr, l, sh_idx, sc_idx):
    b, s, d = x.shape
    tm = 512
    return pl.pallas_call(
        _norm_mod_kernel,
        grid=(b, s // tm),
        in_specs=[
            pl.BlockSpec((None, tm, d), lambda bi, i: (bi, i, 0)),
            pl.BlockSpec((None, 1, d), lambda bi, i: (l, 0, 0)),
            pl.BlockSpec((None, None, None, 1, d), lambda bi, i: (l, bi, sc_idx, 0, 0)),
            pl.BlockSpec((None, None, None, 1, d), lambda bi, i: (l, bi, sh_idx, 0, 0)),
        ],
        out_specs=pl.BlockSpec((None, tm, d), lambda bi, i: (bi, i, 0)),
        out_shape=jax.ShapeDtypeStruct((b, s, d), BF16),
        compiler_params=_cparams(("arbitrary", "arbitrary")),
        name="norm_mod",
    )(x, g3, modr, modr)


def _mm_kernel(*refs, k_sizes, lead_blocks, lead_scale):
    n_a = len(k_sizes)
    a_refs = refs[:n_a]
    w_ref, o_ref, wb_ref = refs[n_a], refs[n_a + 1], refs[n_a + 2]

    @pl.when(pl.program_id(1) == 0)
    def _():
        wb_ref[...] = w_ref[...].astype(BF16)

    acc = None
    off = 0
    for a_ref, ks in zip(a_refs, k_sizes):
        part = jnp.dot(a_ref[...], wb_ref[off:off + ks, :], preferred_element_type=F32)
        acc = part if acc is None else acc + part
        off += ks
    if lead_blocks:
        acc = acc * jnp.where(pl.program_id(0) < lead_blocks, lead_scale, 1.0)
    o_ref[...] = acc.astype(o_ref.dtype)


def _project(a_list, w, l, col0, ncols, out_dtype, tm=1024, tn=512, lead_cols=0, lead_scale=1.0,
             name="project"):
    m = a_list[0].shape[0]
    k_sizes = tuple(a.shape[1] for a in a_list)
    k = sum(k_sizes)
    assert w.shape[1] == k and col0 % tn == 0 and ncols % tn == 0 and m % tm == 0 and lead_cols % tn == 0
    cb0 = col0 // tn
    in_specs = [pl.BlockSpec((tm, ks), lambda j, i: (i, 0)) for ks in k_sizes]
    in_specs.append(pl.BlockSpec((None, k, tn), lambda j, i: (l, 0, cb0 + j)))
    return pl.pallas_call(
        functools.partial(_mm_kernel, k_sizes=k_sizes, lead_blocks=lead_cols // tn, lead_scale=lead_scale),
        grid=(ncols // tn, m // tm),
        in_specs=in_specs,
        out_specs=pl.BlockSpec((tm, tn), lambda j, i: (i, j)),
        out_shape=jax.ShapeDtypeStruct((m, ncols), out_dtype),
        scratch_shapes=[pltpu.VMEM((k, tn), BF16)],
        compiler_params=_cparams(("arbitrary", "arbitrary"), 48),
        name=name,
    )(*a_list, w)


def _dft_tables(s):
    sp = np.arange(s, dtype=np.int64)
    a = np.arange(64, dtype=np.int64)[:, None]
    ang1 = 2.0 * np.pi * ((a * sp[None, :]) % 64) / 64.0
    ang2 = 2.0 * np.pi * ((a * sp[None, :]) % s) / float(s)
    t1c = np.cos(ang1).astype(np.float32).reshape(64, 1, s)
    t1s = np.sin(ang1).astype(np.float32).reshape(64, 1, s)
    t2c = np.cos(ang2).astype(np.float32)
    t2s = np.sin(ang2).astype(np.float32)
    return t1c, t1s, t2c, t2s


def _dftgen_kernel(t1c_ref, t1s_ref, t2c_ref, t2s_ref, o_ref):
    s = t2c_ref.shape[1]
    c1, s1 = t1c_ref[...], t1s_ref[...]
    c2, s2 = t2c_ref[...], t2s_ref[...]
    o_ref[:, :s] = (c1 * c2 - s1 * s2).astype(BF16)
    o_ref[:, s:] = (-(s1 * c2 + c1 * s2)).astype(BF16)


def _dft_matrix(s):
    assert s % 64 == 0 and s // 64 == 64
    t1c, t1s, t2c, t2s = _dft_tables(s)
    return pl.pallas_call(
        _dftgen_kernel,
        grid=(64,),
        in_specs=[
            pl.BlockSpec((None, 1, s), lambda a: (a, 0, 0)),
            pl.BlockSpec((None, 1, s), lambda a: (a, 0, 0)),
            pl.BlockSpec((64, s), lambda a: (0, 0)),
            pl.BlockSpec((64, s), lambda a: (0, 0)),
        ],
        out_specs=pl.BlockSpec((64, 2 * s), lambda a: (a, 0)),
        out_shape=jax.ShapeDtypeStruct((s, 2 * s), BF16),
        compiler_params=_cparams(("arbitrary",)),
        name="dft_matrix",
    )(t1c, t1s, t2c, t2s)


def _fnet_w_kernel(cc_ref, sc_ref, w_ref, o_ref, *, norm):
    depth, groups, cg, _ = w_ref.shape
    cc, sc = cc_ref[...], sc_ref[...]
    for l in range(depth):
        for g in range(groups):
            w = w_ref[l, g]
            a = jnp.dot(cc, w, preferred_element_type=F32, precision=lax.Precision.HIGHEST)
            b = jnp.dot(sc, w, preferred_element_type=F32, precision=lax.Precision.HIGHEST)
            o_ref[l, g, :, :cg] = (a * norm).astype(BF16)
            o_ref[l, g, :, cg:] = (b * norm).astype(BF16)


def _fnet_weights(fnet_w, s):
    depth, groups, cg, _ = fnet_w.shape
    idx = np.arange(cg, dtype=np.int64)
    ang = 2.0 * np.pi * ((idx[:, None] * idx[None, :]) % cg) / float(cg)
    cc = np.cos(ang).astype(np.float32)
    sc = np.sin(ang).astype(np.float32)
    norm = 1.0 / math.sqrt(float(s) * float(cg))
    return pl.pallas_call(
        functools.partial(_fnet_w_kernel, norm=norm),
        out_shape=jax.ShapeDtypeStruct((depth, groups, cg, 2 * cg), BF16),
        name="fnet_weights",
    )(cc, sc, fnet_w)


def _fnet_z_kernel(u_ref, ab_ref, za_ref, zb_ref):
    groups, cg = ab_ref.shape[0], ab_ref.shape[1]
    for g in range(groups):
        ug = u_ref[:, g * cg:(g + 1) * cg].astype(BF16)
        z = jnp.dot(ug, ab_ref[g], preferred_element_type=F32)
        za_ref[:, g * cg:(g + 1) * cg] = z[:, :cg].astype(BF16)
        zb_ref[:, g * cg:(g + 1) * cg] = z[:, cg:].astype(BF16)


def _fnet_z(u_ab, ab, l):
    b, s, _ = u_ab.shape
    groups, cg = ab.shape[1], ab.shape[2]
    fw = groups * cg
    ts = 512
    shp = jax.ShapeDtypeStruct((s, b * fw), BF16)
    return pl.pallas_call(
        _fnet_z_kernel,
        grid=(b, s // ts),
        in_specs=[
            pl.BlockSpec((None, ts, fw), lambda bi, i: (bi, i, 0)),
            pl.BlockSpec((None, groups, cg, 2 * cg), lambda bi, i: (l, 0, 0, 0)),
        ],
        out_specs=[pl.BlockSpec((ts, fw), lambda bi, i: (i, bi)),
                   pl.BlockSpec((ts, fw), lambda bi, i: (i, bi))],
        out_shape=[shp, shp],
        compiler_params=_cparams(("arbitrary", "arbitrary")),
        name="fnet_z",
    )(u_ab, ab)


def _dft_apply_kernel(wd_ref, za_ref, zb_ref, o_ref):
    s = za_ref.shape[0]
    acc = jnp.dot(wd_ref[:, :s], za_ref[...], preferred_element_type=F32)
    acc = acc + jnp.dot(wd_ref[:, s:], zb_ref[...], preferred_element_type=F32)
    o_ref[...] = acc.astype(o_ref.dtype)


def _dft_apply(wd, za, zb, b):
    s = wd.shape[0]
    fw = za.shape[1] // b
    tm = 512
    return pl.pallas_call(
        _dft_apply_kernel,
        grid=(b, s // tm),
        in_specs=[
            pl.BlockSpec((tm, 2 * s), lambda bi, i: (i, 0)),
            pl.BlockSpec((s, fw), lambda bi, i: (0, bi)),
            pl.BlockSpec((s, fw), lambda bi, i: (0, bi)),
        ],
        out_specs=pl.BlockSpec((None, tm, fw), lambda bi, i: (bi, i, 0)),
        out_shape=jax.ShapeDtypeStruct((b, s, fw), BF16),
        compiler_params=_cparams(("arbitrary", "arbitrary"), 48),
        name="dft_apply",
    )(wd, za, zb)


def _pool_kernel(u_ref, w_ref, sc_ref, o_ref, pad_ref):
    s, cg = u_ref.shape
    t = 256
    half = jnp.left_shift(jnp.int32(1), pl.program_id(1))
    pad_ref[0:POOL_HALO, :] = jnp.zeros((POOL_HALO, cg), F32)
    pad_ref[s + POOL_HALO:s + 2 * POOL_HALO, :] = jnp.zeros((POOL_HALO, cg), F32)
    pad_ref[POOL_HALO:s + POOL_HALO, :] = u_ref[...]
    ii = lax.broadcasted_iota(I32, (t, t + 2 * POOL_HALO), 0)
    jj = lax.broadcasted_iota(I32, (t, t + 2 * POOL_HALO), 1)
    dlt = jj - ii - POOL_HALO
    band = jnp.where(dlt >= -half, jnp.where(dlt <= half - 1, 1.0, 0.0), 0.0).astype(BF16)
    wb = w_ref[...].astype(BF16)
    scale = sc_ref[...]

    def body(ti, carry):
        r0 = pl.multiple_of(ti * t, t)
        seg = pad_ref[pl.ds(r0, t + 2 * POOL_HALO), :]
        hi = seg.astype(BF16)
        lo = (seg - hi.astype(F32)).astype(BF16)
        win = jnp.dot(band, hi, preferred_element_type=F32) + jnp.dot(band, lo, preferred_element_type=F32)
        gi = r0 + lax.broadcasted_iota(I32, (t, cg), 0)
        lo_i = jnp.maximum(gi - half, 0)
        hi_i = jnp.minimum(gi + half - 1, s - 1)
        cnt = (hi_i - lo_i + 1).astype(F32)
        dmean = win / cnt - seg[POOL_HALO:POOL_HALO + t, :]
        y = jnp.dot(dmean.astype(BF16), wb, preferred_element_type=F32) * scale
        o_ref[pl.ds(r0, t), :] = y.astype(o_ref.dtype)
        return carry

    lax.fori_loop(0, s // t, body, 0)


def _pool_mixer(u_ab, pool_w, pool_scale3, l, col_block0):
    b, s, _ = u_ab.shape
    groups, cg = pool_w.shape[1], pool_w.shape[2]
    return pl.pallas_call(
        _pool_kernel,
        grid=(b, groups),
        in_specs=[
            pl.BlockSpec((None, s, cg), lambda bi, g: (bi, 0, col_block0 + g)),
            pl.BlockSpec((None, None, cg, cg), lambda bi, g: (l, g, 0, 0)),
            pl.BlockSpec((None, 1, cg), lambda bi, g: (l, 0, g)),
        ],
        out_specs=pl.BlockSpec((None, s, cg), lambda bi, g: (bi, 0, g)),
        out_shape=jax.ShapeDtypeStruct((b, s, groups * cg), BF16),
        scratch_shapes=[pltpu.VMEM((s + 2 * POOL_HALO, cg), F32)],
        compiler_params=_cparams(("arbitrary", "arbitrary")),
        name="pool_mixer",
    )(u_ab, pool_w, pool_scale3)


def _bucket(rel):
    nb = N_BUCKETS // 2
    max_exact = nb // 2
    n = jnp.abs(rel)
    nf = jnp.maximum(n, 1).astype(F32)
    large = max_exact + (jnp.log(nf / max_exact) / math.log(MAX_DISTANCE / max_exact)
                         * (nb - max_exact)).astype(I32)
    large = jnp.minimum(large, nb - 1)
    return jnp.where(rel > 0, nb, 0) + jnp.where(n < max_exact, n, large)


def _bias_tile(tab_ref, h, rel):
    bucket = _bucket(rel)
    val = jnp.full(rel.shape, tab_ref[h], F32)
    for j in range(1, N_BUCKETS):
        val = jnp.where(bucket == j, tab_ref[j * N_HEADS + h], val)
    return val * LOG2E


def _near_bias_kernel(slotj_ref, nnear_ref, tabt_ref, pq_ref, pk_ref, o_ref):
    del slotj_ref
    used = jnp.where(pl.program_id(1) < nnear_ref[pl.program_id(0)], 1.0, 0.0)
    bucket = _bucket(pk_ref[...] - pq_ref[...])
    tq, tk = bucket.shape
    for h in range(N_HEADS):
        row = jnp.broadcast_to(tabt_ref[h:h + 1, :], (tq, LANES))
        cols = [jnp.take_along_axis(row, bucket[:, c * LANES:(c + 1) * LANES], axis=1)
                for c in range(tk // LANES)]
        o_ref[h] = jnp.concatenate(cols, axis=1) * used


def _near_bias_table(rel_bias, positions, slotj, nnear):
    s = positions.shape[0]
    nq = s // ATT_TQ
    tabt = jnp.zeros((N_HEADS, LANES), F32).at[:, :N_BUCKETS].set(rel_bias.T * LOG2E)
    grid_spec = pltpu.PrefetchScalarGridSpec(
        num_scalar_prefetch=2,
        grid=(nq, ATT_NS),
        in_specs=[
            pl.BlockSpec((N_HEADS, LANES), lambda i, n, sj, nn: (0, 0)),
            pl.BlockSpec((ATT_TQ, 1), lambda i, n, sj, nn: (i, 0)),
            pl.BlockSpec((1, ATT_TK), lambda i, n, sj, nn: (0, sj[i * ATT_NS + n])),
        ],
        out_specs=pl.BlockSpec((N_HEADS, None, None, ATT_TQ, ATT_TK), lambda i, n, sj, nn: (0, i, n, 0, 0)),
    )
    return pl.pallas_call(
        _near_bias_kernel,
        grid_spec=grid_spec,
        out_shape=jax.ShapeDtypeStruct((N_HEADS, nq, ATT_NS, ATT_TQ, ATT_TK), F32),
        compiler_params=_cparams(("arbitrary", "arbitrary")),
        name="near_bias_table",
    )(slotj, nnear, tabt, positions.reshape(s, 1), positions.reshape(1, s))


def _attn_plan(positions, rel_bias):
    s = positions.shape[0]
    nq, nk = s // ATT_TQ, s // ATT_TK
    pq = positions.reshape(nq, ATT_TQ)
    pk = positions.reshape(nk, ATT_TK)
    rel_min = pk.min(axis=1)[None, :] - pq.max(axis=1)[:, None]
    rel_max = pk.max(axis=1)[None, :] - pq.min(axis=1)[:, None]
    cls = jnp.where(rel_min >= MAX_DISTANCE, 1, jnp.where(rel_max <= -MAX_DISTANCE, 0, 2)).astype(I32)
    near = cls == 2
    nnear = near.sum(axis=1).astype(I32)
    slotj = jnp.argsort(jnp.logical_not(near), axis=1, stable=True)[:, :ATT_NS].astype(I32)
    fits = jnp.all(nnear <= ATT_NS)
    nb = N_BUCKETS // 2
    ctab = jnp.stack([rel_bias[nb - 1], rel_bias[2 * nb - 1], jnp.zeros((N_HEADS,), F32)], axis=1) * LOG2E
    return cls.reshape(-1), slotj.reshape(-1), nnear, ctab.reshape(-1).astype(F32), fits


def _lambda(lq1_ref, lk1_ref, lq2_ref, lk2_ref, lam_init):
    return (jnp.exp(jnp.sum(lq1_ref[...] * lk1_ref[...], axis=-1, keepdims=True))
            - jnp.exp(jnp.sum(lq2_ref[...] * lk2_ref[...], axis=-1, keepdims=True)) + lam_init)


def _stack_maps(q):
    lane = lax.broadcasted_iota(I32, q.shape, 1)
    zero = jnp.zeros_like(q)
    return jnp.concatenate([jnp.where(lane < DK, q, zero), jnp.where(lane >= DK, q, zero)], axis=0)


def _attn_finish(o1, o2, lam, sg, lam_init, dtype):
    o = o1 - lam * o2
    ms = jnp.mean(o * o, axis=-1, keepdims=True)
    y = o * lax.rsqrt(ms + EPS) * sg
    return (y * (1.0 - lam_init)).astype(dtype)


def _attn_kernel(cls_ref, slotj_ref, ctab_ref, lq1_ref, lk1_ref, lq2_ref, lk2_ref,
                 q_ref, k_ref, v_ref, qn_ref, kn_ref, nbp0_ref, nbp1_ref, nba0_ref, nba1_ref, nbb0_ref, nbb1_ref,
                 sg_ref, o_ref, sa_scr, sb_scr, ma_scr, mb_scr, vaug_scr, *, lam_init):
    bi = pl.program_id(0)
    h = pl.program_id(1)
    ip = pl.program_id(2)
    n_b = pl.num_programs(0)
    n_ip = pl.num_programs(2)
    tq = ATT_TQ
    s = k_ref.shape[0]
    nk = s // ATT_TK
    lam = _lambda(lq1_ref, lk1_ref, lq2_ref, lk2_ref, lam_init)
    sg = sg_ref[...]

    def lane_tile_max(blk):
        out = blk[:, :LANES]
        for c in range(1, blk.shape[1] // LANES):
            out = jnp.maximum(out, blk[:, c * LANES:(c + 1) * LANES])
        return out

    def scores(t, hh, qr, kr, s_scr, m_scr, nb_refs):
        qs = _stack_maps(qr[pl.ds(pl.multiple_of(t * tq, tq), tq), :])
        mrow = None
        for j in range(nk):
            c = cls_ref[t * nk + j]
            sc = lax.dot_general(qs, kr[j * ATT_TK:(j + 1) * ATT_TK, :], (((1,), (1,)), ((), ())),
                                 preferred_element_type=F32) + ctab_ref[hh * 3 + c]
            s_scr[j] = sc
            cm = lane_tile_max(sc) + jnp.where(c == 2, -1e30, 0.0)
            mrow = cm if mrow is None else jnp.maximum(mrow, cm)
        for n, nb_ref in enumerate(nb_refs):
            j = slotj_ref[t * ATT_NS + n]
            bias = nb_ref[...]
            top = s_scr[j, 0:tq, :] + bias
            bot = s_scr[j, tq:2 * tq, :] + bias
            s_scr[j, 0:tq, :] = top
            s_scr[j, tq:2 * tq, :] = bot
            mrow = jnp.maximum(mrow, jnp.concatenate([lane_tile_max(top), lane_tile_max(bot)], axis=0))
        m_scr[...] = mrow

    def outputs(s_scr, m_scr, row0):
        m = jnp.max(m_scr[...], axis=-1, keepdims=True)
        acc = None
        for j in range(nk):
            e = jnp.exp2(s_scr[j] - m).astype(BF16)
            part = jnp.dot(e, vaug_scr[j * ATT_TK:(j + 1) * ATT_TK, :], preferred_element_type=F32)
            acc = part if acc is None else acc + part
        o1 = acc[:tq, :DV] / acc[:tq, DV:DV + 1]
        o2 = acc[tq:, :DV] / acc[tq:, DV:DV + 1]
        o_ref[row0:row0 + tq, :] = _attn_finish(o1, o2, lam, sg, lam_init, o_ref.dtype)

    @pl.when(ip == 0)
    def _():
        vaug_scr[:, :DV] = v_ref[...]
        lane = lax.broadcasted_iota(I32, (s, DV), 1)
        vaug_scr[:, DV:] = jnp.where(lane == 0, 1.0, 0.0).astype(BF16)

    @pl.when(jnp.logical_and(jnp.logical_and(bi == 0, h == 0), ip == 0))
    def _():
        scores(0, h, q_ref, k_ref, sa_scr, ma_scr, (nbp0_ref, nbp1_ref))

    last = ip == n_ip - 1
    _, h_next = _next_head(bi, h, last, n_b)
    t_next = jnp.where(last, 0, 2 * ip + 2)
    scores(2 * ip + 1, h, q_ref, k_ref, sb_scr, mb_scr, (nbb0_ref, nbb1_ref))
    outputs(sa_scr, ma_scr, 0)
    scores(t_next, h_next, qn_ref, kn_ref, sa_scr, ma_scr, (nba0_ref, nba1_ref))
    outputs(sb_scr, mb_scr, tq)


def _next_head(bi, h, last, n_b):
    lin = jnp.minimum(bi * N_HEADS + h + jnp.where(last, 1, 0), n_b * N_HEADS - 1)
    return lin // N_HEADS, lin % N_HEADS


def _attn_fast(qkv, nbt, plan, lam4, sub_g3, l):
    b, s, _ = qkv.shape
    cls, slotj, _, ctab, _ = plan
    nq = s // ATT_TQ
    n_ip = nq // 2
    assert nq % 2 == 0 and ATT_NS == 2
    lam_init = 0.8 - 0.6 * math.exp(-0.3 * l)
    smem = pl.BlockSpec(memory_space=pltpu.SMEM)
    lam_specs = [pl.BlockSpec((None, 1, DK), lambda bi, h, ip: (l, 0, 0)) for _ in range(4)]
    nb_tile = (None, None, None, ATT_TQ, ATT_TK)

    def nxt(bi, h, ip):
        return _next_head(bi, h, ip == n_ip - 1, b)

    def nba_map(n):
        def index_map(bi, h, ip):
            return (nxt(bi, h, ip)[1], jnp.where(ip == n_ip - 1, 0, 2 * ip + 2), n, 0, 0)
        return index_map

    nb_specs = ([pl.BlockSpec(nb_tile, lambda bi, h, ip, n=n: (0, 0, n, 0, 0)) for n in range(ATT_NS)]
                + [pl.BlockSpec(nb_tile, nba_map(n)) for n in range(ATT_NS)]
                + [pl.BlockSpec(nb_tile, lambda bi, h, ip, n=n: (h, 2 * ip + 1, n, 0, 0)) for n in range(ATT_NS)])
    return pl.pallas_call(
        functools.partial(_attn_kernel, lam_init=lam_init),
        grid=(b, N_HEADS, n_ip),
        in_specs=[smem, smem, smem] + lam_specs + [
            pl.BlockSpec((None, s, 2 * DK), lambda bi, h, ip: (bi, 0, h)),
            pl.BlockSpec((None, s, 2 * DK), lambda bi, h, ip: (bi, 0, N_HEADS + h)),
            pl.BlockSpec((None, s, DV), lambda bi, h, ip: (bi, 0, 2 * N_HEADS + h)),
            pl.BlockSpec((None, s, 2 * DK), lambda bi, h, ip: (nxt(bi, h, ip)[0], 0, nxt(bi, h, ip)[1])),
            pl.BlockSpec((None, s, 2 * DK), lambda bi, h, ip: (nxt(bi, h, ip)[0], 0, N_HEADS + nxt(bi, h, ip)[1])),
        ] + nb_specs + [pl.BlockSpec((None, 1, DV), lambda bi, h, ip: (l, 0, 0))],
        out_specs=pl.BlockSpec((None, 2 * ATT_TQ, DV), lambda bi, h, ip: (bi, ip, h)),
        out_shape=jax.ShapeDtypeStruct((b, s, N_HEADS * DV), BF16),
        scratch_shapes=[pltpu.VMEM((s // ATT_TK, 2 * ATT_TQ, ATT_TK), F32),
                        pltpu.VMEM((s // ATT_TK, 2 * ATT_TQ, ATT_TK), F32),
                        pltpu.VMEM((2 * ATT_TQ, LANES), F32),
                        pltpu.VMEM((2 * ATT_TQ, LANES), F32),
                        pltpu.VMEM((s, 2 * DV), BF16)],
        compiler_params=_cparams(("arbitrary", "arbitrary", "arbitrary"), 56),
        name="diff_attention",
    )(cls, slotj, ctab, *lam4, qkv, qkv, qkv, qkv, qkv, nbt, nbt, nbt, nbt, nbt, nbt, sub_g3)


def _attn_any_kernel(tab_ref, lq1_ref, lk1_ref, lq2_ref, lk2_ref, q_ref, k_ref, v_ref, pq_ref, pk_ref,
                     sg_ref, o_ref, *, lam_init):
    h = pl.program_id(1)
    tq = q_ref.shape[0]
    bias = _bias_tile(tab_ref, h, pk_ref[...] - pq_ref[...])
    sc = lax.dot_general(_stack_maps(q_ref[...]), k_ref[...], (((1,), (1,)), ((), ())),
                         preferred_element_type=F32)
    v = v_ref[...]

    def one_map(sm):
        sm = sm + bias
        e = jnp.exp2(sm - jnp.max(sm, axis=-1, keepdims=True))
        den = jnp.sum(e, axis=-1, keepdims=True)
        return jnp.dot(e.astype(BF16), v, preferred_element_type=F32) / den

    lam = _lambda(lq1_ref, lk1_ref, lq2_ref, lk2_ref, lam_init)
    o_ref[...] = _attn_finish(one_map(sc[:tq]), one_map(sc[tq:]), lam, sg_ref[...], lam_init, o_ref.dtype)


def _attn_any(qkv, rel_bias, positions, lam4, sub_g3, l):
    b, s, _ = qkv.shape
    tq = 128
    lam_init = 0.8 - 0.6 * math.exp(-0.3 * l)
    lam_specs = [pl.BlockSpec((None, 1, DK), lambda bi, h, i: (l, 0, 0)) for _ in range(4)]
    return pl.pallas_call(
        functools.partial(_attn_any_kernel, lam_init=lam_init),
        grid=(b, N_HEADS, s // tq),
        in_specs=[pl.BlockSpec(memory_space=pltpu.SMEM)] + lam_specs + [
            pl.BlockSpec((None, tq, 2 * DK), lambda bi, h, i: (bi, i, h)),
            pl.BlockSpec((None, s, 2 * DK), lambda bi, h, i: (bi, 0, N_HEADS + h)),
            pl.BlockSpec((None, s, DV), lambda bi, h, i: (bi, 0, 2 * N_HEADS + h)),
            pl.BlockSpec((tq, 1), lambda bi, h, i: (i, 0)),
            pl.BlockSpec((1, s), lambda bi, h, i: (0, 0)),
            pl.BlockSpec((None, 1, DV), lambda bi, h, i: (l, 0, 0)),
        ],
        out_specs=pl.BlockSpec((None, tq, DV), lambda bi, h, i: (bi, i, h)),
        out_shape=jax.ShapeDtypeStruct((b, s, N_HEADS * DV), BF16),
        compiler_params=_cparams(("arbitrary", "arbitrary", "arbitrary"), 48),
        name="diff_attention_any",
    )(rel_bias.reshape(-1), *lam4, qkv, qkv, qkv, positions.reshape(s, 1), positions.reshape(1, s), sub_g3)


def _diff_attention(qkv, nbt, plan, rel_bias, positions, lam4, sub_g3, l):
    return lax.cond(plan[4],
                    lambda: _attn_fast(qkv, nbt, plan, lam4, sub_g3, l),
                    lambda: _attn_any(qkv, rel_bias, positions, lam4, sub_g3, l))


def _post_mix_kernel(x_ref, mo_ref, g1_ref, ng_ref, sc_ref, sh_ref, wr_ref, x1_ref, h_ref, lg_ref):
    x1 = x_ref[...] + g1_ref[...] * mo_ref[...].astype(F32)
    x1_ref[...] = x1
    ms = jnp.mean(x1 * x1, axis=-1, keepdims=True)
    h = x1 * lax.rsqrt(ms + EPS) * ng_ref[...]
    h = h * (1.0 + sc_ref[...]) + sh_ref[...]
    d = h.shape[1]
    wr = wr_ref[...]
    h_hi = h.astype(BF16)
    h_lo = (h - h_hi.astype(F32)).astype(BF16)
    w_hi = wr.astype(BF16)
    w_lo = (wr - w_hi.astype(F32)).astype(BF16)
    lg = (jnp.dot(h_hi, w_hi, preferred_element_type=F32) + jnp.dot(h_lo, w_hi, preferred_element_type=F32)
          + jnp.dot(h_hi, w_lo, preferred_element_type=F32))
    lane = lax.broadcasted_iota(I32, lg.shape, 1)
    valid = lane < N_EXPERTS
    lgm = jnp.where(valid, lg, -1e30)
    ex = jnp.where(valid, jnp.exp(lgm - jnp.max(lgm, axis=-1, keepdims=True)), 0.0)
    aff = ex / jnp.sum(ex, axis=-1, keepdims=True)
    h_ref[:, :d] = h
    h_ref[:, d:] = aff
    lg_ref[...] = aff


def _post_mix(x, mixo, modr, norm2_g3, w_router_p, l):
    b, s, d = x.shape
    tm = 256
    mspec = lambda idx: pl.BlockSpec((None, None, None, 1, d), lambda bi, i: (l, bi, idx, 0, 0))
    return pl.pallas_call(
        _post_mix_kernel,
        grid=(b, s // tm),
        in_specs=[
            pl.BlockSpec((None, tm, d), lambda bi, i: (bi, i, 0)),
            pl.BlockSpec((None, tm, d), lambda bi, i: (bi, i, 0)),
            mspec(2),
            pl.BlockSpec((None, 1, d), lambda bi, i: (l, 0, 0)),
            mspec(4),
            mspec(3),
            pl.BlockSpec((None, d, LANES), lambda bi, i: (l, 0, 0)),
        ],
        out_specs=[
            pl.BlockSpec((None, tm, d), lambda bi, i: (bi, i, 0)),
            pl.BlockSpec((None, tm, d + LANES), lambda bi, i: (bi, i, 0)),
            pl.BlockSpec((None, tm, LANES), lambda bi, i: (bi, i, 0)),
        ],
        out_shape=[
            jax.ShapeDtypeStruct((b, s, d), F32),
            jax.ShapeDtypeStruct((b, s, d + LANES), F32),
            jax.ShapeDtypeStruct((b, s, LANES), F32),
        ],
        compiler_params=_cparams(("arbitrary", "arbitrary")),
        name="post_mix",
    )(x, mixo, modr, norm2_g3, modr, modr, w_router_p)


def _cumsum_lanes(x01):
    rows, n = x01.shape
    blk = MOE_TS
    ii = lax.broadcasted_iota(I32, (blk, blk), 0)
    jj = lax.broadcasted_iota(I32, (blk, blk), 1)
    tri = jnp.where(ii <= jj, 1.0, 0.0).astype(BF16)
    xb = x01.astype(BF16)
    carry = jnp.zeros((rows, 1), F32)
    outs, totals = [], [carry]
    for c in range(n // blk):
        part = jnp.dot(xb[:, c * blk:(c + 1) * blk], tri, preferred_element_type=F32) + carry
        outs.append(part)
        carry = part[:, blk - 1:blk]
        totals.append(carry)
    return jnp.concatenate(outs, axis=1), totals


def _route_kernel(aff_ref, idx_ref, slott_ref, lo_ref, cs_scr, *, cap):
    at = aff_ref[...].T[:N_EXPERTS, :]
    keys = lax.bitcast_convert_type(at, I32)

    def body(i, prefix):
        cand = prefix | jnp.left_shift(jnp.int32(1), 30 - i)
        cnt = jnp.sum(jnp.where(keys >= cand, 1.0, 0.0), axis=1, keepdims=True)
        return jnp.where(cnt >= cap, cand, prefix)

    thr = lax.fori_loop(0, 31, body, jnp.zeros((N_EXPERTS, 1), I32))
    gt = jnp.where(keys > thr, 1.0, 0.0)
    eq = jnp.where(keys == thr, 1.0, 0.0)
    need = cap - jnp.sum(gt, axis=1, keepdims=True)
    take = eq * jnp.where(_cumsum_lanes(eq)[0] <= need, 1.0, 0.0)
    sel = gt + take
    cs, totals = _cumsum_lanes(sel)
    slot = jnp.where(sel > 0.5, cs - 1.0, -1.0).astype(I32)
    s = slot.shape[1]
    pad = jnp.full((LANES - N_EXPERTS, s), -1, I32)
    slott_ref[...] = jnp.concatenate([slot, pad], axis=0).T

    lane_e = lax.broadcasted_iota(I32, (N_EXPERTS, LANES), 1)
    lo = jnp.zeros((N_EXPERTS, LANES), F32)
    for j, tot in enumerate(totals):
        lo = jnp.where(lane_e == j, tot, lo)
    lo_ref[...] = lo.astype(I32)

    cs_scr[...] = cs
    lane = lax.broadcasted_iota(I32, (cap, LANES), 1)
    rows = 128

    def one_expert(e, idxt):
        row = cs_scr[pl.ds(e, 1), :]
        cols = []
        for c0 in range(0, cap, rows):
            cio = (lax.broadcasted_iota(I32, (rows, s), 0) + c0).astype(F32)
            cols.append(jnp.sum(jnp.where(row <= cio, 1.0, 0.0), axis=1, keepdims=True))
        return jnp.where(lane == e, jnp.concatenate(cols, axis=0), idxt)

    idxt = lax.fori_loop(0, N_EXPERTS, one_expert, jnp.zeros((cap, LANES), F32))
    idx_ref[...] = idxt.T[:N_EXPERTS, :].astype(I32)


def _route(aff, cap):
    b, s, _ = aff.shape
    return pl.pallas_call(
        functools.partial(_route_kernel, cap=cap),
        grid=(b,),
        in_specs=[pl.BlockSpec((None, s, LANES), lambda bi: (bi, 0, 0))],
        out_specs=[
            pl.BlockSpec((None, N_EXPERTS, cap), lambda bi: (bi, 0, 0)),
            pl.BlockSpec((None, s, LANES), lambda bi: (bi, 0, 0)),
            pl.BlockSpec((None, N_EXPERTS, LANES), lambda bi: (bi, 0, 0)),
        ],
        out_shape=[
            jax.ShapeDtypeStruct((b, N_EXPERTS, cap), I32),
            jax.ShapeDtypeStruct((b, s, LANES), I32),
            jax.ShapeDtypeStruct((b, N_EXPERTS, LANES), I32),
        ],
        scratch_shapes=[pltpu.VMEM((N_EXPERTS, s), F32)],
        compiler_params=_cparams(("arbitrary",), 48),
        name="route",
    )(aff)


def _expert_kernel(idx_ref, h_ref, w1_ref, w3_ref, w2_ref, y_ref, rows_ref, xb_ref, acc_ref, gate_ref, sem):
    e = pl.program_id(0)
    f = pl.program_id(1)
    ne = pl.num_programs(0)
    nf = pl.num_programs(1)
    nb, cap, d = y_ref.shape
    n_rows = nb * cap
    per_step = n_rows // MOE_NF

    steps_per_batch = cap // per_step
    assert steps_per_batch & (steps_per_batch - 1) == 0 and steps_per_batch * per_step == cap

    def wait_all_rows():
        for ff in range(MOE_NF):
            pltpu.make_async_copy(h_ref.at[0, pl.ds(0, per_step), :], rows_ref.at[ff], sem).wait()

    @pl.when(jnp.logical_and(e == 0, f == 0))
    def _():
        for ff in range(MOE_NF):
            src = (ff // steps_per_batch) * N_EXPERTS * cap + (ff % steps_per_batch) * per_step

            def body(r, carry, ff=ff, src=src):
                pltpu.make_async_copy(h_ref.at[ff // steps_per_batch, pl.ds(idx_ref[src + r], 1), :],
                                      rows_ref.at[ff, pl.ds(r, 1), :], sem).start()
                return carry
            lax.fori_loop(0, per_step, body, 0)

    @pl.when(f == 0)
    def _():
        wait_all_rows()
        lane = lax.broadcasted_iota(I32, (per_step, LANES), 1)
        for ff in range(MOE_NF):
            rs = slice(ff * per_step, (ff + 1) * per_step)
            xb_ref[rs, :] = rows_ref[ff, :, :d].astype(BF16)
            gate_ref[rs, :] = jnp.sum(jnp.where(lane == e, rows_ref[ff, :, d:], 0.0), axis=1, keepdims=True)
        acc_ref[...] = jnp.zeros(acc_ref.shape, F32)

    e_next = jnp.minimum(e + 1, ne - 1)
    bi = lax.shift_right_logical(f, steps_per_batch.bit_length() - 1)
    c0 = (f & (steps_per_batch - 1)) * per_step
    src0 = (bi * N_EXPERTS + e_next) * cap + c0
    for r in range(per_step):
        pltpu.make_async_copy(h_ref.at[bi, pl.ds(idx_ref[src0 + r], 1), :],
                              rows_ref.at[f, pl.ds(r, 1), :], sem).start()

    xg = xb_ref[...]
    a = jnp.dot(xg, w1_ref[...].astype(BF16), preferred_element_type=F32)
    g = jnp.dot(xg, w3_ref[...].astype(BF16), preferred_element_type=F32)
    act = (_silu(a) * g).astype(BF16)
    acc_ref[...] += jnp.dot(act, w2_ref[...].astype(BF16), preferred_element_type=F32)

    @pl.when(f == nf - 1)
    def _():
        y_ref[...] = (acc_ref[...] * gate_ref[...]).reshape(nb, cap, d).astype(y_ref.dtype)

    @pl.when(jnp.logical_and(e == ne - 1, f == nf - 1))
    def _():
        wait_all_rows()


def _experts(idx, h2, w1, w3, w2, l, cap):
    b, s, dext = h2.shape
    d = dext - LANES
    e = w1.shape[1]
    ff = w1.shape[-1]
    fc = ff // MOE_NF
    grid_spec = pltpu.PrefetchScalarGridSpec(
        num_scalar_prefetch=1,
        grid=(e, MOE_NF),
        in_specs=[
            pl.BlockSpec(memory_space=pl.ANY),
            pl.BlockSpec((None, None, d, fc), lambda ei, f, ix: (l, ei, 0, f)),
            pl.BlockSpec((None, None, d, fc), lambda ei, f, ix: (l, ei, 0, f)),
            pl.BlockSpec((None, None, fc, d), lambda ei, f, ix: (l, ei, f, 0)),
        ],
        out_specs=pl.BlockSpec((b, None, cap, d), lambda ei, f, ix: (0, ei, 0, 0)),
        scratch_shapes=[pltpu.VMEM((MOE_NF, b * cap // MOE_NF, dext), F32), pltpu.VMEM((b * cap, d), BF16),
                        pltpu.VMEM((b * cap, d), F32), pltpu.VMEM((b * cap, 1), F32),
                        pltpu.SemaphoreType.DMA],
    )
    return pl.pallas_call(
        _expert_kernel,
        grid_spec=grid_spec,
        out_shape=jax.ShapeDtypeStruct((b, e, cap, d), BF16),
        compiler_params=_cparams(("arbitrary", "arbitrary"), 56),
        name="moe_experts",
    )(idx.reshape(-1), h2, w1, w3, w2)


def _scatter_kernel(lo_ref, slott_ref, y_ref, x1_ref, g2_ref, ng_ref, *rest, final_norm):
    if final_norm:
        o_ref, win_ref, full_ref, oh_ref, wsem, fsem = rest
    else:
        sc_ref, sh_ref, o_ref, h_ref, win_ref, full_ref, oh_ref, wsem, fsem = rest
    _scatter_body(lo_ref, slott_ref, y_ref, x1_ref, g2_ref, o_ref, win_ref, full_ref, oh_ref, wsem, fsem)
    xo = o_ref[...]
    xn = xo * lax.rsqrt(jnp.mean(xo * xo, axis=-1, keepdims=True) + EPS) * ng_ref[...]
    if final_norm:
        o_ref[...] = xn
    else:
        h_ref[...] = (xn * (1.0 + sc_ref[...]) + sh_ref[...]).astype(h_ref.dtype)


def _scatter_body(lo_ref, slott_ref, y_ref, x1_ref, g2_ref, o_ref, win_ref, full_ref, oh_ref, wsem, fsem):
    bi = pl.program_id(0)
    ti = pl.program_id(1)
    nb = pl.num_programs(0)
    nt = pl.num_programs(1)
    ts = slott_ref.shape[0]
    cap = y_ref.shape[2]
    step = bi * nt + ti
    buf = step & 1

    def lo_at(b_, t_, e):
        return lo_ref[(b_ * N_EXPERTS + e) * LANES + t_]

    def win_start(b_, t_, e):
        start = lax.shift_left(lax.shift_right_logical(lo_at(b_, t_, e), 4), 4)
        return pl.multiple_of(jnp.minimum(start, cap - MOE_WIN), 16)

    def win_copy(b_, t_, e, slot):
        return pltpu.make_async_copy(y_ref.at[b_, e, pl.ds(win_start(b_, t_, e), MOE_WIN), :],
                                     win_ref.at[slot, pl.ds(e * MOE_WIN, MOE_WIN), :], wsem.at[slot])

    @pl.when(step == 0)
    def _():
        for e in range(N_EXPERTS):
            win_copy(0, 0, e, 0).start()

    @pl.when(step + 1 < nb * nt)
    def _():
        wrap = ti + 1 == nt
        b_n = jnp.where(wrap, bi + 1, bi)
        t_n = jnp.where(wrap, 0, ti + 1)
        for e in range(N_EXPERTS):
            win_copy(b_n, t_n, e, 1 - buf).start()

    for e in range(N_EXPERTS):
        win_copy(bi, ti, e, buf).wait()

    lane = lax.broadcasted_iota(I32, (ts, LANES), 1)

    def slot_col(e):
        return jnp.sum(jnp.where(lane == e, slott_ref[...].astype(F32), 0.0), axis=1, keepdims=True).astype(I32)

    fits = None
    for e in range(N_EXPERTS):
        ok = lo_at(bi, ti + 1, e) - win_start(bi, ti, e) <= MOE_WIN
        fits = ok if fits is None else jnp.logical_and(fits, ok)

    @pl.when(fits)
    def _():
        widx = lax.broadcasted_iota(I32, (ts, MOE_WIN), 1)
        for e in range(N_EXPERTS):
            oh_ref[:, e * MOE_WIN:(e + 1) * MOE_WIN] = jnp.where(
                widx == slot_col(e) - win_start(bi, ti, e), 1.0, 0.0).astype(BF16)
        moe = jnp.dot(oh_ref[...], win_ref[buf], preferred_element_type=F32)
        o_ref[...] = x1_ref[...] + g2_ref[...] * moe

    @pl.when(jnp.logical_not(fits))
    def _():
        o_ref[...] = x1_ref[...]
        cidx = lax.broadcasted_iota(I32, (ts, cap), 1)

        def one_expert(e, carry):
            cp = pltpu.make_async_copy(y_ref.at[bi, e], full_ref, fsem)
            cp.start()
            cp.wait()
            onehot = jnp.where(cidx == slot_col(e), 1.0, 0.0).astype(BF16)
            o_ref[...] += g2_ref[...] * jnp.dot(onehot, full_ref[...], preferred_element_type=F32)
            return carry

        lax.fori_loop(0, N_EXPERTS, one_expert, 0)


def _scatter(lo, slott, y, x1, modr, l, next_g3, final_norm):
    b, s, d = x1.shape
    cap = y.shape[2]
    assert cap >= MOE_WIN and s // MOE_TS < LANES
    row_tile = pl.BlockSpec((None, MOE_TS, d), lambda bi, i, lo_: (bi, i, 0))
    ln = 0 if final_norm else l + 1
    in_specs = [
        pl.BlockSpec((None, MOE_TS, LANES), lambda bi, i, lo_: (bi, i, 0)),
        pl.BlockSpec(memory_space=pl.ANY),
        row_tile,
        pl.BlockSpec((None, None, None, 1, d), lambda bi, i, lo_: (l, bi, 5, 0, 0)),
        pl.BlockSpec((None, 1, d), lambda bi, i, lo_: (ln, 0, 0)),
    ]
    args = [lo.reshape(-1), slott, y, x1, modr, next_g3]
    out_specs = [row_tile]
    out_shape = [jax.ShapeDtypeStruct((b, s, d), F32)]
    if not final_norm:
        in_specs += [pl.BlockSpec((None, None, None, 1, d), lambda bi, i, lo_: (ln, bi, 1, 0, 0)),
                     pl.BlockSpec((None, None, None, 1, d), lambda bi, i, lo_: (ln, bi, 0, 0, 0))]
        args += [modr, modr]
        out_specs.append(row_tile)
        out_shape.append(jax.ShapeDtypeStruct((b, s, d), BF16))
    grid_spec = pltpu.PrefetchScalarGridSpec(
        num_scalar_prefetch=1,
        grid=(b, s // MOE_TS),
        in_specs=in_specs,
        out_specs=out_specs,
        scratch_shapes=[pltpu.VMEM((2, N_EXPERTS * MOE_WIN, d), BF16), pltpu.VMEM((cap, d), BF16),
                        pltpu.VMEM((MOE_TS, N_EXPERTS * MOE_WIN), BF16),
                        pltpu.SemaphoreType.DMA((2,)), pltpu.SemaphoreType.DMA],
    )
    outs = pl.pallas_call(
        functools.partial(_scatter_kernel, final_norm=final_norm),
        grid_spec=grid_spec,
        out_shape=out_shape,
        compiler_params=_cparams(("arbitrary", "arbitrary"), 58),
        name="moe_scatter",
    )(*args)
    return (outs[0], None) if final_norm else (outs[0], outs[1])


def kernel(x, c, positions, w_mod, b_mod, norm1_g, norm2_g, w_in, fnet_w, pool_w, pool_scale,
           lam_q1, lam_k1, lam_q2, lam_k2, sub_g, rel_bias, w_out, w_router, w1, w3, w2, final_g):
    b, s, d = x.shape
    depth = w_mod.shape[0]
    fnet_wd = fnet_w.shape[1] * fnet_w.shape[2]
    pool_wd = pool_w.shape[1] * pool_w.shape[2]
    ab_w = fnet_wd + pool_wd
    in_w = w_in.shape[-1]
    cap = EC_CAPACITY * s // N_EXPERTS

    c8 = jnp.zeros((8, d), F32).at[:b].set(c)
    mod = _modulation(c8, w_mod, b_mod.reshape(depth, 1, N_MOD * d))
    modr = mod[:, :b].reshape(depth, b, N_MOD, 1, d)

    norm1_g3 = norm1_g.reshape(depth, 1, d)
    norm2_g3 = norm2_g.reshape(depth, 1, d)
    pool_scale3 = pool_scale.reshape(depth, 1, pool_wd)
    sub_g3 = sub_g.reshape(depth, 1, DV)
    lam4 = [a.reshape(depth, 1, DK) for a in (lam_q1, lam_k1, lam_q2, lam_k2)]
    w_router_p = jnp.zeros((depth, d, LANES), F32).at[:, :, :N_EXPERTS].set(w_router)

    wd = _dft_matrix(s)
    ab = _fnet_weights(fnet_w, s)
    plan = _attn_plan(positions, rel_bias)
    nbt = _near_bias_table(rel_bias, positions, plan[1], plan[2])

    h1 = _norm_mod(x, norm1_g3, modr, 0, 0, 1)
    for l in range(depth):
        last = l == depth - 1
        h1 = h1.reshape(b * s, d)
        u_ab = _project([h1], w_in, l, 0, ab_w, F32, tn=1024, name="proj_in_ab").reshape(b, s, ab_w)
        qkv = _project([h1], w_in, l, ab_w, in_w - ab_w, BF16, tn=1024, lead_cols=N_HEADS * 2 * DK,
                       lead_scale=LOG2E * DK ** -0.5, name="proj_in_qkv").reshape(b, s, in_w - ab_w)

        za, zb = _fnet_z(u_ab, ab, l)
        ya = _dft_apply(wd, za, zb, b)
        yb = _pool_mixer(u_ab, pool_w, pool_scale3, l, fnet_wd // pool_w.shape[2])
        yc = _diff_attention(qkv, nbt, plan, rel_bias, positions, lam4, sub_g3, l)

        mixo = _project([ya.reshape(b * s, -1), yb.reshape(b * s, -1), yc.reshape(b * s, -1)],
                        w_out, l, 0, d, BF16, tn=1024, name="proj_out").reshape(b, s, d)
        x1, h2, aff = _post_mix(x, mixo, modr, norm2_g3, w_router_p, l)

        idx, slott, lo = _route(aff, cap)
        y = _experts(idx, h2, w1, w3, w2, l, cap)
        x, h1 = _scatter(lo, slott, y, x1, modr, l, final_g.reshape(1, 1, d) if last else norm1_g3,
                         final_norm=last)

    return x
```

```python
import functools
import math

import numpy as np
import jax
import jax.numpy as jnp
from jax import lax
from jax.experimental import pallas as pl
from jax.experimental.pallas import tpu as pltpu

F32 = jnp.float32
BF16 = jnp.bfloat16
I32 = jnp.int32

FNET_GROUPS = 4
POOL_GROUPS = 4
POOL_HALO = 64
N_HEADS = 8
DK = 64
DV = 128
N_BUCKETS = 32
MAX_DISTANCE = 128
N_EXPERTS = 16
EC_CAPACITY = 2
N_MOD = 6
EPS = 1e-6
LANES = 128
LOG2E = 1.4426950408889634
ATT_TQ = 256
ATT_TK = 512
ATT_NS = 2
MOE_NF = 4
MOE_TS = 512
MOE_WIN = 128


def _cparams(sem, vmem_mb=None):
    kw = dict(dimension_semantics=sem)
    if vmem_mb is not None:
        kw["vmem_limit_bytes"] = vmem_mb * 1024 * 1024
    return pltpu.CompilerParams(**kw)


def _silu(x):
    return x * jax.nn.sigmoid(x)


def _mod_kernel(c_ref, w_ref, b_ref, o_ref):
    ca = _silu(c_ref[...]).astype(BF16)
    o_ref[...] = jnp.dot(ca, w_ref[...].astype(BF16), preferred_element_type=F32) + b_ref[...]


def _modulation(c8, w_mod, b_mod3):
    depth, d, n = w_mod.shape
    tn = 1024
    return pl.pallas_call(
        _mod_kernel,
        grid=(depth, n // tn),
        in_specs=[
            pl.BlockSpec((8, d), lambda l, j: (0, 0)),
            pl.BlockSpec((None, d, tn), lambda l, j: (l, 0, j)),
            pl.BlockSpec((None, 1, tn), lambda l, j: (l, 0, j)),
        ],
        out_specs=pl.BlockSpec((None, 8, tn), lambda l, j: (l, 0, j)),
        out_shape=jax.ShapeDtypeStruct((depth, 8, n), F32),
        compiler_params=_cparams(("arbitrary", "arbitrary")),
        name="modulation",
    )(c8, w_mod, b_mod3)


def _norm_mod_kernel(x_ref, g_ref, sc_ref, sh_ref, o_ref):
    x = x_ref[...]
    ms = jnp.mean(x * x, axis=-1, keepdims=True)
    y = x * lax.rsqrt(ms + EPS) * g_ref[...]
    o_ref[...] = (y * (1.0 + sc_ref[...]) + sh_ref[...]).astype(o_ref.dtype)


def _norm_mod(x, g3, modr, l, sh_idx, sc_idx):
    b, s, d = x.shape
    tm = 512
    return pl.pallas_call(
        _norm_mod_kernel,
        grid=(b, s // tm),
        in_specs=[
            pl.BlockSpec((None, tm, d), lambda bi, i: (bi, i, 0)),
            pl.BlockSpec((None, 1, d), lambda bi, i: (l, 0, 0)),
            pl.BlockSpec((None, None, None, 1, d), lambda bi, i: (l, bi, sc_idx, 0, 0)),
            pl.BlockSpec((None, None, None, 1, d), lambda bi, i: (l, bi, sh_idx, 0, 0)),
        ],
        out_specs=pl.BlockSpec((None, tm, d), lambda bi, i: (bi, i, 0)),
        out_shape=jax.ShapeDtypeStruct((b, s, d), BF16),
        compiler_params=_cparams(("arbitrary", "arbitrary")),
        name="norm_mod",
    )(x, g3, modr, modr)


def _mm_kernel(*refs, k_sizes, lead_blocks, lead_scale):
    n_a = len(k_sizes)
    a_refs = refs[:n_a]
    w_ref, o_ref, wb_ref = refs[n_a], refs[n_a + 1], refs[n_a + 2]

    @pl.when(pl.program_id(1) == 0)
    def _():
        wb_ref[...] = w_ref[...].astype(BF16)

    acc = None
    off = 0
    for a_ref, ks in zip(a_refs, k_sizes):
        part = jnp.dot(a_ref[...], wb_ref[off:off + ks, :], preferred_element_type=F32)
        acc = part if acc is None else acc + part
        off += ks
    if lead_blocks:
        acc = acc * jnp.where(pl.program_id(0) < lead_blocks, lead_scale, 1.0)
    o_ref[...] = acc.astype(o_ref.dtype)


def _project(a_list, w, l, col0, ncols, out_dtype, tm=1024, tn=512, lead_cols=0, lead_scale=1.0,
             name="project"):
    m = a_list[0].shape[0]
    k_sizes = tuple(a.shape[1] for a in a_list)
    k = sum(k_sizes)
    assert w.shape[1] == k and col0 % tn == 0 and ncols % tn == 0 and m % tm == 0 and lead_cols % tn == 0
    cb0 = col0 // tn
    in_specs = [pl.BlockSpec((tm, ks), lambda j, i: (i, 0)) for ks in k_sizes]
    in_specs.append(pl.BlockSpec((None, k, tn), lambda j, i: (l, 0, cb0 + j)))
    return pl.pallas_call(
        functools.partial(_mm_kernel, k_sizes=k_sizes, lead_blocks=lead_cols // tn, lead_scale=lead_scale),
        grid=(ncols // tn, m // tm),
        in_specs=in_specs,
        out_specs=pl.BlockSpec((tm, tn), lambda j, i: (i, j)),
        out_shape=jax.ShapeDtypeStruct((m, ncols), out_dtype),
        scratch_shapes=[pltpu.VMEM((k, tn), BF16)],
        compiler_params=_cparams(("arbitrary", "arbitrary"), 48),
        name=name,
    )(*a_list, w)


def _dft_tables(s):
    sp = np.arange(s, dtype=np.int64)
    a = np.arange(64, dtype=np.int64)[:, None]
    ang1 = 2.0 * np.pi * ((a * sp[None, :]) % 64) / 64.0
    ang2 = 2.0 * np.pi * ((a * sp[None, :]) % s) / float(s)
    t1c = np.cos(ang1).astype(np.float32).reshape(64, 1, s)
    t1s = np.sin(ang1).astype(np.float32).reshape(64, 1, s)
    t2c = np.cos(ang2).astype(np.float32)
    t2s = np.sin(ang2).astype(np.float32)
    return t1c, t1s, t2c, t2s


def _dftgen_kernel(t1c_ref, t1s_ref, t2c_ref, t2s_ref, o_ref):
    s = t2c_ref.shape[1]
    c1, s1 = t1c_ref[...], t1s_ref[...]
    c2, s2 = t2c_ref[...], t2s_ref[...]
    o_ref[:, :s] = (c1 * c2 - s1 * s2).astype(BF16)
    o_ref[:, s:] = (-(s1 * c2 + c1 * s2)).astype(BF16)


def _dft_matrix(s):
    assert s % 64 == 0 and s // 64 == 64
    t1c, t1s, t2c, t2s = _dft_tables(s)
    return pl.pallas_call(
        _dftgen_kernel,
        grid=(64,),
        in_specs=[
            pl.BlockSpec((None, 1, s), lambda a: (a, 0, 0)),
            pl.BlockSpec((None, 1, s), lambda a: (a, 0, 0)),
            pl.BlockSpec((64, s), lambda a: (0, 0)),
            pl.BlockSpec((64, s), lambda a: (0, 0)),
        ],
        out_specs=pl.BlockSpec((64, 2 * s), lambda a: (a, 0)),
        out_shape=jax.ShapeDtypeStruct((s, 2 * s), BF16),
        compiler_params=_cparams(("arbitrary",)),
        name="dft_matrix",
    )(t1c, t1s, t2c, t2s)


def _fnet_w_kernel(cc_ref, sc_ref, w_ref, o_ref, *, norm):
    depth, groups, cg, _ = w_ref.shape
    cc, sc = cc_ref[...], sc_ref[...]
    for l in range(depth):
        for g in range(groups):
            w = w_ref[l, g]
            a = jnp.dot(cc, w, preferred_element_type=F32, precision=lax.Precision.HIGHEST)
            b = jnp.dot(sc, w, preferred_element_type=F32, precision=lax.Precision.HIGHEST)
            o_ref[l, g, :, :cg] = (a * norm).astype(BF16)
            o_ref[l, g, :, cg:] = (b * norm).astype(BF16)


def _fnet_weights(fnet_w, s):
    depth, groups, cg, _ = fnet_w.shape
    idx = np.arange(cg, dtype=np.int64)
    ang = 2.0 * np.pi * ((idx[:, None] * idx[None, :]) % cg) / float(cg)
    cc = np.cos(ang).astype(np.float32)
    sc = np.sin(ang).astype(np.float32)
    norm = 1.0 / math.sqrt(float(s) * float(cg))
    return pl.pallas_call(
        functools.partial(_fnet_w_kernel, norm=norm),
        out_shape=jax.ShapeDtypeStruct((depth, groups, cg, 2 * cg), BF16),
        name="fnet_weights",
    )(cc, sc, fnet_w)


def _proj_ab_kernel(a_ref, w_ref, ab_ref, ub_ref, za_ref, zb_ref, wb_ref):
    @pl.when(pl.program_id(0) == 0)
    def _():
        wb_ref[...] = w_ref[...].astype(BF16)

    u = jnp.dot(a_ref[...], wb_ref[...], preferred_element_type=F32)
    groups, cg = ab_ref.shape[0], ab_ref.shape[1]
    fw = groups * cg
    ub_ref[...] = u[:, fw:]
    for g in range(groups):
        z = jnp.dot(u[:, g * cg:(g + 1) * cg].astype(BF16), ab_ref[g], preferred_element_type=F32)
        za_ref[:, g * cg:(g + 1) * cg] = z[:, :cg].astype(BF16)
        zb_ref[:, g * cg:(g + 1) * cg] = z[:, cg:].astype(BF16)


def _project_ab(h1, w_in, ab, l, b, s, ab_w):
    m, k = h1.shape
    groups, cg = ab.shape[1], ab.shape[2]
    fw = groups * cg
    tm = 1024
    per_b = s // tm
    z_spec = pl.BlockSpec((tm, fw), lambda i: (i % per_b, i // per_b))
    z_shape = jax.ShapeDtypeStruct((s, b * fw), BF16)
    ub, za, zb = pl.pallas_call(
        _proj_ab_kernel,
        grid=(m // tm,),
        in_specs=[
            pl.BlockSpec((tm, k), lambda i: (i, 0)),
            pl.BlockSpec((None, k, ab_w), lambda i: (l, 0, 0)),
            pl.BlockSpec((None, groups, cg, 2 * cg), lambda i: (l, 0, 0, 0)),
        ],
        out_specs=[pl.BlockSpec((tm, ab_w - fw), lambda i: (i, 0)), z_spec, z_spec],
        out_shape=[jax.ShapeDtypeStruct((m, ab_w - fw), F32), z_shape, z_shape],
        scratch_shapes=[pltpu.VMEM((k, ab_w), BF16)],
        compiler_params=_cparams(("arbitrary",), 48),
        name="proj_in_ab",
    )(h1, w_in, ab)
    return ub.reshape(b, s, ab_w - fw), za, zb


def _dft_apply_kernel(wd_ref, za_ref, zb_ref, o_ref):
    s = za_ref.shape[0]
    acc = jnp.dot(wd_ref[:, :s], za_ref[...], preferred_element_type=F32)
    acc = acc + jnp.dot(wd_ref[:, s:], zb_ref[...], preferred_element_type=F32)
    o_ref[...] = acc.astype(o_ref.dtype)


def _dft_apply(wd, za, zb, b):
    s = wd.shape[0]
    fw = za.shape[1] // b
    tm = 512
    return pl.pallas_call(
        _dft_apply_kernel,
        grid=(b, s // tm),
        in_specs=[
            pl.BlockSpec((tm, 2 * s), lambda bi, i: (i, 0)),
            pl.BlockSpec((s, fw), lambda bi, i: (0, bi)),
            pl.BlockSpec((s, fw), lambda bi, i: (0, bi)),
        ],
        out_specs=pl.BlockSpec((None, tm, fw), lambda bi, i: (bi, i, 0)),
        out_shape=jax.ShapeDtypeStruct((b, s, fw), BF16),
        compiler_params=_cparams(("arbitrary", "arbitrary"), 48),
        name="dft_apply",
    )(wd, za, zb)


def _pool_kernel(u_ref, w_ref, sc_ref, o_ref, pad_ref):
    s, cg = u_ref.shape
    t = 256
    half = jnp.left_shift(jnp.int32(1), pl.program_id(1))
    pad_ref[0:POOL_HALO, :] = jnp.zeros((POOL_HALO, cg), F32)
    pad_ref[s + POOL_HALO:s + 2 * POOL_HALO, :] = jnp.zeros((POOL_HALO, cg), F32)
    pad_ref[POOL_HALO:s + POOL_HALO, :] = u_ref[...]
    ii = lax.broadcasted_iota(I32, (t, t + 2 * POOL_HALO), 0)
    jj = lax.broadcasted_iota(I32, (t, t + 2 * POOL_HALO), 1)
    dlt = jj - ii - POOL_HALO
    band = jnp.where(dlt >= -half, jnp.where(dlt <= half - 1, 1.0, 0.0), 0.0).astype(BF16)
    wb = w_ref[...].astype(BF16)
    scale = sc_ref[...]

    def body(ti, carry):
        r0 = pl.multiple_of(ti * t, t)
        seg = pad_ref[pl.ds(r0, t + 2 * POOL_HALO), :]
        win = jnp.dot(band, seg.astype(BF16), preferred_element_type=F32)
        gi = r0 + lax.broadcasted_iota(I32, (t, cg), 0)
        lo_i = jnp.maximum(gi - half, 0)
        hi_i = jnp.minimum(gi + half - 1, s - 1)
        cnt = (hi_i - lo_i + 1).astype(F32)
        dmean = win / cnt - seg[POOL_HALO:POOL_HALO + t, :]
        y = jnp.dot(dmean.astype(BF16), wb, preferred_element_type=F32) * scale
        o_ref[pl.ds(r0, t), :] = y.astype(o_ref.dtype)
        return carry

    lax.fori_loop(0, s // t, body, 0)


def _pool_mixer(u_ab, pool_w, pool_scale3, l, col_block0):
    b, s, _ = u_ab.shape
    groups, cg = pool_w.shape[1], pool_w.shape[2]
    return pl.pallas_call(
        _pool_kernel,
        grid=(b, groups),
        in_specs=[
            pl.BlockSpec((None, s, cg), lambda bi, g: (bi, 0, col_block0 + g)),
            pl.BlockSpec((None, None, cg, cg), lambda bi, g: (l, g, 0, 0)),
            pl.BlockSpec((None, 1, cg), lambda bi, g: (l, 0, g)),
        ],
        out_specs=pl.BlockSpec((None, s, cg), lambda bi, g: (bi, 0, g)),
        out_shape=jax.ShapeDtypeStruct((b, s, groups * cg), BF16),
        scratch_shapes=[pltpu.VMEM((s + 2 * POOL_HALO, cg), F32)],
        compiler_params=_cparams(("arbitrary", "arbitrary")),
        name="pool_mixer",
    )(u_ab, pool_w, pool_scale3)


def _bucket(rel):
    nb = N_BUCKETS // 2
    max_exact = nb // 2
    n = jnp.abs(rel)
    nf = jnp.maximum(n, 1).astype(F32)
    large = max_exact + (jnp.log(nf / max_exact) / math.log(MAX_DISTANCE / max_exact)
                         * (nb - max_exact)).astype(I32)
    large = jnp.minimum(large, nb - 1)
    return jnp.where(rel > 0, nb, 0) + jnp.where(n < max_exact, n, large)


def _bias_tile(tab_ref, h, rel):
    bucket = _bucket(rel)
    val = jnp.full(rel.shape, tab_ref[h], F32)
    for j in range(1, N_BUCKETS):
        val = jnp.where(bucket == j, tab_ref[j * N_HEADS + h], val)
    return val * LOG2E


def _near_bias_kernel(slotj_ref, nnear_ref, tabt_ref, pq_ref, pk_ref, o_ref):
    del slotj_ref
    used = jnp.where(pl.program_id(1) < nnear_ref[pl.program_id(0)], 1.0, 0.0)
    bucket = _bucket(pk_ref[...] - pq_ref[...])
    tq, tk = bucket.shape
    for h in range(N_HEADS):
        row = jnp.broadcast_to(tabt_ref[h:h + 1, :], (tq, LANES))
        cols = [jnp.take_along_axis(row, bucket[:, c * LANES:(c + 1) * LANES], axis=1)
                for c in range(tk // LANES)]
        o_ref[h] = jnp.concatenate(cols, axis=1) * used


def _near_bias_table(rel_bias, positions, slotj, nnear):
    s = positions.shape[0]
    nq = s // ATT_TQ
    tabt = jnp.zeros((N_HEADS, LANES), F32).at[:, :N_BUCKETS].set(rel_bias.T * LOG2E)
    grid_spec = pltpu.PrefetchScalarGridSpec(
        num_scalar_prefetch=2,
        grid=(nq, ATT_NS),
        in_specs=[
            pl.BlockSpec((N_HEADS, LANES), lambda i, n, sj, nn: (0, 0)),
            pl.BlockSpec((ATT_TQ, 1), lambda i, n, sj, nn: (i, 0)),
            pl.BlockSpec((1, ATT_TK), lambda i, n, sj, nn: (0, sj[i * ATT_NS + n])),
        ],
        out_specs=pl.BlockSpec((N_HEADS, None, None, ATT_TQ, ATT_TK), lambda i, n, sj, nn: (0, i, n, 0, 0)),
    )
    return pl.pallas_call(
        _near_bias_kernel,
        grid_spec=grid_spec,
        out_shape=jax.ShapeDtypeStruct((N_HEADS, nq, ATT_NS, ATT_TQ, ATT_TK), F32),
        compiler_params=_cparams(("arbitrary", "arbitrary")),
        name="near_bias_table",
    )(slotj, nnear, tabt, positions.reshape(s, 1), positions.reshape(1, s))


def _attn_plan(positions, rel_bias):
    s = positions.shape[0]
    nq, nk = s // ATT_TQ, s // ATT_TK
    pq = positions.reshape(nq, ATT_TQ)
    pk = positions.reshape(nk, ATT_TK)
    rel_min = pk.min(axis=1)[None, :] - pq.max(axis=1)[:, None]
    rel_max = pk.max(axis=1)[None, :] - pq.min(axis=1)[:, None]
    cls = jnp.where(rel_min >= MAX_DISTANCE, 1, jnp.where(rel_max <= -MAX_DISTANCE, 0, 2)).astype(I32)
    near = cls == 2
    nnear = near.sum(axis=1).astype(I32)
    slotj = jnp.argsort(jnp.logical_not(near), axis=1, stable=True)[:, :ATT_NS].astype(I32)
    fits = jnp.all(nnear <= ATT_NS)
    nb = N_BUCKETS // 2
    ctab = jnp.stack([rel_bias[nb - 1], rel_bias[2 * nb - 1], jnp.zeros((N_HEADS,), F32)], axis=1) * LOG2E
    return cls.reshape(-1), slotj.reshape(-1), nnear, ctab.reshape(-1).astype(F32), fits


def _lambda(lq1_ref, lk1_ref, lq2_ref, lk2_ref, lam_init):
    return (jnp.exp(jnp.sum(lq1_ref[...] * lk1_ref[...], axis=-1, keepdims=True))
            - jnp.exp(jnp.sum(lq2_ref[...] * lk2_ref[...], axis=-1, keepdims=True)) + lam_init)


def _stack_maps(q):
    lane = lax.broadcasted_iota(I32, q.shape, 1)
    zero = jnp.zeros_like(q)
    return jnp.concatenate([jnp.where(lane < DK, q, zero), jnp.where(lane >= DK, q, zero)], axis=0)


def _attn_finish(o1, o2, lam, sg, lam_init, dtype):
    o = o1 - lam * o2
    ms = jnp.mean(o * o, axis=-1, keepdims=True)
    y = o * lax.rsqrt(ms + EPS) * sg
    return (y * (1.0 - lam_init)).astype(dtype)


def _attn_kernel(cls_ref, slotj_ref, ctab_ref, lq1_ref, lk1_ref, lq2_ref, lk2_ref,
                 q_ref, k_ref, v_ref, qn_ref, kn_ref, nbp0_ref, nbp1_ref, nba0_ref, nba1_ref, nbb0_ref, nbb1_ref,
                 sg_ref, o_ref, sa_scr, sb_scr, ma_scr, mb_scr, vaug_scr, *, lam_init):
    bi = pl.program_id(0)
    h = pl.program_id(1)
    ip = pl.program_id(2)
    n_b = pl.num_programs(0)
    n_ip = pl.num_programs(2)
    tq = ATT_TQ
    s = k_ref.shape[0]
    nk = s // ATT_TK
    lam = _lambda(lq1_ref, lk1_ref, lq2_ref, lk2_ref, lam_init)
    sg = sg_ref[...]

    def lane_tile_max(blk):
        out = blk[:, :LANES]
        for c in range(1, blk.shape[1] // LANES):
            out = jnp.maximum(out, blk[:, c * LANES:(c + 1) * LANES])
        return out

    def scores(t, hh, qr, kr, s_scr, m_scr, nb_refs):
        qs = _stack_maps(qr[pl.ds(pl.multiple_of(t * tq, tq), tq), :])
        mrow = None
        for j in range(nk):
            c = cls_ref[t * nk + j]
            sc = lax.dot_general(qs, kr[j * ATT_TK:(j + 1) * ATT_TK, :], (((1,), (1,)), ((), ())),
                                 preferred_element_type=F32) + ctab_ref[hh * 3 + c]
            s_scr[j] = sc
            cm = lane_tile_max(sc) + jnp.where(c == 2, -1e30, 0.0)
            mrow = cm if mrow is None else jnp.maximum(mrow, cm)
        for n, nb_ref in enumerate(nb_refs):
            j = slotj_ref[t * ATT_NS + n]
            bias = nb_ref[...]
            top = s_scr[j, 0:tq, :] + bias
            bot = s_scr[j, tq:2 * tq, :] + bias
            s_scr[j, 0:tq, :] = top
            s_scr[j, tq:2 * tq, :] = bot
            mrow = jnp.maximum(mrow, jnp.concatenate([lane_tile_max(top), lane_tile_max(bot)], axis=0))
        m_scr[...] = mrow

    def outputs(s_scr, m_scr, row0):
        m = jnp.max(m_scr[...], axis=-1, keepdims=True)
        acc = None
        for j in range(nk):
            e = jnp.exp2(s_scr[j] - m).astype(BF16)
            part = jnp.dot(e, vaug_scr[j * ATT_TK:(j + 1) * ATT_TK, :], preferred_element_type=F32)
            acc = part if acc is None else acc + part
        o1 = acc[:tq, :DV] / acc[:tq, DV:DV + 1]
        o2 = acc[tq:, :DV] / acc[tq:, DV:DV + 1]
        o_ref[row0:row0 + tq, :] = _attn_finish(o1, o2, lam, sg, lam_init, o_ref.dtype)

    @pl.when(ip == 0)
    def _():
        vaug_scr[:, :DV] = v_ref[...]
        lane = lax.broadcasted_iota(I32, (s, DV), 1)
        vaug_scr[:, DV:] = jnp.where(lane == 0, 1.0, 0.0).astype(BF16)

    @pl.when(jnp.logical_and(jnp.logical_and(bi == 0, h == 0), ip == 0))
    def _():
        scores(0, h, q_ref, k_ref, sa_scr, ma_scr, (nbp0_ref, nbp1_ref))

    last = ip == n_ip - 1
    _, h_next = _next_head(bi, h, last, n_b)
    t_next = jnp.where(last, 0, 2 * ip + 2)
    scores(2 * ip + 1, h, q_ref, k_ref, sb_scr, mb_scr, (nbb0_ref, nbb1_ref))
    outputs(sa_scr, ma_scr, 0)
    scores(t_next, h_next, qn_ref, kn_ref, sa_scr, ma_scr, (nba0_ref, nba1_ref))
    outputs(sb_scr, mb_scr, tq)


def _next_head(bi, h, last, n_b):
    lin = jnp.minimum(bi * N_HEADS + h + jnp.where(last, 1, 0), n_b * N_HEADS - 1)
    return lin // N_HEADS, lin % N_HEADS


def _attn_fast(qkv, nbt, plan, lam4, sub_g3, l):
    b, s, _ = qkv.shape
    cls, slotj, _, ctab, _ = plan
    nq = s // ATT_TQ
    n_ip = nq // 2
    assert nq % 2 == 0 and ATT_NS == 2
    lam_init = 0.8 - 0.6 * math.exp(-0.3 * l)
    smem = pl.BlockSpec(memory_space=pltpu.SMEM)
    lam_specs = [pl.BlockSpec((None, 1, DK), lambda bi, h, ip: (l, 0, 0)) for _ in range(4)]
    nb_tile = (None, None, None, ATT_TQ, ATT_TK)

    def nxt(bi, h, ip):
        return _next_head(bi, h, ip == n_ip - 1, b)

    def nba_map(n):
        def index_map(bi, h, ip):
            return (nxt(bi, h, ip)[1], jnp.where(ip == n_ip - 1, 0, 2 * ip + 2), n, 0, 0)
        return index_map

    nb_specs = ([pl.BlockSpec(nb_tile, lambda bi, h, ip, n=n: (0, 0, n, 0, 0)) for n in range(ATT_NS)]
                + [pl.BlockSpec(nb_tile, nba_map(n)) for n in range(ATT_NS)]
                + [pl.BlockSpec(nb_tile, lambda bi, h, ip, n=n: (h, 2 * ip + 1, n, 0, 0)) for n in range(ATT_NS)])
    return pl.pallas_call(
        functools.partial(_attn_kernel, lam_init=lam_init),
        grid=(b, N_HEADS, n_ip),
        in_specs=[smem, smem, smem] + lam_specs + [
            pl.BlockSpec((None, s, 2 * DK), lambda bi, h, ip: (bi, 0, h)),
            pl.BlockSpec((None, s, 2 * DK), lambda bi, h, ip: (bi, 0, N_HEADS + h)),
            pl.BlockSpec((None, s, DV), lambda bi, h, ip: (bi, 0, 2 * N_HEADS + h)),
            pl.BlockSpec((None, s, 2 * DK), lambda bi, h, ip: (nxt(bi, h, ip)[0], 0, nxt(bi, h, ip)[1])),
            pl.BlockSpec((None, s, 2 * DK), lambda bi, h, ip: (nxt(bi, h, ip)[0], 0, N_HEADS + nxt(bi, h, ip)[1])),
        ] + nb_specs + [pl.BlockSpec((None, 1, DV), lambda bi, h, ip: (l, 0, 0))],
        out_specs=pl.BlockSpec((None, 2 * ATT_TQ, DV), lambda bi, h, ip: (bi, ip, h)),
        out_shape=jax.ShapeDtypeStruct((b, s, N_HEADS * DV), BF16),
        scratch_shapes=[pltpu.VMEM((s // ATT_TK, 2 * ATT_TQ, ATT_TK), F32),
                        pltpu.VMEM((s // ATT_TK, 2 * ATT_TQ, ATT_TK), F32),
                        pltpu.VMEM((2 * ATT_TQ, LANES), F32),
                        pltpu.VMEM((2 * ATT_TQ, LANES), F32),
                        pltpu.VMEM((s, 2 * DV), BF16)],
        compiler_params=_cparams(("arbitrary", "arbitrary", "arbitrary"), 56),
        name="diff_attention",
    )(cls, slotj, ctab, *lam4, qkv, qkv, qkv, qkv, qkv, nbt, nbt, nbt, nbt, nbt, nbt, sub_g3)


def _attn_any_kernel(tab_ref, lq1_ref, lk1_ref, lq2_ref, lk2_ref, q_ref, k_ref, v_ref, pq_ref, pk_ref,
                     sg_ref, o_ref, *, lam_init):
    h = pl.program_id(1)
    tq = q_ref.shape[0]
    bias = _bias_tile(tab_ref, h, pk_ref[...] - pq_ref[...])
    sc = lax.dot_general(_stack_maps(q_ref[...]), k_ref[...], (((1,), (1,)), ((), ())),
                         preferred_element_type=F32)
    v = v_ref[...]

    def one_map(sm):
        sm = sm + bias
        e = jnp.exp2(sm - jnp.max(sm, axis=-1, keepdims=True))
        den = jnp.sum(e, axis=-1, keepdims=True)
        return jnp.dot(e.astype(BF16), v, preferred_element_type=F32) / den

    lam = _lambda(lq1_ref, lk1_ref, lq2_ref, lk2_ref, lam_init)
    o_ref[...] = _attn_finish(one_map(sc[:tq]), one_map(sc[tq:]), lam, sg_ref[...], lam_init, o_ref.dtype)


def _attn_any(qkv, rel_bias, positions, lam4, sub_g3, l):
    b, s, _ = qkv.shape
    tq = 128
    lam_init = 0.8 - 0.6 * math.exp(-0.3 * l)
    lam_specs = [pl.BlockSpec((None, 1, DK), lambda bi, h, i: (l, 0, 0)) for _ in range(4)]
    return pl.pallas_call(
        functools.partial(_attn_any_kernel, lam_init=lam_init),
        grid=(b, N_HEADS, s // tq),
        in_specs=[pl.BlockSpec(memory_space=pltpu.SMEM)] + lam_specs + [
            pl.BlockSpec((None, tq, 2 * DK), lambda bi, h, i: (bi, i, h)),
            pl.BlockSpec((None, s, 2 * DK), lambda bi, h, i: (bi, 0, N_HEADS + h)),
            pl.BlockSpec((None, s, DV), lambda bi, h, i: (bi, 0, 2 * N_HEADS + h)),
            pl.BlockSpec((tq, 1), lambda bi, h, i: (i, 0)),
            pl.BlockSpec((1, s), lambda bi, h, i: (0, 0)),
            pl.BlockSpec((None, 1, DV), lambda bi, h, i: (l, 0, 0)),
        ],
        out_specs=pl.BlockSpec((None, tq, DV), lambda bi, h, i: (bi, i, h)),
        out_shape=jax.ShapeDtypeStruct((b, s, N_HEADS * DV), BF16),
        compiler_params=_cparams(("arbitrary", "arbitrary", "arbitrary"), 48),
        name="diff_attention_any",
    )(rel_bias.reshape(-1), *lam4, qkv, qkv, qkv, positions.reshape(s, 1), positions.reshape(1, s), sub_g3)


def _diff_attention(qkv, nbt, plan, rel_bias, positions, lam4, sub_g3, l):
    return lax.cond(plan[4],
                    lambda: _attn_fast(qkv, nbt, plan, lam4, sub_g3, l),
                    lambda: _attn_any(qkv, rel_bias, positions, lam4, sub_g3, l))


def _post_mix_kernel(x_ref, mo_ref, g1_ref, ng_ref, sc_ref, sh_ref, wr_ref, x1_ref, h_ref, lg_ref):
    x1 = x_ref[...] + g1_ref[...] * mo_ref[...].astype(F32)
    x1_ref[...] = x1
    ms = jnp.mean(x1 * x1, axis=-1, keepdims=True)
    h = x1 * lax.rsqrt(ms + EPS) * ng_ref[...]
    h = h * (1.0 + sc_ref[...]) + sh_ref[...]
    d = h.shape[1]
    wr = wr_ref[...]
    h_hi = h.astype(BF16)
    h_lo = (h - h_hi.astype(F32)).astype(BF16)
    w_hi = wr.astype(BF16)
    w_lo = (wr - w_hi.astype(F32)).astype(BF16)
    lg = (jnp.dot(h_hi, w_hi, preferred_element_type=F32) + jnp.dot(h_lo, w_hi, preferred_element_type=F32)
          + jnp.dot(h_hi, w_lo, preferred_element_type=F32))
    lane = lax.broadcasted_iota(I32, lg.shape, 1)
    valid = lane < N_EXPERTS
    lgm = jnp.where(valid, lg, -1e30)
    ex = jnp.where(valid, jnp.exp(lgm - jnp.max(lgm, axis=-1, keepdims=True)), 0.0)
    aff = ex / jnp.sum(ex, axis=-1, keepdims=True)
    h_ref[:, :d] = h
    h_ref[:, d:] = aff
    lg_ref[...] = aff


def _post_mix(x, mixo, modr, norm2_g3, w_router_p, l):
    b, s, d = x.shape
    tm = 256
    mspec = lambda idx: pl.BlockSpec((None, None, None, 1, d), lambda bi, i: (l, bi, idx, 0, 0))
    return pl.pallas_call(
        _post_mix_kernel,
        grid=(b, s // tm),
        in_specs=[
            pl.BlockSpec((None, tm, d), lambda bi, i: (bi, i, 0)),
            pl.BlockSpec((None, tm, d), lambda bi, i: (bi, i, 0)),
            mspec(2),
            pl.BlockSpec((None, 1, d), lambda bi, i: (l, 0, 0)),
            mspec(4),
            mspec(3),
            pl.BlockSpec((None, d, LANES), lambda bi, i: (l, 0, 0)),
        ],
        out_specs=[
            pl.BlockSpec((None, tm, d), lambda bi, i: (bi, i, 0)),
            pl.BlockSpec((None, tm, d + LANES), lambda bi, i: (bi, i, 0)),
            pl.BlockSpec((None, tm, LANES), lambda bi, i: (bi, i, 0)),
        ],
        out_shape=[
            jax.ShapeDtypeStruct((b, s, d), F32),
            jax.ShapeDtypeStruct((b, s, d + LANES), F32),
            jax.ShapeDtypeStruct((b, s, LANES), F32),
        ],
        compiler_params=_cparams(("arbitrary", "arbitrary")),
        name="post_mix",
    )(x, mixo, modr, norm2_g3, modr, modr, w_router_p)


def _cumsum_lanes(x01):
    rows, n = x01.shape
    blk = MOE_TS
    ii = lax.broadcasted_iota(I32, (blk, blk), 0)
    jj = lax.broadcasted_iota(I32, (blk, blk), 1)
    tri = jnp.where(ii <= jj, 1.0, 0.0).astype(BF16)
    xb = x01.astype(BF16)
    carry = jnp.zeros((rows, 1), F32)
    outs, totals = [], [carry]
    for c in range(n // blk):
        part = jnp.dot(xb[:, c * blk:(c + 1) * blk], tri, preferred_element_type=F32) + carry
        outs.append(part)
        carry = part[:, blk - 1:blk]
        totals.append(carry)
    return jnp.concatenate(outs, axis=1), totals


def _route_kernel(aff_ref, idx_ref, slott_ref, lo_ref, cs_scr, *, cap):
    at = aff_ref[...].T[:N_EXPERTS, :]
    keys = lax.bitcast_convert_type(at, I32)

    def body(i, prefix):
        cand = prefix | jnp.left_shift(jnp.int32(1), 30 - i)
        cnt = jnp.sum(jnp.where(keys >= cand, 1.0, 0.0), axis=1, keepdims=True)
        return jnp.where(cnt >= cap, cand, prefix)

    thr = lax.fori_loop(0, 31, body, jnp.zeros((N_EXPERTS, 1), I32))
    gt = jnp.where(keys > thr, 1.0, 0.0)
    eq = jnp.where(keys == thr, 1.0, 0.0)
    need = cap - jnp.sum(gt, axis=1, keepdims=True)
    take = eq * jnp.where(_cumsum_lanes(eq)[0] <= need, 1.0, 0.0)
    sel = gt + take
    cs, totals = _cumsum_lanes(sel)
    slot = jnp.where(sel > 0.5, cs - 1.0, -1.0).astype(I32)
    s = slot.shape[1]
    pad = jnp.full((LANES - N_EXPERTS, s), -1, I32)
    slott_ref[...] = jnp.concatenate([slot, pad], axis=0).T

    lane_e = lax.broadcasted_iota(I32, (N_EXPERTS, LANES), 1)
    lo = jnp.zeros((N_EXPERTS, LANES), F32)
    for j, tot in enumerate(totals):
        lo = jnp.where(lane_e == j, tot, lo)
    lo_ref[...] = lo.astype(I32)

    cs_scr[...] = cs
    lane = lax.broadcasted_iota(I32, (cap, LANES), 1)
    rows = 128

    def one_expert(e, idxt):
        row = cs_scr[pl.ds(e, 1), :]
        cols = []
        for c0 in range(0, cap, rows):
            cio = (lax.broadcasted_iota(I32, (rows, s), 0) + c0).astype(F32)
            cols.append(jnp.sum(jnp.where(row <= cio, 1.0, 0.0), axis=1, keepdims=True))
        return jnp.where(lane == e, jnp.concatenate(cols, axis=0), idxt)

    idxt = lax.fori_loop(0, N_EXPERTS, one_expert, jnp.zeros((cap, LANES), F32))
    idx_ref[...] = idxt.T[:N_EXPERTS, :].astype(I32)


def _route(aff, cap):
    b, s, _ = aff.shape
    return pl.pallas_call(
        functools.partial(_route_kernel, cap=cap),
        grid=(b,),
        in_specs=[pl.BlockSpec((None, s, LANES), lambda bi: (bi, 0, 0))],
        out_specs=[
            pl.BlockSpec((None, N_EXPERTS, cap), lambda bi: (bi, 0, 0)),
            pl.BlockSpec((None, s, LANES), lambda bi: (bi, 0, 0)),
            pl.BlockSpec((None, N_EXPERTS, LANES), lambda bi: (bi, 0, 0)),
        ],
        out_shape=[
            jax.ShapeDtypeStruct((b, N_EXPERTS, cap), I32),
            jax.ShapeDtypeStruct((b, s, LANES), I32),
            jax.ShapeDtypeStruct((b, N_EXPERTS, LANES), I32),
        ],
        scratch_shapes=[pltpu.VMEM((N_EXPERTS, s), F32)],
        compiler_params=_cparams(("arbitrary",), 48),
        name="route",
    )(aff)


def _expert_kernel(idx_ref, h_ref, w1_ref, w3_ref, w2_ref, y_ref, rows_ref, xb_ref, acc_ref, gate_ref, sem):
    e = pl.program_id(0)
    f = pl.program_id(1)
    ne = pl.num_programs(0)
    nf = pl.num_programs(1)
    nb, cap, d = y_ref.shape
    n_rows = nb * cap
    per_step = n_rows // MOE_NF

    steps_per_batch = cap // per_step
    assert steps_per_batch & (steps_per_batch - 1) == 0 and steps_per_batch * per_step == cap

    def wait_all_rows():
        for ff in range(MOE_NF):
            pltpu.make_async_copy(h_ref.at[0, pl.ds(0, per_step), :], rows_ref.at[ff], sem).wait()

    @pl.when(jnp.logical_and(e == 0, f == 0))
    def _():
        for ff in range(MOE_NF):
            src = (ff // steps_per_batch) * N_EXPERTS * cap + (ff % steps_per_batch) * per_step

            def body(r, carry, ff=ff, src=src):
                pltpu.make_async_copy(h_ref.at[ff // steps_per_batch, pl.ds(idx_ref[src + r], 1), :],
                                      rows_ref.at[ff, pl.ds(r, 1), :], sem).start()
                return carry
            lax.fori_loop(0, per_step, body, 0)

    @pl.when(f == 0)
    def _():
        wait_all_rows()
        lane = lax.broadcasted_iota(I32, (per_step, LANES), 1)
        for ff in range(MOE_NF):
            rs = slice(ff * per_step, (ff + 1) * per_step)
            xb_ref[rs, :] = rows_ref[ff, :, :d].astype(BF16)
            gate_ref[rs, :] = jnp.sum(jnp.where(lane == e, rows_ref[ff, :, d:], 0.0), axis=1, keepdims=True)
        acc_ref[...] = jnp.zeros(acc_ref.shape, F32)

    e_next = jnp.minimum(e + 1, ne - 1)
    bi = lax.shift_right_logical(f, steps_per_batch.bit_length() - 1)
    c0 = (f & (steps_per_batch - 1)) * per_step
    src0 = (bi * N_EXPERTS + e_next) * cap + c0
    for r in range(per_step):
        pltpu.make_async_copy(h_ref.at[bi, pl.ds(idx_ref[src0 + r], 1), :],
                              rows_ref.at[f, pl.ds(r, 1), :], sem).start()

    xg = xb_ref[...]
    a = jnp.dot(xg, w1_ref[...].astype(BF16), preferred_element_type=F32)
    g = jnp.dot(xg, w3_ref[...].astype(BF16), preferred_element_type=F32)
    act = (_silu(a) * g).astype(BF16)
    acc_ref[...] += jnp.dot(act, w2_ref[...].astype(BF16), preferred_element_type=F32)

    @pl.when(f == nf - 1)
    def _():
        y_ref[...] = (acc_ref[...] * gate_ref[...]).reshape(nb, cap, d).astype(y_ref.dtype)

    @pl.when(jnp.logical_and(e == ne - 1, f == nf - 1))
    def _():
        wait_all_rows()


def _experts(idx, h2, w1, w3, w2, l, cap):
    b, s, dext = h2.shape
    d = dext - LANES
    e = w1.shape[1]
    ff = w1.shape[-1]
    fc = ff // MOE_NF
    grid_spec = pltpu.PrefetchScalarGridSpec(
        num_scalar_prefetch=1,
        grid=(e, MOE_NF),
        in_specs=[
            pl.BlockSpec(memory_space=pl.ANY),
            pl.BlockSpec((None, None, d, fc), lambda ei, f, ix: (l, ei, 0, f)),
            pl.BlockSpec((None, None, d, fc), lambda ei, f, ix: (l, ei, 0, f)),
            pl.BlockSpec((None, None, fc, d), lambda ei, f, ix: (l, ei, f, 0)),
        ],
        out_specs=pl.BlockSpec((b, None, cap, d), lambda ei, f, ix: (0, ei, 0, 0)),
        scratch_shapes=[pltpu.VMEM((MOE_NF, b * cap // MOE_NF, dext), F32), pltpu.VMEM((b * cap, d), BF16),
                        pltpu.VMEM((b * cap, d), F32), pltpu.VMEM((b * cap, 1), F32),
                        pltpu.SemaphoreType.DMA],
    )
    return pl.pallas_call(
        _expert_kernel,
        grid_spec=grid_spec,
        out_shape=jax.ShapeDtypeStruct((b, e, cap, d), BF16),
        compiler_params=_cparams(("arbitrary", "arbitrary"), 56),
        name="moe_experts",
    )(idx.reshape(-1), h2, w1, w3, w2)


def _scatter_kernel(lo_ref, slott_ref, y_ref, x1_ref, g2_ref, ng_ref, *rest, final_norm):
    if final_norm:
        o_ref, win_ref, full_ref, oh_ref, wsem, fsem = rest
    else:
        sc_ref, sh_ref, o_ref, h_ref, win_ref, full_ref, oh_ref, wsem, fsem = rest
    _scatter_body(lo_ref, slott_ref, y_ref, x1_ref, g2_ref, o_ref, win_ref, full_ref, oh_ref, wsem, fsem)
    xo = o_ref[...]
    xn = xo * lax.rsqrt(jnp.mean(xo * xo, axis=-1, keepdims=True) + EPS) * ng_ref[...]
    if final_norm:
        o_ref[...] = xn
    else:
        h_ref[...] = (xn * (1.0 + sc_ref[...]) + sh_ref[...]).astype(h_ref.dtype)


def _scatter_body(lo_ref, slott_ref, y_ref, x1_ref, g2_ref, o_ref, win_ref, full_ref, oh_ref, wsem, fsem):
    bi = pl.program_id(0)
    ti = pl.program_id(1)
    nb = pl.num_programs(0)
    nt = pl.num_programs(1)
    ts = slott_ref.shape[0]
    cap = y_ref.shape[2]
    step = bi * nt + ti
    buf = step & 1

    def lo_at(b_, t_, e):
        return lo_ref[(b_ * N_EXPERTS + e) * LANES + t_]

    def win_start(b_, t_, e):
        start = lax.shift_left(lax.shift_right_logical(lo_at(b_, t_, e), 4), 4)
        return pl.multiple_of(jnp.minimum(start, cap - MOE_WIN), 16)

    def win_copy(b_, t_, e, slot):
        return pltpu.make_async_copy(y_ref.at[b_, e, pl.ds(win_start(b_, t_, e), MOE_WIN), :],
                                     win_ref.at[slot, pl.ds(e * MOE_WIN, MOE_WIN), :], wsem.at[slot])

    @pl.when(step == 0)
    def _():
        for e in range(N_EXPERTS):
            win_copy(0, 0, e, 0).start()

    @pl.when(step + 1 < nb * nt)
    def _():
        wrap = ti + 1 == nt
        b_n = jnp.where(wrap, bi + 1, bi)
        t_n = jnp.where(wrap, 0, ti + 1)
        for e in range(N_EXPERTS):
            win_copy(b_n, t_n, e, 1 - buf).start()

    for e in range(N_EXPERTS):
        win_copy(bi, ti, e, buf).wait()

    lane = lax.broadcasted_iota(I32, (ts, LANES), 1)

    def slot_col(e):
        return jnp.sum(jnp.where(lane == e, slott_ref[...].astype(F32), 0.0), axis=1, keepdims=True).astype(I32)

    fits = None
    for e in range(N_EXPERTS):
        ok = lo_at(bi, ti + 1, e) - win_start(bi, ti, e) <= MOE_WIN
        fits = ok if fits is None else jnp.logical_and(fits, ok)

    @pl.when(fits)
    def _():
        widx = lax.broadcasted_iota(I32, (ts, MOE_WIN), 1)
        for e in range(N_EXPERTS):
            oh_ref[:, e * MOE_WIN:(e + 1) * MOE_WIN] = jnp.where(
                widx == slot_col(e) - win_start(bi, ti, e), 1.0, 0.0).astype(BF16)
        moe = jnp.dot(oh_ref[...], win_ref[buf], preferred_element_type=F32)
        o_ref[...] = x1_ref[...] + g2_ref[...] * moe

    @pl.when(jnp.logical_not(fits))
    def _():
        o_ref[...] = x1_ref[...]
        cidx = lax.broadcasted_iota(I32, (ts, cap), 1)

        def one_expert(e, carry):
            cp = pltpu.make_async_copy(y_ref.at[bi, e], full_ref, fsem)
            cp.start()
            cp.wait()
            onehot = jnp.where(cidx == slot_col(e), 1.0, 0.0).astype(BF16)
            o_ref[...] += g2_ref[...] * jnp.dot(onehot, full_ref[...], preferred_element_type=F32)
            return carry

        lax.fori_loop(0, N_EXPERTS, one_expert, 0)


def _scatter(lo, slott, y, x1, modr, l, next_g3, final_norm):
    b, s, d = x1.shape
    cap = y.shape[2]
    assert cap >= MOE_WIN and s // MOE_TS < LANES
    row_tile = pl.BlockSpec((None, MOE_TS, d), lambda bi, i, lo_: (bi, i, 0))
    ln = 0 if final_norm else l + 1
    in_specs = [
        pl.BlockSpec((None, MOE_TS, LANES), lambda bi, i, lo_: (bi, i, 0)),
        pl.BlockSpec(memory_space=pl.ANY),
        row_tile,
        pl.BlockSpec((None, None, None, 1, d), lambda bi, i, lo_: (l, bi, 5, 0, 0)),
        pl.BlockSpec((None, 1, d), lambda bi, i, lo_: (ln, 0, 0)),
    ]
    args = [lo.reshape(-1), slott, y, x1, modr, next_g3]
    out_specs = [row_tile]
    out_shape = [jax.ShapeDtypeStruct((b, s, d), F32)]
    if not final_norm:
        in_specs += [pl.BlockSpec((None, None, None, 1, d), lambda bi, i, lo_: (ln, bi, 1, 0, 0)),
                     pl.BlockSpec((None, None, None, 1, d), lambda bi, i, lo_: (ln, bi, 0, 0, 0))]
        args += [modr, modr]
        out_specs.append(row_tile)
        out_shape.append(jax.ShapeDtypeStruct((b, s, d), BF16))
    grid_spec = pltpu.PrefetchScalarGridSpec(
        num_scalar_prefetch=1,
        grid=(b, s // MOE_TS),
        in_specs=in_specs,
        out_specs=out_specs,
        scratch_shapes=[pltpu.VMEM((2, N_EXPERTS * MOE_WIN, d), BF16), pltpu.VMEM((cap, d), BF16),
                        pltpu.VMEM((MOE_TS, N_EXPERTS * MOE_WIN), BF16),
                        pltpu.SemaphoreType.DMA((2,)), pltpu.SemaphoreType.DMA],
    )
    outs = pl.pallas_call(
        functools.partial(_scatter_kernel, final_norm=final_norm),
        grid_spec=grid_spec,
        out_shape=out_shape,
        compiler_params=_cparams(("arbitrary", "arbitrary"), 58),
        name="moe_scatter",
    )(*args)
    return (outs[0], None) if final_norm else (outs[0], outs[1])


def kernel(x, c, positions, w_mod, b_mod, norm1_g, norm2_g, w_in, fnet_w, pool_w, pool_scale,
           lam_q1, lam_k1, lam_q2, lam_k2, sub_g, rel_bias, w_out, w_router, w1, w3, w2, final_g):
    b, s, d = x.shape
    depth = w_mod.shape[0]
    fnet_wd = fnet_w.shape[1] * fnet_w.shape[2]
    pool_wd = pool_w.shape[1] * pool_w.shape[2]
    ab_w = fnet_wd + pool_wd
    in_w = w_in.shape[-1]
    cap = EC_CAPACITY * s // N_EXPERTS

    c8 = jnp.zeros((8, d), F32).at[:b].set(c)
    mod = _modulation(c8, w_mod, b_mod.reshape(depth, 1, N_MOD * d))
    modr = mod[:, :b].reshape(depth, b, N_MOD, 1, d)

    norm1_g3 = norm1_g.reshape(depth, 1, d)
    norm2_g3 = norm2_g.reshape(depth, 1, d)
    pool_scale3 = pool_scale.reshape(depth, 1, pool_wd)
    sub_g3 = sub_g.reshape(depth, 1, DV)
    lam4 = [a.reshape(depth, 1, DK) for a in (lam_q1, lam_k1, lam_q2, lam_k2)]
    w_router_p = jnp.zeros((depth, d, LANES), F32).at[:, :, :N_EXPERTS].set(w_router)

    wd = _dft_matrix(s)
    ab = _fnet_weights(fnet_w, s)
    plan = _attn_plan(positions, rel_bias)
    nbt = _near_bias_table(rel_bias, positions, plan[1], plan[2])

    h1 = _norm_mod(x, norm1_g3, modr, 0, 0, 1)
    for l in range(depth):
        last = l == depth - 1
        h1 = h1.reshape(b * s, d)
        u_pool, za, zb = _project_ab(h1, w_in, ab, l, b, s, ab_w)
        qkv = _project([h1], w_in, l, ab_w, in_w - ab_w, BF16, tn=1024, lead_cols=N_HEADS * 2 * DK,
                       lead_scale=LOG2E * DK ** -0.5, name="proj_in_qkv").reshape(b, s, in_w - ab_w)

        ya = _dft_apply(wd, za, zb, b)
        yb = _pool_mixer(u_pool, pool_w, pool_scale3, l, 0)
        yc = _diff_attention(qkv, nbt, plan, rel_bias, positions, lam4, sub_g3, l)

        mixo = _project([ya.reshape(b * s, -1), yb.reshape(b * s, -1), yc.reshape(b * s, -1)],
                        w_out, l, 0, d, BF16, tn=1024, name="proj_out").reshape(b, s, d)
        x1, h2, aff = _post_mix(x, mixo, modr, norm2_g3, w_router_p, l)

        idx, slott, lo = _route(aff, cap)
        y = _experts(idx, h2, w1, w3, w2, l, cap)
        x, h1 = _scatter(lo, slott, y, x1, modr, l, final_g.reshape(1, 1, d) if last else norm1_g3,
                         final_norm=last)

    return x
```

```python
import functools
import math

import numpy as np
import jax
import jax.numpy as jnp
from jax import lax
from jax.experimental import pallas as pl
from jax.experimental.pallas import tpu as pltpu

F32 = jnp.float32
BF16 = jnp.bfloat16
I32 = jnp.int32

FNET_GROUPS = 4
POOL_GROUPS = 4
POOL_HALO = 64
N_HEADS = 8
DK = 64
DV = 128
N_BUCKETS = 32
MAX_DISTANCE = 128
N_EXPERTS = 16
EC_CAPACITY = 2
N_MOD = 6
EPS = 1e-6
LANES = 128
LOG2E = 1.4426950408889634
ATT_TQ = 256
ATT_TK = 512
ATT_NS = 2
MOE_NF = 4
MOE_TS = 512
MOE_WIN = 128


def _cparams(sem, vmem_mb=None):
    kw = dict(dimension_semantics=sem)
    if vmem_mb is not None:
        kw["vmem_limit_bytes"] = vmem_mb * 1024 * 1024
    return pltpu.CompilerParams(**kw)


def _silu(x):
    return x * jax.nn.sigmoid(x)


def _mod_kernel(c_ref, w_ref, b_ref, o_ref):
    ca = _silu(c_ref[...]).astype(BF16)
    o_ref[...] = jnp.dot(ca, w_ref[...].astype(BF16), preferred_element_type=F32) + b_ref[...]


def _modulation(c8, w_mod, b_mod3):
    depth, d, n = w_mod.shape
    tn = 1024
    return pl.pallas_call(
        _mod_kernel,
        grid=(depth, n // tn),
        in_specs=[
            pl.BlockSpec((8, d), lambda l, j: (0, 0)),
            pl.BlockSpec((None, d, tn), lambda l, j: (l, 0, j)),
            pl.BlockSpec((None, 1, tn), lambda l, j: (l, 0, j)),
        ],
        out_specs=pl.BlockSpec((None, 8, tn), lambda l, j: (l, 0, j)),
        out_shape=jax.ShapeDtypeStruct((depth, 8, n), F32),
        compiler_params=_cparams(("arbitrary", "arbitrary")),
        name="modulation",
    )(c8, w_mod, b_mod3)


def _norm_mod_kernel(x_ref, g_ref, sc_ref, sh_ref, o_ref):
    x = x_ref[...]
    ms = jnp.mean(x * x, axis=-1, keepdims=True)
    y = x * lax.rsqrt(ms + EPS) * g_ref[...]
    o_ref[...] = (y * (1.0 + sc_ref[...]) + sh_ref[...]).astype(o_ref.dtype)


def _norm_mod(x, g3, modr, l, sh_idx, sc_idx):
    b, s, d = x.shape
    tm = 512
    return pl.pallas_call(
        _norm_mod_kernel,
        grid=(b, s // tm),
        in_specs=[
            pl.BlockSpec((None, tm, d), lambda bi, i: (bi, i, 0)),
            pl.BlockSpec((None, 1, d), lambda bi, i: (l, 0, 0)),
            pl.BlockSpec((None, None, None, 1, d), lambda bi, i: (l, bi, sc_idx, 0, 0)),
            pl.BlockSpec((None, None, None, 1, d), lambda bi, i: (l, bi, sh_idx, 0, 0)),
        ],
        out_specs=pl.BlockSpec((None, tm, d), lambda bi, i: (bi, i, 0)),
        out_shape=jax.ShapeDtypeStruct((b, s, d), BF16),
        compiler_params=_cparams(("arbitrary", "arbitrary")),
        name="norm_mod",
    )(x, g3, modr, modr)


def _mm_kernel(*refs, k_sizes, lead_blocks, lead_scale):
    n_a = len(k_sizes)
    a_refs = refs[:n_a]
    w_ref, o_ref, wb_ref = refs[n_a], refs[n_a + 1], refs[n_a + 2]

    @pl.when(pl.program_id(1) == 0)
    def _():
        wb_ref[...] = w_ref[...].astype(BF16)

    acc = None
    off = 0
    for a_ref, ks in zip(a_refs, k_sizes):
        part = jnp.dot(a_ref[...], wb_ref[off:off + ks, :], preferred_element_type=F32)
        acc = part if acc is None else acc + part
        off += ks
    if lead_blocks:
        acc = acc * jnp.where(pl.program_id(0) < lead_blocks, lead_scale, 1.0)
    o_ref[...] = acc.astype(o_ref.dtype)


def _project(a_list, w, l, col0, ncols, out_dtype, tm=1024, tn=512, lead_cols=0, lead_scale=1.0,
             name="project"):
    m = a_list[0].shape[0]
    k_sizes = tuple(a.shape[1] for a in a_list)
    k = sum(k_sizes)
    assert w.shape[1] == k and col0 % tn == 0 and ncols % tn == 0 and m % tm == 0 and lead_cols % tn == 0
    cb0 = col0 // tn
    in_specs = [pl.BlockSpec((tm, ks), lambda j, i: (i, 0)) for ks in k_sizes]
    in_specs.append(pl.BlockSpec((None, k, tn), lambda j, i: (l, 0, cb0 + j)))
    return pl.pallas_call(
        functools.partial(_mm_kernel, k_sizes=k_sizes, lead_blocks=lead_cols // tn, lead_scale=lead_scale),
        grid=(ncols // tn, m // tm),
        in_specs=in_specs,
        out_specs=pl.BlockSpec((tm, tn), lambda j, i: (i, j)),
        out_shape=jax.ShapeDtypeStruct((m, ncols), out_dtype),
        scratch_shapes=[pltpu.VMEM((k, tn), BF16)],
        compiler_params=_cparams(("arbitrary", "arbitrary"), 48),
        name=name,
    )(*a_list, w)


def _dft_tables(s):
    sp = np.arange(s, dtype=np.int64)
    a = np.arange(64, dtype=np.int64)[:, None]
    ang1 = 2.0 * np.pi * ((a * sp[None, :]) % 64) / 64.0
    ang2 = 2.0 * np.pi * ((a * sp[None, :]) % s) / float(s)
    t1c = np.cos(ang1).astype(np.float32).reshape(64, 1, s)
    t1s = np.sin(ang1).astype(np.float32).reshape(64, 1, s)
    t2c = np.cos(ang2).astype(np.float32)
    t2s = np.sin(ang2).astype(np.float32)
    return t1c, t1s, t2c, t2s


def _dftgen_kernel(t1c_ref, t1s_ref, t2c_ref, t2s_ref, o_ref):
    s = t2c_ref.shape[1]
    c1, s1 = t1c_ref[...], t1s_ref[...]
    c2, s2 = t2c_ref[...], t2s_ref[...]
    o_ref[:, :s] = (c1 * c2 - s1 * s2).astype(BF16)
    o_ref[:, s:] = (-(s1 * c2 + c1 * s2)).astype(BF16)


def _dft_matrix(s):
    assert s % 64 == 0 and s // 64 == 64
    t1c, t1s, t2c, t2s = _dft_tables(s)
    return pl.pallas_call(
        _dftgen_kernel,
        grid=(64,),
        in_specs=[
            pl.BlockSpec((None, 1, s), lambda a: (a, 0, 0)),
            pl.BlockSpec((None, 1, s), lambda a: (a, 0, 0)),
            pl.BlockSpec((64, s), lambda a: (0, 0)),
            pl.BlockSpec((64, s), lambda a: (0, 0)),
        ],
        out_specs=pl.BlockSpec((64, 2 * s), lambda a: (a, 0)),
        out_shape=jax.ShapeDtypeStruct((s, 2 * s), BF16),
        compiler_params=_cparams(("arbitrary",)),
        name="dft_matrix",
    )(t1c, t1s, t2c, t2s)


def _fnet_w_kernel(cc_ref, sc_ref, w_ref, o_ref, *, norm):
    depth, groups, cg, _ = w_ref.shape
    cc, sc = cc_ref[...], sc_ref[...]
    for l in range(depth):
        for g in range(groups):
            w = w_ref[l, g]
            a = jnp.dot(cc, w, preferred_element_type=F32, precision=lax.Precision.HIGHEST)
            b = jnp.dot(sc, w, preferred_element_type=F32, precision=lax.Precision.HIGHEST)
            o_ref[l, g, :, :cg] = (a * norm).astype(BF16)
            o_ref[l, g, :, cg:] = (b * norm).astype(BF16)


def _fnet_weights(fnet_w, s):
    depth, groups, cg, _ = fnet_w.shape
    idx = np.arange(cg, dtype=np.int64)
    ang = 2.0 * np.pi * ((idx[:, None] * idx[None, :]) % cg) / float(cg)
    cc = np.cos(ang).astype(np.float32)
    sc = np.sin(ang).astype(np.float32)
    norm = 1.0 / math.sqrt(float(s) * float(cg))
    return pl.pallas_call(
        functools.partial(_fnet_w_kernel, norm=norm),
        out_shape=jax.ShapeDtypeStruct((depth, groups, cg, 2 * cg), BF16),
        name="fnet_weights",
    )(cc, sc, fnet_w)


def _proj_ab_kernel(a_ref, w_ref, ab_ref, ub_ref, za_ref, zb_ref, wb_ref):
    @pl.when(pl.program_id(0) == 0)
    def _():
        wb_ref[...] = w_ref[...].astype(BF16)

    u = jnp.dot(a_ref[...], wb_ref[...], preferred_element_type=F32)
    groups, cg = ab_ref.shape[0], ab_ref.shape[1]
    fw = groups * cg
    ub_ref[...] = u[:, fw:]
    for g in range(groups):
        z = jnp.dot(u[:, g * cg:(g + 1) * cg].astype(BF16), ab_ref[g], preferred_element_type=F32)
        za_ref[:, g * cg:(g + 1) * cg] = z[:, :cg].astype(BF16)
        zb_ref[:, g * cg:(g + 1) * cg] = z[:, cg:].astype(BF16)


def _project_ab(h1, w_in, ab, l, b, s, ab_w):
    m, k = h1.shape
    groups, cg = ab.shape[1], ab.shape[2]
    fw = groups * cg
    tm = 1024
    per_b = s // tm
    z_spec = pl.BlockSpec((tm, fw), lambda i: (i % per_b, i // per_b))
    z_shape = jax.ShapeDtypeStruct((s, b * fw), BF16)
    ub, za, zb = pl.pallas_call(
        _proj_ab_kernel,
        grid=(m // tm,),
        in_specs=[
            pl.BlockSpec((tm, k), lambda i: (i, 0)),
            pl.BlockSpec((None, k, ab_w), lambda i: (l, 0, 0)),
            pl.BlockSpec((None, groups, cg, 2 * cg), lambda i: (l, 0, 0, 0)),
        ],
        out_specs=[pl.BlockSpec((tm, ab_w - fw), lambda i: (i, 0)), z_spec, z_spec],
        out_shape=[jax.ShapeDtypeStruct((m, ab_w - fw), F32), z_shape, z_shape],
        scratch_shapes=[pltpu.VMEM((k, ab_w), BF16)],
        compiler_params=_cparams(("arbitrary",), 48),
        name="proj_in_ab",
    )(h1, w_in, ab)
    return ub.reshape(b, s, ab_w - fw), za, zb


def _dft_apply_kernel(wd_ref, za_ref, zb_ref, o_ref):
    s = za_ref.shape[0]
    acc = jnp.dot(wd_ref[:, :s], za_ref[...], preferred_element_type=F32)
    acc = acc + jnp.dot(wd_ref[:, s:], zb_ref[...], preferred_element_type=F32)
    o_ref[...] = acc.astype(o_ref.dtype)


def _dft_apply(wd, za, zb, b):
    s = wd.shape[0]
    fw = za.shape[1] // b
    tm = 512
    return pl.pallas_call(
        _dft_apply_kernel,
        grid=(b, s // tm),
        in_specs=[
            pl.BlockSpec((tm, 2 * s), lambda bi, i: (i, 0)),
            pl.BlockSpec((s, fw), lambda bi, i: (0, bi)),
            pl.BlockSpec((s, fw), lambda bi, i: (0, bi)),
        ],
        out_specs=pl.BlockSpec((None, tm, fw), lambda bi, i: (bi, i, 0)),
        out_shape=jax.ShapeDtypeStruct((b, s, fw), BF16),
        compiler_params=_cparams(("arbitrary", "arbitrary"), 48),
        name="dft_apply",
    )(wd, za, zb)


def _pool_kernel(u_ref, w_ref, sc_ref, o_ref, pad_ref):
    s, cg = u_ref.shape
    t = 256
    half = jnp.left_shift(jnp.int32(1), pl.program_id(1))
    pad_ref[0:POOL_HALO, :] = jnp.zeros((POOL_HALO, cg), F32)
    pad_ref[s + POOL_HALO:s + 2 * POOL_HALO, :] = jnp.zeros((POOL_HALO, cg), F32)
    pad_ref[POOL_HALO:s + POOL_HALO, :] = u_ref[...]
    ii = lax.broadcasted_iota(I32, (t, t + 2 * POOL_HALO), 0)
    jj = lax.broadcasted_iota(I32, (t, t + 2 * POOL_HALO), 1)
    dlt = jj - ii - POOL_HALO
    band = jnp.where(dlt >= -half, jnp.where(dlt <= half - 1, 1.0, 0.0), 0.0).astype(BF16)
    wb = w_ref[...].astype(BF16)
    scale = sc_ref[...]

    def body(ti, carry):
        r0 = pl.multiple_of(ti * t, t)
        seg = pad_ref[pl.ds(r0, t + 2 * POOL_HALO), :]
        win = jnp.dot(band, seg.astype(BF16), preferred_element_type=F32)
        gi = r0 + lax.broadcasted_iota(I32, (t, cg), 0)
        lo_i = jnp.maximum(gi - half, 0)
        hi_i = jnp.minimum(gi + half - 1, s - 1)
        cnt = (hi_i - lo_i + 1).astype(F32)
        dmean = win / cnt - seg[POOL_HALO:POOL_HALO + t, :]
        y = jnp.dot(dmean.astype(BF16), wb, preferred_element_type=F32) * scale
        o_ref[pl.ds(r0, t), :] = y.astype(o_ref.dtype)
        return carry

    lax.fori_loop(0, s // t, body, 0)


def _pool_mixer(u_ab, pool_w, pool_scale3, l, col_block0):
    b, s, _ = u_ab.shape
    groups, cg = pool_w.shape[1], pool_w.shape[2]
    return pl.pallas_call(
        _pool_kernel,
        grid=(b, groups),
        in_specs=[
            pl.BlockSpec((None, s, cg), lambda bi, g: (bi, 0, col_block0 + g)),
            pl.BlockSpec((None, None, cg, cg), lambda bi, g: (l, g, 0, 0)),
            pl.BlockSpec((None, 1, cg), lambda bi, g: (l, 0, g)),
        ],
        out_specs=pl.BlockSpec((None, s, cg), lambda bi, g: (bi, 0, g)),
        out_shape=jax.ShapeDtypeStruct((b, s, groups * cg), BF16),
        scratch_shapes=[pltpu.VMEM((s + 2 * POOL_HALO, cg), F32)],
        compiler_params=_cparams(("arbitrary", "arbitrary")),
        name="pool_mixer",
    )(u_ab, pool_w, pool_scale3)


def _bucket(rel):
    nb = N_BUCKETS // 2
    max_exact = nb // 2
    n = jnp.abs(rel)
    nf = jnp.maximum(n, 1).astype(F32)
    large = max_exact + (jnp.log(nf / max_exact) / math.log(MAX_DISTANCE / max_exact)
                         * (nb - max_exact)).astype(I32)
    large = jnp.minimum(large, nb - 1)
    return jnp.where(rel > 0, nb, 0) + jnp.where(n < max_exact, n, large)


def _bias_tile(tab_ref, h, rel):
    bucket = _bucket(rel)
    val = jnp.full(rel.shape, tab_ref[h], F32)
    for j in range(1, N_BUCKETS):
        val = jnp.where(bucket == j, tab_ref[j * N_HEADS + h], val)
    return val * LOG2E


def _near_bias_kernel(slotj_ref, nnear_ref, tabt_ref, pq_ref, pk_ref, o_ref):
    del slotj_ref
    used = jnp.where(pl.program_id(1) < nnear_ref[pl.program_id(0)], 1.0, 0.0)
    bucket = _bucket(pk_ref[...] - pq_ref[...])
    tq, tk = bucket.shape
    for h in range(N_HEADS):
        row = jnp.broadcast_to(tabt_ref[h:h + 1, :], (tq, LANES))
        cols = [jnp.take_along_axis(row, bucket[:, c * LANES:(c + 1) * LANES], axis=1)
                for c in range(tk // LANES)]
        o_ref[h] = jnp.concatenate(cols, axis=1) * used


def _near_bias_table(rel_bias, positions, slotj, nnear):
    s = positions.shape[0]
    nq = s // ATT_TQ
    tabt = jnp.zeros((N_HEADS, LANES), F32).at[:, :N_BUCKETS].set(rel_bias.T * LOG2E)
    grid_spec = pltpu.PrefetchScalarGridSpec(
        num_scalar_prefetch=2,
        grid=(nq, ATT_NS),
        in_specs=[
            pl.BlockSpec((N_HEADS, LANES), lambda i, n, sj, nn: (0, 0)),
            pl.BlockSpec((ATT_TQ, 1), lambda i, n, sj, nn: (i, 0)),
            pl.BlockSpec((1, ATT_TK), lambda i, n, sj, nn: (0, sj[i * ATT_NS + n])),
        ],
        out_specs=pl.BlockSpec((N_HEADS, None, None, ATT_TQ, ATT_TK), lambda i, n, sj, nn: (0, i, n, 0, 0)),
    )
    return pl.pallas_call(
        _near_bias_kernel,
        grid_spec=grid_spec,
        out_shape=jax.ShapeDtypeStruct((N_HEADS, nq, ATT_NS, ATT_TQ, ATT_TK), F32),
        compiler_params=_cparams(("arbitrary", "arbitrary")),
        name="near_bias_table",
    )(slotj, nnear, tabt, positions.reshape(s, 1), positions.reshape(1, s))


def _attn_plan(positions, rel_bias):
    s = positions.shape[0]
    nq, nk = s // ATT_TQ, s // ATT_TK
    pq = positions.reshape(nq, ATT_TQ)
    pk = positions.reshape(nk, ATT_TK)
    rel_min = pk.min(axis=1)[None, :] - pq.max(axis=1)[:, None]
    rel_max = pk.max(axis=1)[None, :] - pq.min(axis=1)[:, None]
    cls = jnp.where(rel_min >= MAX_DISTANCE, 1, jnp.where(rel_max <= -MAX_DISTANCE, 0, 2)).astype(I32)
    near = cls == 2
    nnear = near.sum(axis=1).astype(I32)
    slotj = jnp.argsort(jnp.logical_not(near), axis=1, stable=True)[:, :ATT_NS].astype(I32)
    fits = jnp.all(nnear <= ATT_NS)
    nb = N_BUCKETS // 2
    ctab = jnp.stack([rel_bias[nb - 1], rel_bias[2 * nb - 1], jnp.zeros((N_HEADS,), F32)], axis=1) * LOG2E
    return cls.reshape(-1), slotj.reshape(-1), nnear, ctab.reshape(-1).astype(F32), fits


def _lambda(lq1_ref, lk1_ref, lq2_ref, lk2_ref, lam_init):
    return (jnp.exp(jnp.sum(lq1_ref[...] * lk1_ref[...], axis=-1, keepdims=True))
            - jnp.exp(jnp.sum(lq2_ref[...] * lk2_ref[...], axis=-1, keepdims=True)) + lam_init)


def _stack_maps(q):
    lane = lax.broadcasted_iota(I32, q.shape, 1)
    zero = jnp.zeros_like(q)
    return jnp.concatenate([jnp.where(lane < DK, q, zero), jnp.where(lane >= DK, q, zero)], axis=0)


def _attn_finish(o1, o2, lam, sg, lam_init, dtype):
    o = o1 - lam * o2
    ms = jnp.mean(o * o, axis=-1, keepdims=True)
    y = o * lax.rsqrt(ms + EPS) * sg
    return (y * (1.0 - lam_init)).astype(dtype)


def _attn_kernel(cls_ref, slotj_ref, ctab_ref, lq1_ref, lk1_ref, lq2_ref, lk2_ref,
                 q_ref, k_ref, v_ref, qn_ref, kn_ref, nbp0_ref, nbp1_ref, nba0_ref, nba1_ref, nbb0_ref, nbb1_ref,
                 sg_ref, o_ref, sa_scr, sb_scr, ma_scr, mb_scr, vaug_scr, *, lam_init):
    bi = pl.program_id(0)
    h = pl.program_id(1)
    ip = pl.program_id(2)
    n_b = pl.num_programs(0)
    n_ip = pl.num_programs(2)
    tq = ATT_TQ
    s = k_ref.shape[0]
    nk = s // ATT_TK
    lam = _lambda(lq1_ref, lk1_ref, lq2_ref, lk2_ref, lam_init)
    sg = sg_ref[...]

    def lane_tile_max(blk):
        out = blk[:, :LANES]
        for c in range(1, blk.shape[1] // LANES):
            out = jnp.maximum(out, blk[:, c * LANES:(c + 1) * LANES])
        return out

    def scores(t, hh, qr, kr, s_scr, m_scr, nb_refs):
        qs = _stack_maps(qr[pl.ds(pl.multiple_of(t * tq, tq), tq), :])
        mrow = None
        for j in range(nk):
            c = cls_ref[t * nk + j]
            sc = lax.dot_general(qs, kr[j * ATT_TK:(j + 1) * ATT_TK, :], (((1,), (1,)), ((), ())),
                                 preferred_element_type=F32) + ctab_ref[hh * 3 + c]
            s_scr[j] = sc
            cm = lane_tile_max(sc) + jnp.where(c == 2, -1e30, 0.0)
            mrow = cm if mrow is None else jnp.maximum(mrow, cm)
        for n, nb_ref in enumerate(nb_refs):
            j = slotj_ref[t * ATT_NS + n]
            bias = nb_ref[...]
            top = s_scr[j, 0:tq, :] + bias
            bot = s_scr[j, tq:2 * tq, :] + bias
            s_scr[j, 0:tq, :] = top
            s_scr[j, tq:2 * tq, :] = bot
            mrow = jnp.maximum(mrow, jnp.concatenate([lane_tile_max(top), lane_tile_max(bot)], axis=0))
        m_scr[...] = mrow

    def outputs(s_scr, m_scr, row0):
        m = jnp.max(m_scr[...], axis=-1, keepdims=True)
        acc = None
        for j in range(nk):
            e = jnp.exp2(s_scr[j] - m).astype(BF16)
            part = jnp.dot(e, vaug_scr[j * ATT_TK:(j + 1) * ATT_TK, :], preferred_element_type=F32)
            acc = part if acc is None else acc + part
        o1 = acc[:tq, :DV] / acc[:tq, DV:DV + 1]
        o2 = acc[tq:, :DV] / acc[tq:, DV:DV + 1]
        o_ref[row0:row0 + tq, :] = _attn_finish(o1, o2, lam, sg, lam_init, o_ref.dtype)

    @pl.when(ip == 0)
    def _():
        vaug_scr[:, :DV] = v_ref[...]
        lane = lax.broadcasted_iota(I32, (s, DV), 1)
        vaug_scr[:, DV:] = jnp.where(lane == 0, 1.0, 0.0).astype(BF16)

    @pl.when(jnp.logical_and(jnp.logical_and(bi == 0, h == 0), ip == 0))
    def _():
        scores(0, h, q_ref, k_ref, sa_scr, ma_scr, (nbp0_ref, nbp1_ref))

    last = ip == n_ip - 1
    _, h_next = _next_head(bi, h, last, n_b)
    t_next = jnp.where(last, 0, 2 * ip + 2)
    scores(2 * ip + 1, h, q_ref, k_ref, sb_scr, mb_scr, (nbb0_ref, nbb1_ref))
    outputs(sa_scr, ma_scr, 0)
    scores(t_next, h_next, qn_ref, kn_ref, sa_scr, ma_scr, (nba0_ref, nba1_ref))
    outputs(sb_scr, mb_scr, tq)


def _next_head(bi, h, last, n_b):
    lin = jnp.minimum(bi * N_HEADS + h + jnp.where(last, 1, 0), n_b * N_HEADS - 1)
    return lin // N_HEADS, lin % N_HEADS


def _attn_fast(qkv, nbt, plan, lam4, sub_g3, l):
    b, s, _ = qkv.shape
    cls, slotj, _, ctab, _ = plan
    nq = s // ATT_TQ
    n_ip = nq // 2
    assert nq % 2 == 0 and ATT_NS == 2
    lam_init = 0.8 - 0.6 * math.exp(-0.3 * l)
    smem = pl.BlockSpec(memory_space=pltpu.SMEM)
    lam_specs = [pl.BlockSpec((None, 1, DK), lambda bi, h, ip: (l, 0, 0)) for _ in range(4)]
    nb_tile = (None, None, None, ATT_TQ, ATT_TK)

    def nxt(bi, h, ip):
        return _next_head(bi, h, ip == n_ip - 1, b)

    def nba_map(n):
        def index_map(bi, h, ip):
            return (nxt(bi, h, ip)[1], jnp.where(ip == n_ip - 1, 0, 2 * ip + 2), n, 0, 0)
        return index_map

    nb_specs = ([pl.BlockSpec(nb_tile, lambda bi, h, ip, n=n: (0, 0, n, 0, 0)) for n in range(ATT_NS)]
                + [pl.BlockSpec(nb_tile, nba_map(n)) for n in range(ATT_NS)]
                + [pl.BlockSpec(nb_tile, lambda bi, h, ip, n=n: (h, 2 * ip + 1, n, 0, 0)) for n in range(ATT_NS)])
    return pl.pallas_call(
        functools.partial(_attn_kernel, lam_init=lam_init),
        grid=(b, N_HEADS, n_ip),
        in_specs=[smem, smem, smem] + lam_specs + [
            pl.BlockSpec((None, s, 2 * DK), lambda bi, h, ip: (bi, 0, h)),
            pl.BlockSpec((None, s, 2 * DK), lambda bi, h, ip: (bi, 0, N_HEADS + h)),
            pl.BlockSpec((None, s, DV), lambda bi, h, ip: (bi, 0, 2 * N_HEADS + h)),
            pl.BlockSpec((None, s, 2 * DK), lambda bi, h, ip: (nxt(bi, h, ip)[0], 0, nxt(bi, h, ip)[1])),
            pl.BlockSpec((None, s, 2 * DK), lambda bi, h, ip: (nxt(bi, h, ip)[0], 0, N_HEADS + nxt(bi, h, ip)[1])),
        ] + nb_specs + [pl.BlockSpec((None, 1, DV), lambda bi, h, ip: (l, 0, 0))],
        out_specs=pl.BlockSpec((None, 2 * ATT_TQ, DV), lambda bi, h, ip: (bi, ip, h)),
        out_shape=jax.ShapeDtypeStruct((b, s, N_HEADS * DV), BF16),
        scratch_shapes=[pltpu.VMEM((s // ATT_TK, 2 * ATT_TQ, ATT_TK), F32),
                        pltpu.VMEM((s // ATT_TK, 2 * ATT_TQ, ATT_TK), F32),
                        pltpu.VMEM((2 * ATT_TQ, LANES), F32),
                        pltpu.VMEM((2 * ATT_TQ, LANES), F32),
                        pltpu.VMEM((s, 2 * DV), BF16)],
        compiler_params=_cparams(("arbitrary", "arbitrary", "arbitrary"), 56),
        name="diff_attention",
    )(cls, slotj, ctab, *lam4, qkv, qkv, qkv, qkv, qkv, nbt, nbt, nbt, nbt, nbt, nbt, sub_g3)


def _attn_any_kernel(tab_ref, lq1_ref, lk1_ref, lq2_ref, lk2_ref, q_ref, k_ref, v_ref, pq_ref, pk_ref,
                     sg_ref, o_ref, *, lam_init):
    h = pl.program_id(1)
    tq = q_ref.shape[0]
    bias = _bias_tile(tab_ref, h, pk_ref[...] - pq_ref[...])
    sc = lax.dot_general(_stack_maps(q_ref[...]), k_ref[...], (((1,), (1,)), ((), ())),
                         preferred_element_type=F32)
    v = v_ref[...]

    def one_map(sm):
        sm = sm + bias
        e = jnp.exp2(sm - jnp.max(sm, axis=-1, keepdims=True))
        den = jnp.sum(e, axis=-1, keepdims=True)
        return jnp.dot(e.astype(BF16), v, preferred_element_type=F32) / den

    lam = _lambda(lq1_ref, lk1_ref, lq2_ref, lk2_ref, lam_init)
    o_ref[...] = _attn_finish(one_map(sc[:tq]), one_map(sc[tq:]), lam, sg_ref[...], lam_init, o_ref.dtype)


def _attn_any(qkv, rel_bias, positions, lam4, sub_g3, l):
    b, s, _ = qkv.shape
    tq = 128
    lam_init = 0.8 - 0.6 * math.exp(-0.3 * l)
    lam_specs = [pl.BlockSpec((None, 1, DK), lambda bi, h, i: (l, 0, 0)) for _ in range(4)]
    return pl.pallas_call(
        functools.partial(_attn_any_kernel, lam_init=lam_init),
        grid=(b, N_HEADS, s // tq),
        in_specs=[pl.BlockSpec(memory_space=pltpu.SMEM)] + lam_specs + [
            pl.BlockSpec((None, tq, 2 * DK), lambda bi, h, i: (bi, i, h)),
            pl.BlockSpec((None, s, 2 * DK), lambda bi, h, i: (bi, 0, N_HEADS + h)),
            pl.BlockSpec((None, s, DV), lambda bi, h, i: (bi, 0, 2 * N_HEADS + h)),
            pl.BlockSpec((tq, 1), lambda bi, h, i: (i, 0)),
            pl.BlockSpec((1, s), lambda bi, h, i: (0, 0)),
            pl.BlockSpec((None, 1, DV), lambda bi, h, i: (l, 0, 0)),
        ],
        out_specs=pl.BlockSpec((None, tq, DV), lambda bi, h, i: (bi, i, h)),
        out_shape=jax.ShapeDtypeStruct((b, s, N_HEADS * DV), BF16),
        compiler_params=_cparams(("arbitrary", "arbitrary", "arbitrary"), 48),
        name="diff_attention_any",
    )(rel_bias.reshape(-1), *lam4, qkv, qkv, qkv, positions.reshape(s, 1), positions.reshape(1, s), sub_g3)


def _diff_attention(qkv, nbt, plan, rel_bias, positions, lam4, sub_g3, l):
    return lax.cond(plan[4],
                    lambda: _attn_fast(qkv, nbt, plan, lam4, sub_g3, l),
                    lambda: _attn_any(qkv, rel_bias, positions, lam4, sub_g3, l))


def _post_mix_kernel(x_ref, mo_ref, g1_ref, ng_ref, sc_ref, sh_ref, wr_ref, x1_ref, h_ref, lg_ref):
    x1 = x_ref[...] + g1_ref[...] * mo_ref[...].astype(F32)
    x1_ref[...] = x1
    ms = jnp.mean(x1 * x1, axis=-1, keepdims=True)
    h = x1 * lax.rsqrt(ms + EPS) * ng_ref[...]
    h = h * (1.0 + sc_ref[...]) + sh_ref[...]
    d = h.shape[1]
    wr = wr_ref[...]
    h_hi = h.astype(BF16)
    h_lo = (h - h_hi.astype(F32)).astype(BF16)
    w_hi = wr.astype(BF16)
    w_lo = (wr - w_hi.astype(F32)).astype(BF16)
    lg = (jnp.dot(h_hi, w_hi, preferred_element_type=F32) + jnp.dot(h_lo, w_hi, preferred_element_type=F32)
          + jnp.dot(h_hi, w_lo, preferred_element_type=F32))
    lane = lax.broadcasted_iota(I32, lg.shape, 1)
    valid = lane < N_EXPERTS
    lgm = jnp.where(valid, lg, -1e30)
    ex = jnp.where(valid, jnp.exp(lgm - jnp.max(lgm, axis=-1, keepdims=True)), 0.0)
    aff = ex / jnp.sum(ex, axis=-1, keepdims=True)
    h_ref[:, :d] = h
    h_ref[:, d:] = aff
    lg_ref[...] = aff


def _post_mix(x, mixo, modr, norm2_g3, w_router_p, l):
    b, s, d = x.shape
    tm = 512
    mspec = lambda idx: pl.BlockSpec((None, None, None, 1, d), lambda bi, i: (l, bi, idx, 0, 0))
    return pl.pallas_call(
        _post_mix_kernel,
        grid=(b, s // tm),
        in_specs=[
            pl.BlockSpec((None, tm, d), lambda bi, i: (bi, i, 0)),
            pl.BlockSpec((None, tm, d), lambda bi, i: (bi, i, 0)),
            mspec(2),
            pl.BlockSpec((None, 1, d), lambda bi, i: (l, 0, 0)),
            mspec(4),
            mspec(3),
            pl.BlockSpec((None, d, LANES), lambda bi, i: (l, 0, 0)),
        ],
        out_specs=[
            pl.BlockSpec((None, tm, d), lambda bi, i: (bi, i, 0)),
            pl.BlockSpec((None, tm, d + LANES), lambda bi, i: (bi, i, 0)),
            pl.BlockSpec((None, tm, LANES), lambda bi, i: (bi, i, 0)),
        ],
        out_shape=[
            jax.ShapeDtypeStruct((b, s, d), F32),
            jax.ShapeDtypeStruct((b, s, d + LANES), F32),
            jax.ShapeDtypeStruct((b, s, LANES), F32),
        ],
        compiler_params=_cparams(("arbitrary", "arbitrary")),
        name="post_mix",
    )(x, mixo, modr, norm2_g3, modr, modr, w_router_p)


def _cumsum_lanes(x01):
    rows, n = x01.shape
    blk = MOE_TS
    ii = lax.broadcasted_iota(I32, (blk, blk), 0)
    jj = lax.broadcasted_iota(I32, (blk, blk), 1)
    tri = jnp.where(ii <= jj, 1.0, 0.0).astype(BF16)
    xb = x01.astype(BF16)
    carry = jnp.zeros((rows, 1), F32)
    outs, totals = [], [carry]
    for c in range(n // blk):
        part = jnp.dot(xb[:, c * blk:(c + 1) * blk], tri, preferred_element_type=F32) + carry
        outs.append(part)
        carry = part[:, blk - 1:blk]
        totals.append(carry)
    return jnp.concatenate(outs, axis=1), totals


def _route_kernel(aff_ref, idx_ref, slott_ref, lo_ref, cs_scr, *, cap):
    at = aff_ref[...].T[:N_EXPERTS, :]
    keys = lax.bitcast_convert_type(at, I32)

    def body(i, prefix):
        cand = prefix | jnp.left_shift(jnp.int32(1), 30 - i)
        cnt = jnp.sum(jnp.where(keys >= cand, 1.0, 0.0), axis=1, keepdims=True)
        return jnp.where(cnt >= cap, cand, prefix)

    thr = lax.fori_loop(0, 31, body, jnp.zeros((N_EXPERTS, 1), I32))
    gt = jnp.where(keys > thr, 1.0, 0.0)
    eq = jnp.where(keys == thr, 1.0, 0.0)
    need = cap - jnp.sum(gt, axis=1, keepdims=True)
    take = eq * jnp.where(_cumsum_lanes(eq)[0] <= need, 1.0, 0.0)
    sel = gt + take
    cs, totals = _cumsum_lanes(sel)
    slot = jnp.where(sel > 0.5, cs - 1.0, -1.0).astype(I32)
    s = slot.shape[1]
    pad = jnp.full((LANES - N_EXPERTS, s), -1, I32)
    slott_ref[...] = jnp.concatenate([slot, pad], axis=0).T

    lane_e = lax.broadcasted_iota(I32, (N_EXPERTS, LANES), 1)
    lo = jnp.zeros((N_EXPERTS, LANES), F32)
    for j, tot in enumerate(totals):
        lo = jnp.where(lane_e == j, tot, lo)
    lo_ref[...] = lo.astype(I32)

    cs_scr[...] = cs
    lane = lax.broadcasted_iota(I32, (cap, LANES), 1)
    rows = 128

    def one_expert(e, idxt):
        row = cs_scr[pl.ds(e, 1), :]
        cols = []
        for c0 in range(0, cap, rows):
            cio = (lax.broadcasted_iota(I32, (rows, s), 0) + c0).astype(F32)
            cols.append(jnp.sum(jnp.where(row <= cio, 1.0, 0.0), axis=1, keepdims=True))
        return jnp.where(lane == e, jnp.concatenate(cols, axis=0), idxt)

    idxt = lax.fori_loop(0, N_EXPERTS, one_expert, jnp.zeros((cap, LANES), F32))
    idx_ref[...] = idxt.T[:N_EXPERTS, :].astype(I32)


def _route(aff, cap):
    b, s, _ = aff.shape
    return pl.pallas_call(
        functools.partial(_route_kernel, cap=cap),
        grid=(b,),
        in_specs=[pl.BlockSpec((None, s, LANES), lambda bi: (bi, 0, 0))],
        out_specs=[
            pl.BlockSpec((None, N_EXPERTS, cap), lambda bi: (bi, 0, 0)),
            pl.BlockSpec((None, s, LANES), lambda bi: (bi, 0, 0)),
            pl.BlockSpec((None, N_EXPERTS, LANES), lambda bi: (bi, 0, 0)),
        ],
        out_shape=[
            jax.ShapeDtypeStruct((b, N_EXPERTS, cap), I32),
            jax.ShapeDtypeStruct((b, s, LANES), I32),
            jax.ShapeDtypeStruct((b, N_EXPERTS, LANES), I32),
        ],
        scratch_shapes=[pltpu.VMEM((N_EXPERTS, s), F32)],
        compiler_params=_cparams(("arbitrary",), 48),
        name="route",
    )(aff)


def _expert_kernel(idx_ref, h_ref, w1_ref, w3_ref, w2_ref, y_ref, rows_ref, xb_ref, acc_ref, gate_ref, sem):
    e = pl.program_id(0)
    f = pl.program_id(1)
    ne = pl.num_programs(0)
    nf = pl.num_programs(1)
    nb, cap, d = y_ref.shape
    n_rows = nb * cap
    per_step = n_rows // MOE_NF

    steps_per_batch = cap // per_step
    assert steps_per_batch & (steps_per_batch - 1) == 0 and steps_per_batch * per_step == cap

    def wait_all_rows():
        for ff in range(MOE_NF):
            pltpu.make_async_copy(h_ref.at[0, pl.ds(0, per_step), :], rows_ref.at[ff], sem).wait()

    @pl.when(jnp.logical_and(e == 0, f == 0))
    def _():
        for ff in range(MOE_NF):
            src = (ff // steps_per_batch) * N_EXPERTS * cap + (ff % steps_per_batch) * per_step

            def body(r, carry, ff=ff, src=src):
                pltpu.make_async_copy(h_ref.at[ff // steps_per_batch, pl.ds(idx_ref[src + r], 1), :],
                                      rows_ref.at[ff, pl.ds(r, 1), :], sem).start()
                return carry
            lax.fori_loop(0, per_step, body, 0)
        acc_ref[...] = jnp.zeros(acc_ref.shape, F32)

    @pl.when(f == 0)
    def _():
        wait_all_rows()
        lane = lax.broadcasted_iota(I32, (per_step, LANES), 1)
        for ff in range(MOE_NF):
            rs = slice(ff * per_step, (ff + 1) * per_step)
            xb_ref[rs, :] = rows_ref[ff, :, :d].astype(BF16)
            gate_ref[rs, :] = jnp.sum(jnp.where(lane == e, rows_ref[ff, :, d:], 0.0), axis=1, keepdims=True)

    e_next = jnp.minimum(e + 1, ne - 1)
    bi = lax.shift_right_logical(f, steps_per_batch.bit_length() - 1)
    c0 = (f & (steps_per_batch - 1)) * per_step
    src0 = (bi * N_EXPERTS + e_next) * cap + c0
    for r in range(per_step):
        pltpu.make_async_copy(h_ref.at[bi, pl.ds(idx_ref[src0 + r], 1), :],
                              rows_ref.at[f, pl.ds(r, 1), :], sem).start()

    xg = xb_ref[...]
    a = jnp.dot(xg, w1_ref[...].astype(BF16), preferred_element_type=F32)
    g = jnp.dot(xg, w3_ref[...].astype(BF16), preferred_element_type=F32)
    act = (_silu(a) * g).astype(BF16)
    prev = jnp.where(f == 0, 0.0, acc_ref[...])
    acc_ref[...] = prev + jnp.dot(act, w2_ref[...].astype(BF16), preferred_element_type=F32)

    @pl.when(f == nf - 1)
    def _():
        y_ref[...] = (acc_ref[...] * gate_ref[...]).reshape(nb, cap, d).astype(y_ref.dtype)

    @pl.when(jnp.logical_and(e == ne - 1, f == nf - 1))
    def _():
        wait_all_rows()


def _experts(idx, h2, w1, w3, w2, l, cap):
    b, s, dext = h2.shape
    d = dext - LANES
    e = w1.shape[1]
    ff = w1.shape[-1]
    fc = ff // MOE_NF
    grid_spec = pltpu.PrefetchScalarGridSpec(
        num_scalar_prefetch=1,
        grid=(e, MOE_NF),
        in_specs=[
            pl.BlockSpec(memory_space=pl.ANY),
            pl.BlockSpec((None, None, d, fc), lambda ei, f, ix: (l, ei, 0, f)),
            pl.BlockSpec((None, None, d, fc), lambda ei, f, ix: (l, ei, 0, f)),
            pl.BlockSpec((None, None, fc, d), lambda ei, f, ix: (l, ei, f, 0)),
        ],
        out_specs=pl.BlockSpec((b, None, cap, d), lambda ei, f, ix: (0, ei, 0, 0)),
        scratch_shapes=[pltpu.VMEM((MOE_NF, b * cap // MOE_NF, dext), F32), pltpu.VMEM((b * cap, d), BF16),
                        pltpu.VMEM((b * cap, d), F32), pltpu.VMEM((b * cap, 1), F32),
                        pltpu.SemaphoreType.DMA],
    )
    return pl.pallas_call(
        _expert_kernel,
        grid_spec=grid_spec,
        out_shape=jax.ShapeDtypeStruct((b, e, cap, d), BF16),
        compiler_params=_cparams(("arbitrary", "arbitrary"), 56),
        name="moe_experts",
    )(idx.reshape(-1), h2, w1, w3, w2)


def _scatter_kernel(lo_ref, slott_ref, y_ref, x1_ref, g2_ref, ng_ref, *rest, final_norm):
    if final_norm:
        o_ref, win_ref, full_ref, oh_ref, wsem, fsem = rest
    else:
        sc_ref, sh_ref, o_ref, h_ref, win_ref, full_ref, oh_ref, wsem, fsem = rest
    _scatter_body(lo_ref, slott_ref, y_ref, x1_ref, g2_ref, o_ref, win_ref, full_ref, oh_ref, wsem, fsem)
    xo = o_ref[...]
    xn = xo * lax.rsqrt(jnp.mean(xo * xo, axis=-1, keepdims=True) + EPS) * ng_ref[...]
    if final_norm:
        o_ref[...] = xn
    else:
        h_ref[...] = (xn * (1.0 + sc_ref[...]) + sh_ref[...]).astype(h_ref.dtype)


def _scatter_body(lo_ref, slott_ref, y_ref, x1_ref, g2_ref, o_ref, win_ref, full_ref, oh_ref, wsem, fsem):
    bi = pl.program_id(0)
    ti = pl.program_id(1)
    nb = pl.num_programs(0)
    nt = pl.num_programs(1)
    ts = slott_ref.shape[0]
    cap = y_ref.shape[2]
    step = bi * nt + ti
    buf = step & 1

    def lo_at(b_, t_, e):
        return lo_ref[(b_ * N_EXPERTS + e) * LANES + t_]

    def win_start(b_, t_, e):
        start = lax.shift_left(lax.shift_right_logical(lo_at(b_, t_, e), 4), 4)
        return pl.multiple_of(jnp.minimum(start, cap - MOE_WIN), 16)

    def win_copy(b_, t_, e, slot):
        return pltpu.make_async_copy(y_ref.at[b_, e, pl.ds(win_start(b_, t_, e), MOE_WIN), :],
                                     win_ref.at[slot, pl.ds(e * MOE_WIN, MOE_WIN), :], wsem.at[slot])

    @pl.when(step == 0)
    def _():
        for e in range(N_EXPERTS):
            win_copy(0, 0, e, 0).start()

    @pl.when(step + 1 < nb * nt)
    def _():
        wrap = ti + 1 == nt
        b_n = jnp.where(wrap, bi + 1, bi)
        t_n = jnp.where(wrap, 0, ti + 1)
        for e in range(N_EXPERTS):
            win_copy(b_n, t_n, e, 1 - buf).start()

    for e in range(N_EXPERTS):
        win_copy(bi, ti, e, buf).wait()

    lane = lax.broadcasted_iota(I32, (ts, LANES), 1)

    def slot_col(e):
        return jnp.sum(jnp.where(lane == e, slott_ref[...].astype(F32), 0.0), axis=1, keepdims=True).astype(I32)

    fits = None
    for e in range(N_EXPERTS):
        ok = lo_at(bi, ti + 1, e) - win_start(bi, ti, e) <= MOE_WIN
        fits = ok if fits is None else jnp.logical_and(fits, ok)

    @pl.when(fits)
    def _():
        widx = lax.broadcasted_iota(I32, (ts, MOE_WIN), 1)
        for e in range(N_EXPERTS):
            oh_ref[:, e * MOE_WIN:(e + 1) * MOE_WIN] = jnp.where(
                widx == slot_col(e) - win_start(bi, ti, e), 1.0, 0.0).astype(BF16)
        moe = jnp.dot(oh_ref[...], win_ref[buf], preferred_element_type=F32)
        o_ref[...] = x1_ref[...] + g2_ref[...] * moe

    @pl.when(jnp.logical_not(fits))
    def _():
        o_ref[...] = x1_ref[...]
        cidx = lax.broadcasted_iota(I32, (ts, cap), 1)

        def one_expert(e, carry):
            cp = pltpu.make_async_copy(y_ref.at[bi, e], full_ref, fsem)
            cp.start()
            cp.wait()
            onehot = jnp.where(cidx == slot_col(e), 1.0, 0.0).astype(BF16)
            o_ref[...] += g2_ref[...] * jnp.dot(onehot, full_ref[...], preferred_element_type=F32)
            return carry

        lax.fori_loop(0, N_EXPERTS, one_expert, 0)


def _scatter(lo, slott, y, x1, modr, l, next_g3, final_norm):
    b, s, d = x1.shape
    cap = y.shape[2]
    assert cap >= MOE_WIN and s // MOE_TS < LANES
    row_tile = pl.BlockSpec((None, MOE_TS, d), lambda bi, i, lo_: (bi, i, 0))
    ln = 0 if final_norm else l + 1
    in_specs = [
        pl.BlockSpec((None, MOE_TS, LANES), lambda bi, i, lo_: (bi, i, 0)),
        pl.BlockSpec(memory_space=pl.ANY),
        row_tile,
        pl.BlockSpec((None, None, None, 1, d), lambda bi, i, lo_: (l, bi, 5, 0, 0)),
        pl.BlockSpec((None, 1, d), lambda bi, i, lo_: (ln, 0, 0)),
    ]
    args = [lo.reshape(-1), slott, y, x1, modr, next_g3]
    out_specs = [row_tile]
    out_shape = [jax.ShapeDtypeStruct((b, s, d), F32)]
    if not final_norm:
        in_specs += [pl.BlockSpec((None, None, None, 1, d), lambda bi, i, lo_: (ln, bi, 1, 0, 0)),
                     pl.BlockSpec((None, None, None, 1, d), lambda bi, i, lo_: (ln, bi, 0, 0, 0))]
        args += [modr, modr]
        out_specs.append(row_tile)
        out_shape.append(jax.ShapeDtypeStruct((b, s, d), BF16))
    grid_spec = pltpu.PrefetchScalarGridSpec(
        num_scalar_prefetch=1,
        grid=(b, s // MOE_TS),
        in_specs=in_specs,
        out_specs=out_specs,
        scratch_shapes=[pltpu.VMEM((2, N_EXPERTS * MOE_WIN, d), BF16), pltpu.VMEM((cap, d), BF16),
                        pltpu.VMEM((MOE_TS, N_EXPERTS * MOE_WIN), BF16),
                        pltpu.SemaphoreType.DMA((2,)), pltpu.SemaphoreType.DMA],
    )
    outs = pl.pallas_call(
        functools.partial(_scatter_kernel, final_norm=final_norm),
        grid_spec=grid_spec,
        out_shape=out_shape,
        compiler_params=_cparams(("arbitrary", "arbitrary"), 58),
        name="moe_scatter",
    )(*args)
    return (outs[0], None) if final_norm else (outs[0], outs[1])


def kernel(x, c, positions, w_mod, b_mod, norm1_g, norm2_g, w_in, fnet_w, pool_w, pool_scale,
           lam_q1, lam_k1, lam_q2, lam_k2, sub_g, rel_bias, w_out, w_router, w1, w3, w2, final_g):
    b, s, d = x.shape
    depth = w_mod.shape[0]
    fnet_wd = fnet_w.shape[1] * fnet_w.shape[2]
    pool_wd = pool_w.shape[1] * pool_w.shape[2]
    ab_w = fnet_wd + pool_wd
    in_w = w_in.shape[-1]
    cap = EC_CAPACITY * s // N_EXPERTS

    c8 = jnp.zeros((8, d), F32).at[:b].set(c)
    mod = _modulation(c8, w_mod, b_mod.reshape(depth, 1, N_MOD * d))
    modr = mod[:, :b].reshape(depth, b, N_MOD, 1, d)

    norm1_g3 = norm1_g.reshape(depth, 1, d)
    norm2_g3 = norm2_g.reshape(depth, 1, d)
    pool_scale3 = pool_scale.reshape(depth, 1, pool_wd)
    sub_g3 = sub_g.reshape(depth, 1, DV)
    lam4 = [a.reshape(depth, 1, DK) for a in (lam_q1, lam_k1, lam_q2, lam_k2)]
    w_router_p = jnp.zeros((depth, d, LANES), F32).at[:, :, :N_EXPERTS].set(w_router)

    wd = _dft_matrix(s)
    ab = _fnet_weights(fnet_w, s)
    plan = _attn_plan(positions, rel_bias)
    nbt = _near_bias_table(rel_bias, positions, plan[1], plan[2])

    h1 = _norm_mod(x, norm1_g3, modr, 0, 0, 1)
    for l in range(depth):
        last = l == depth - 1
        h1 = h1.reshape(b * s, d)
        u_pool, za, zb = _project_ab(h1, w_in, ab, l, b, s, ab_w)
        qkv = _project([h1], w_in, l, ab_w, in_w - ab_w, BF16, tn=1024, lead_cols=N_HEADS * 2 * DK,
                       lead_scale=LOG2E * DK ** -0.5, name="proj_in_qkv").reshape(b, s, in_w - ab_w)

        ya = _dft_apply(wd, za, zb, b)
        yb = _pool_mixer(u_pool, pool_w, pool_scale3, l, 0)
        yc = _diff_attention(qkv, nbt, plan, rel_bias, positions, lam4, sub_g3, l)

        mixo = _project([ya.reshape(b * s, -1), yb.reshape(b * s, -1), yc.reshape(b * s, -1)],
                        w_out, l, 0, d, BF16, tn=1024, name="proj_out").reshape(b, s, d)
        x1, h2, aff = _post_mix(x, mixo, modr, norm2_g3, w_router_p, l)

        idx, slott, lo = _route(aff, cap)
        y = _experts(idx, h2, w1, w3, w2, l, cap)
        x, h1 = _scatter(lo, slott, y, x1, modr, l, final_g.reshape(1, 1, d) if last else norm1_g3,
                         final_norm=last)

    return x
```

```python
import functools
import math

import numpy as np
import jax
import jax.numpy as jnp
from jax import lax
from jax.experimental import pallas as pl
from jax.experimental.pallas import tpu as pltpu

F32 = jnp.float32
BF16 = jnp.bfloat16
I32 = jnp.int32

FNET_GROUPS = 4
POOL_GROUPS = 4
POOL_HALO = 64
N_HEADS = 8
DK = 64
DV = 128
N_BUCKETS = 32
MAX_DISTANCE = 128
N_EXPERTS = 16
EC_CAPACITY = 2
N_MOD = 6
EPS = 1e-6
LANES = 128
LOG2E = 1.4426950408889634
ATT_TQ = 256
ATT_TK = 512
ATT_NS = 2
MOE_NF = 4
MOE_TS = 512
MOE_WIN = 128


def _cparams(sem, vmem_mb=None):
    kw = dict(dimension_semantics=sem)
    if vmem_mb is not None:
        kw["vmem_limit_bytes"] = vmem_mb * 1024 * 1024
    return pltpu.CompilerParams(**kw)


def _silu(x):
    return x * jax.nn.sigmoid(x)


def _mod_kernel(c_ref, w_ref, b_ref, o_ref):
    ca = _silu(c_ref[...]).astype(BF16)
    o_ref[...] = jnp.dot(ca, w_ref[...].astype(BF16), preferred_element_type=F32) + b_ref[...]


def _modulation(c8, w_mod, b_mod3):
    depth, d, n = w_mod.shape
    tn = 1024
    return pl.pallas_call(
        _mod_kernel,
        grid=(depth, n // tn),
        in_specs=[
            pl.BlockSpec((8, d), lambda l, j: (0, 0)),
            pl.BlockSpec((None, d, tn), lambda l, j: (l, 0, j)),
            pl.BlockSpec((None, 1, tn), lambda l, j: (l, 0, j)),
        ],
        out_specs=pl.BlockSpec((None, 8, tn), lambda l, j: (l, 0, j)),
        out_shape=jax.ShapeDtypeStruct((depth, 8, n), F32),
        compiler_params=_cparams(("arbitrary", "arbitrary")),
        name="modulation",
    )(c8, w_mod, b_mod3)


def _norm_mod_kernel(x_ref, g_ref, sc_ref, sh_ref, o_ref):
    x = x_ref[...]
    ms = jnp.mean(x * x, axis=-1, keepdims=True)
    y = x * lax.rsqrt(ms + EPS) * g_ref[...]
    o_ref[...] = (y * (1.0 + sc_ref[...]) + sh_ref[...]).astype(o_ref.dtype)


def _norm_mod(x, g3, modr, l, sh_idx, sc_idx):
    b, s, d = x.shape
    tm = 512
    return pl.pallas_call(
        _norm_mod_kernel,
        grid=(b, s // tm),
        in_specs=[
            pl.BlockSpec((None, tm, d), lambda bi, i: (bi, i, 0)),
            pl.BlockSpec((None, 1, d), lambda bi, i: (l, 0, 0)),
            pl.BlockSpec((None, None, None, 1, d), lambda bi, i: (l, bi, sc_idx, 0, 0)),
            pl.BlockSpec((None, None, None, 1, d), lambda bi, i: (l, bi, sh_idx, 0, 0)),
        ],
        out_specs=pl.BlockSpec((None, tm, d), lambda bi, i: (bi, i, 0)),
        out_shape=jax.ShapeDtypeStruct((b, s, d), BF16),
        compiler_params=_cparams(("arbitrary", "arbitrary")),
        name="norm_mod",
    )(x, g3, modr, modr)


def _mm_kernel(*refs, k_sizes, lead_blocks, lead_scale):
    n_a = len(k_sizes)
    a_refs = refs[:n_a]
    w_ref, o_ref, wb_ref = refs[n_a], refs[n_a + 1], refs[n_a + 2]

    @pl.when(pl.program_id(1) == 0)
    def _():
        wb_ref[...] = w_ref[...].astype(BF16)

    acc = None
    off = 0
    for a_ref, ks in zip(a_refs, k_sizes):
        part = jnp.dot(a_ref[...], wb_ref[off:off + ks, :], preferred_element_type=F32)
        acc = part if acc is None else acc + part
        off += ks
    if lead_blocks:
        acc = acc * jnp.where(pl.program_id(0) < lead_blocks, lead_scale, 1.0)
    o_ref[...] = acc.astype(o_ref.dtype)


def _project(a_list, w, l, col0, ncols, out_dtype, tm=1024, tn=512, lead_cols=0, lead_scale=1.0,
             name="project"):
    m = a_list[0].shape[0]
    k_sizes = tuple(a.shape[1] for a in a_list)
    k = sum(k_sizes)
    assert w.shape[1] == k and col0 % tn == 0 and ncols % tn == 0 and m % tm == 0 and lead_cols % tn == 0
    cb0 = col0 // tn
    in_specs = [pl.BlockSpec((tm, ks), lambda j, i: (i, 0)) for ks in k_sizes]
    in_specs.append(pl.BlockSpec((None, k, tn), lambda j, i: (l, 0, cb0 + j)))
    return pl.pallas_call(
        functools.partial(_mm_kernel, k_sizes=k_sizes, lead_blocks=lead_cols // tn, lead_scale=lead_scale),
        grid=(ncols // tn, m // tm),
        in_specs=in_specs,
        out_specs=pl.BlockSpec((tm, tn), lambda j, i: (i, j)),
        out_shape=jax.ShapeDtypeStruct((m, ncols), out_dtype),
        scratch_shapes=[pltpu.VMEM((k, tn), BF16)],
        compiler_params=_cparams(("arbitrary", "arbitrary"), 48),
        name=name,
    )(*a_list, w)


def _dft_tables(s):
    sp = np.arange(s, dtype=np.int64)
    a = np.arange(64, dtype=np.int64)[:, None]
    ang1 = 2.0 * np.pi * ((a * sp[None, :]) % 64) / 64.0
    ang2 = 2.0 * np.pi * ((a * sp[None, :]) % s) / float(s)
    t1c = np.cos(ang1).astype(np.float32).reshape(64, 1, s)
    t1s = np.sin(ang1).astype(np.float32).reshape(64, 1, s)
    t2c = np.cos(ang2).astype(np.float32)
    t2s = np.sin(ang2).astype(np.float32)
    return t1c, t1s, t2c, t2s


def _dftgen_kernel(t1c_ref, t1s_ref, t2c_ref, t2s_ref, o_ref):
    s = t2c_ref.shape[1]
    c1, s1 = t1c_ref[...], t1s_ref[...]
    c2, s2 = t2c_ref[...], t2s_ref[...]
    o_ref[:, :s] = (c1 * c2 - s1 * s2).astype(BF16)
    o_ref[:, s:] = (-(s1 * c2 + c1 * s2)).astype(BF16)


def _dft_matrix(s):
    assert s % 64 == 0 and s // 64 == 64
    t1c, t1s, t2c, t2s = _dft_tables(s)
    return pl.pallas_call(
        _dftgen_kernel,
        grid=(64,),
        in_specs=[
            pl.BlockSpec((None, 1, s), lambda a: (a, 0, 0)),
            pl.BlockSpec((None, 1, s), lambda a: (a, 0, 0)),
            pl.BlockSpec((64, s), lambda a: (0, 0)),
            pl.BlockSpec((64, s), lambda a: (0, 0)),
        ],
        out_specs=pl.BlockSpec((64, 2 * s), lambda a: (a, 0)),
        out_shape=jax.ShapeDtypeStruct((s, 2 * s), BF16),
        compiler_params=_cparams(("arbitrary",)),
        name="dft_matrix",
    )(t1c, t1s, t2c, t2s)


def _fnet_w_kernel(cc_ref, sc_ref, w_ref, o_ref, *, norm):
    depth, groups, cg, _ = w_ref.shape
    cc, sc = cc_ref[...], sc_ref[...]
    for l in range(depth):
        for g in range(groups):
            w = w_ref[l, g]
            a = jnp.dot(cc, w, preferred_element_type=F32, precision=lax.Precision.HIGHEST)
            b = jnp.dot(sc, w, preferred_element_type=F32, precision=lax.Precision.HIGHEST)
            o_ref[l, g, :, :cg] = (a * norm).astype(BF16)
            o_ref[l, g, :, cg:] = (b * norm).astype(BF16)


def _fnet_weights(fnet_w, s):
    depth, groups, cg, _ = fnet_w.shape
    idx = np.arange(cg, dtype=np.int64)
    ang = 2.0 * np.pi * ((idx[:, None] * idx[None, :]) % cg) / float(cg)
    cc = np.cos(ang).astype(np.float32)
    sc = np.sin(ang).astype(np.float32)
    norm = 1.0 / math.sqrt(float(s) * float(cg))
    return pl.pallas_call(
        functools.partial(_fnet_w_kernel, norm=norm),
        out_shape=jax.ShapeDtypeStruct((depth, groups, cg, 2 * cg), BF16),
        name="fnet_weights",
    )(cc, sc, fnet_w)


def _proj_ab_kernel(a_ref, w_ref, ab_ref, ub_ref, za_ref, zb_ref, wb_ref):
    @pl.when(pl.program_id(0) == 0)
    def _():
        wb_ref[...] = w_ref[...].astype(BF16)

    u = jnp.dot(a_ref[...], wb_ref[...], preferred_element_type=F32)
    groups, cg = ab_ref.shape[0], ab_ref.shape[1]
    fw = groups * cg
    ub_ref[...] = u[:, fw:]
    for g in range(groups):
        z = jnp.dot(u[:, g * cg:(g + 1) * cg].astype(BF16), ab_ref[g], preferred_element_type=F32)
        za_ref[:, g * cg:(g + 1) * cg] = z[:, :cg].astype(BF16)
        zb_ref[:, g * cg:(g + 1) * cg] = z[:, cg:].astype(BF16)


def _project_ab(h1, w_in, ab, l, b, s, ab_w):
    m, k = h1.shape
    groups, cg = ab.shape[1], ab.shape[2]
    fw = groups * cg
    tm = 1024
    per_b = s // tm
    z_spec = pl.BlockSpec((tm, fw), lambda i: (i % per_b, i // per_b))
    z_shape = jax.ShapeDtypeStruct((s, b * fw), BF16)
    ub, za, zb = pl.pallas_call(
        _proj_ab_kernel,
        grid=(m // tm,),
        in_specs=[
            pl.BlockSpec((tm, k), lambda i: (i, 0)),
            pl.BlockSpec((None, k, ab_w), lambda i: (l, 0, 0)),
            pl.BlockSpec((None, groups, cg, 2 * cg), lambda i: (l, 0, 0, 0)),
        ],
        out_specs=[pl.BlockSpec((tm, ab_w - fw), lambda i: (i, 0)), z_spec, z_spec],
        out_shape=[jax.ShapeDtypeStruct((m, ab_w - fw), F32), z_shape, z_shape],
        scratch_shapes=[pltpu.VMEM((k, ab_w), BF16)],
        compiler_params=_cparams(("arbitrary",), 48),
        name="proj_in_ab",
    )(h1, w_in, ab)
    return ub.reshape(b, s, ab_w - fw), za, zb


def _dft_apply_kernel(wd_ref, za_ref, zb_ref, o_ref):
    s = za_ref.shape[0]
    acc = jnp.dot(wd_ref[:, :s], za_ref[...], preferred_element_type=F32)
    acc = acc + jnp.dot(wd_ref[:, s:], zb_ref[...], preferred_element_type=F32)
    o_ref[...] = acc.astype(o_ref.dtype)


def _dft_apply(wd, za, zb, b):
    s = wd.shape[0]
    fw = za.shape[1] // b
    tm = 512
    return pl.pallas_call(
        _dft_apply_kernel,
        grid=(b, s // tm),
        in_specs=[
            pl.BlockSpec((tm, 2 * s), lambda bi, i: (i, 0)),
            pl.BlockSpec((s, fw), lambda bi, i: (0, bi)),
            pl.BlockSpec((s, fw), lambda bi, i: (0, bi)),
        ],
        out_specs=pl.BlockSpec((None, tm, fw), lambda bi, i: (bi, i, 0)),
        out_shape=jax.ShapeDtypeStruct((b, s, fw), BF16),
        compiler_params=_cparams(("arbitrary", "arbitrary"), 48),
        name="dft_apply",
    )(wd, za, zb)


def _pool_kernel(u_ref, w_ref, sc_ref, o_ref, pad_ref):
    s, cg = u_ref.shape
    t = 256
    half = jnp.left_shift(jnp.int32(1), pl.program_id(1))
    pad_ref[0:POOL_HALO, :] = jnp.zeros((POOL_HALO, cg), F32)
    pad_ref[s + POOL_HALO:s + 2 * POOL_HALO, :] = jnp.zeros((POOL_HALO, cg), F32)
    pad_ref[POOL_HALO:s + POOL_HALO, :] = u_ref[...]
    ii = lax.broadcasted_iota(I32, (t, t + 2 * POOL_HALO), 0)
    jj = lax.broadcasted_iota(I32, (t, t + 2 * POOL_HALO), 1)
    dlt = jj - ii - POOL_HALO
    band = jnp.where(dlt >= -half, jnp.where(dlt <= half - 1, 1.0, 0.0), 0.0).astype(BF16)
    wb = w_ref[...].astype(BF16)
    scale = sc_ref[...]

    def body(ti, carry):
        r0 = pl.multiple_of(ti * t, t)
        seg = pad_ref[pl.ds(r0, t + 2 * POOL_HALO), :]
        win = jnp.dot(band, seg.astype(BF16), preferred_element_type=F32)
        gi = r0 + lax.broadcasted_iota(I32, (t, cg), 0)
        lo_i = jnp.maximum(gi - half, 0)
        hi_i = jnp.minimum(gi + half - 1, s - 1)
        cnt = (hi_i - lo_i + 1).astype(F32)
        dmean = win / cnt - seg[POOL_HALO:POOL_HALO + t, :]
        y = jnp.dot(dmean.astype(BF16), wb, preferred_element_type=F32) * scale
        o_ref[pl.ds(r0, t), :] = y.astype(o_ref.dtype)
        return carry

    lax.fori_loop(0, s // t, body, 0)


def _pool_mixer(u_ab, pool_w, pool_scale3, l, col_block0):
    b, s, _ = u_ab.shape
    groups, cg = pool_w.shape[1], pool_w.shape[2]
    return pl.pallas_call(
        _pool_kernel,
        grid=(b, groups),
        in_specs=[
            pl.BlockSpec((None, s, cg), lambda bi, g: (bi, 0, col_block0 + g)),
            pl.BlockSpec((None, None, cg, cg), lambda bi, g: (l, g, 0, 0)),
            pl.BlockSpec((None, 1, cg), lambda bi, g: (l, 0, g)),
        ],
        out_specs=pl.BlockSpec((None, s, cg), lambda bi, g: (bi, 0, g)),
        out_shape=jax.ShapeDtypeStruct((b, s, groups * cg), BF16),
        scratch_shapes=[pltpu.VMEM((s + 2 * POOL_HALO, cg), F32)],
        compiler_params=_cparams(("arbitrary", "arbitrary")),
        name="pool_mixer",
    )(u_ab, pool_w, pool_scale3)


def _bucket(rel):
    nb = N_BUCKETS // 2
    max_exact = nb // 2
    n = jnp.abs(rel)
    nf = jnp.maximum(n, 1).astype(F32)
    large = max_exact + (jnp.log(nf / max_exact) / math.log(MAX_DISTANCE / max_exact)
                         * (nb - max_exact)).astype(I32)
    large = jnp.minimum(large, nb - 1)
    return jnp.where(rel > 0, nb, 0) + jnp.where(n < max_exact, n, large)


def _bias_tile(tab_ref, h, rel):
    bucket = _bucket(rel)
    val = jnp.full(rel.shape, tab_ref[h], F32)
    for j in range(1, N_BUCKETS):
        val = jnp.where(bucket == j, tab_ref[j * N_HEADS + h], val)
    return val * LOG2E


def _near_bias_kernel(slotj_ref, nnear_ref, tabt_ref, pq_ref, pk_ref, o_ref):
    del slotj_ref
    used = jnp.where(pl.program_id(1) < nnear_ref[pl.program_id(0)], 1.0, 0.0)
    bucket = _bucket(pk_ref[...] - pq_ref[...])
    tq, tk = bucket.shape
    for h in range(N_HEADS):
        row = jnp.broadcast_to(tabt_ref[h:h + 1, :], (tq, LANES))
        cols = [jnp.take_along_axis(row, bucket[:, c * LANES:(c + 1) * LANES], axis=1)
                for c in range(tk // LANES)]
        o_ref[h] = jnp.concatenate(cols, axis=1) * used


def _near_bias_table(rel_bias, positions, slotj, nnear):
    s = positions.shape[0]
    nq = s // ATT_TQ
    tabt = jnp.zeros((N_HEADS, LANES), F32).at[:, :N_BUCKETS].set(rel_bias.T * LOG2E)
    grid_spec = pltpu.PrefetchScalarGridSpec(
        num_scalar_prefetch=2,
        grid=(nq, ATT_NS),
        in_specs=[
            pl.BlockSpec((N_HEADS, LANES), lambda i, n, sj, nn: (0, 0)),
            pl.BlockSpec((ATT_TQ, 1), lambda i, n, sj, nn: (i, 0)),
            pl.BlockSpec((1, ATT_TK), lambda i, n, sj, nn: (0, sj[i * ATT_NS + n])),
        ],
        out_specs=pl.BlockSpec((N_HEADS, None, None, ATT_TQ, ATT_TK), lambda i, n, sj, nn: (0, i, n, 0, 0)),
    )
    return pl.pallas_call(
        _near_bias_kernel,
        grid_spec=grid_spec,
        out_shape=jax.ShapeDtypeStruct((N_HEADS, nq, ATT_NS, ATT_TQ, ATT_TK), F32),
        compiler_params=_cparams(("arbitrary", "arbitrary")),
        name="near_bias_table",
    )(slotj, nnear, tabt, positions.reshape(s, 1), positions.reshape(1, s))


def _attn_plan(positions, rel_bias):
    s = positions.shape[0]
    nq, nk = s // ATT_TQ, s // ATT_TK
    pq = positions.reshape(nq, ATT_TQ)
    pk = positions.reshape(nk, ATT_TK)
    rel_min = pk.min(axis=1)[None, :] - pq.max(axis=1)[:, None]
    rel_max = pk.max(axis=1)[None, :] - pq.min(axis=1)[:, None]
    cls = jnp.where(rel_min >= MAX_DISTANCE, 1, jnp.where(rel_max <= -MAX_DISTANCE, 0, 2)).astype(I32)
    near = cls == 2
    nnear = near.sum(axis=1).astype(I32)
    slotj = jnp.argsort(jnp.logical_not(near), axis=1, stable=True)[:, :ATT_NS].astype(I32)
    fits = jnp.all(nnear <= ATT_NS)
    nb = N_BUCKETS // 2
    ctab = jnp.stack([rel_bias[nb - 1], rel_bias[2 * nb - 1], jnp.zeros((N_HEADS,), F32)], axis=1) * LOG2E
    return cls.reshape(-1), slotj.reshape(-1), nnear, ctab.reshape(-1).astype(F32), fits


def _lambda(lq1_ref, lk1_ref, lq2_ref, lk2_ref, lam_init):
    return (jnp.exp(jnp.sum(lq1_ref[...] * lk1_ref[...], axis=-1, keepdims=True))
            - jnp.exp(jnp.sum(lq2_ref[...] * lk2_ref[...], axis=-1, keepdims=True)) + lam_init)


def _stack_maps(q):
    lane = lax.broadcasted_iota(I32, q.shape, 1)
    zero = jnp.zeros_like(q)
    return jnp.concatenate([jnp.where(lane < DK, q, zero), jnp.where(lane >= DK, q, zero)], axis=0)


def _attn_finish(o1, o2, lam, sg, lam_init, dtype):
    o = o1 - lam * o2
    ms = jnp.mean(o * o, axis=-1, keepdims=True)
    y = o * lax.rsqrt(ms + EPS) * sg
    return (y * (1.0 - lam_init)).astype(dtype)


def _attn_kernel(cls_ref, slotj_ref, ctab_ref, lq1_ref, lk1_ref, lq2_ref, lk2_ref,
                 q_ref, k_ref, v_ref, qn_ref, kn_ref, nbp0_ref, nbp1_ref, nba0_ref, nba1_ref, nbb0_ref, nbb1_ref,
                 sg_ref, o_ref, sa_scr, sb_scr, ma_scr, mb_scr, vaug_scr, *, lam_init):
    bi = pl.program_id(0)
    h = pl.program_id(1)
    ip = pl.program_id(2)
    n_b = pl.num_programs(0)
    n_ip = pl.num_programs(2)
    tq = ATT_TQ
    s = k_ref.shape[0]
    nk = s // ATT_TK
    lam = _lambda(lq1_ref, lk1_ref, lq2_ref, lk2_ref, lam_init)
    sg = sg_ref[...]

    def lane_tile_max(blk):
        out = blk[:, :LANES]
        for c in range(1, blk.shape[1] // LANES):
            out = jnp.maximum(out, blk[:, c * LANES:(c + 1) * LANES])
        return out

    def scores(t, hh, qr, kr, s_scr, m_scr, nb_refs):
        qs = _stack_maps(qr[pl.ds(pl.multiple_of(t * tq, tq), tq), :])
        mrow = None
        for j in range(nk):
            c = cls_ref[t * nk + j]
            sc = lax.dot_general(qs, kr[j * ATT_TK:(j + 1) * ATT_TK, :], (((1,), (1,)), ((), ())),
                                 preferred_element_type=F32) + ctab_ref[hh * 3 + c]
            s_scr[j] = sc
            cm = lane_tile_max(sc) + jnp.where(c == 2, -1e30, 0.0)
            mrow = cm if mrow is None else jnp.maximum(mrow, cm)
        for n, nb_ref in enumerate(nb_refs):
            j = slotj_ref[t * ATT_NS + n]
            bias = nb_ref[...]
            top = s_scr[j, 0:tq, :] + bias
            bot = s_scr[j, tq:2 * tq, :] + bias
            s_scr[j, 0:tq, :] = top
            s_scr[j, tq:2 * tq, :] = bot
            mrow = jnp.maximum(mrow, jnp.concatenate([lane_tile_max(top), lane_tile_max(bot)], axis=0))
        m_scr[...] = mrow

    def outputs(s_scr, m_scr, row0):
        m = jnp.max(m_scr[...], axis=-1, keepdims=True)
        acc = None
        for j in range(nk):
            e = jnp.exp2(s_scr[j] - m).astype(BF16)
            part = jnp.dot(e, vaug_scr[j * ATT_TK:(j + 1) * ATT_TK, :], preferred_element_type=F32)
            acc = part if acc is None else acc + part
        o1 = acc[:tq, :DV] / acc[:tq, DV:DV + 1]
        o2 = acc[tq:, :DV] / acc[tq:, DV:DV + 1]
        o_ref[row0:row0 + tq, :] = _attn_finish(o1, o2, lam, sg, lam_init, o_ref.dtype)

    @pl.when(ip == 0)
    def _():
        vaug_scr[:, :DV] = v_ref[...]
        lane = lax.broadcasted_iota(I32, (s, DV), 1)
        vaug_scr[:, DV:] = jnp.where(lane == 0, 1.0, 0.0).astype(BF16)

    @pl.when(jnp.logical_and(jnp.logical_and(bi == 0, h == 0), ip == 0))
    def _():
        scores(0, h, q_ref, k_ref, sa_scr, ma_scr, (nbp0_ref, nbp1_ref))

    last = ip == n_ip - 1
    _, h_next = _next_head(bi, h, last, n_b)
    t_next = jnp.where(last, 0, 2 * ip + 2)
    scores(2 * ip + 1, h, q_ref, k_ref, sb_scr, mb_scr, (nbb0_ref, nbb1_ref))
    outputs(sa_scr, ma_scr, 0)
    scores(t_next, h_next, qn_ref, kn_ref, sa_scr, ma_scr, (nba0_ref, nba1_ref))
    outputs(sb_scr, mb_scr, tq)


def _next_head(bi, h, last, n_b):
    lin = jnp.minimum(bi * N_HEADS + h + jnp.where(last, 1, 0), n_b * N_HEADS - 1)
    return lin // N_HEADS, lin % N_HEADS


def _attn_fast(qkv, nbt, plan, lam4, sub_g3, l):
    b, s, _ = qkv.shape
    cls, slotj, _, ctab, _ = plan
    nq = s // ATT_TQ
    n_ip = nq // 2
    assert nq % 2 == 0 and ATT_NS == 2
    lam_init = 0.8 - 0.6 * math.exp(-0.3 * l)
    smem = pl.BlockSpec(memory_space=pltpu.SMEM)
    lam_specs = [pl.BlockSpec((None, 1, DK), lambda bi, h, ip: (l, 0, 0)) for _ in range(4)]
    nb_tile = (None, None, None, ATT_TQ, ATT_TK)

    def nxt(bi, h, ip):
        return _next_head(bi, h, ip == n_ip - 1, b)

    def nba_map(n):
        def index_map(bi, h, ip):
            return (nxt(bi, h, ip)[1], jnp.where(ip == n_ip - 1, 0, 2 * ip + 2), n, 0, 0)
        return index_map

    nb_specs = ([pl.BlockSpec(nb_tile, lambda bi, h, ip, n=n: (0, 0, n, 0, 0)) for n in range(ATT_NS)]
                + [pl.BlockSpec(nb_tile, nba_map(n)) for n in range(ATT_NS)]
                + [pl.BlockSpec(nb_tile, lambda bi, h, ip, n=n: (h, 2 * ip + 1, n, 0, 0)) for n in range(ATT_NS)])
    return pl.pallas_call(
        functools.partial(_attn_kernel, lam_init=lam_init),
        grid=(b, N_HEADS, n_ip),
        in_specs=[smem, smem, smem] + lam_specs + [
            pl.BlockSpec((None, s, 2 * DK), lambda bi, h, ip: (bi, 0, h)),
            pl.BlockSpec((None, s, 2 * DK), lambda bi, h, ip: (bi, 0, N_HEADS + h)),
            pl.BlockSpec((None, s, DV), lambda bi, h, ip: (bi, 0, 2 * N_HEADS + h)),
            pl.BlockSpec((None, s, 2 * DK), lambda bi, h, ip: (nxt(bi, h, ip)[0], 0, nxt(bi, h, ip)[1])),
            pl.BlockSpec((None, s, 2 * DK), lambda bi, h, ip: (nxt(bi, h, ip)[0], 0, N_HEADS + nxt(bi, h, ip)[1])),
        ] + nb_specs + [pl.BlockSpec((None, 1, DV), lambda bi, h, ip: (l, 0, 0))],
        out_specs=pl.BlockSpec((None, 2 * ATT_TQ, DV), lambda bi, h, ip: (bi, ip, h)),
        out_shape=jax.ShapeDtypeStruct((b, s, N_HEADS * DV), BF16),
        scratch_shapes=[pltpu.VMEM((s // ATT_TK, 2 * ATT_TQ, ATT_TK), F32),
                        pltpu.VMEM((s // ATT_TK, 2 * ATT_TQ, ATT_TK), F32),
                        pltpu.VMEM((2 * ATT_TQ, LANES), F32),
                        pltpu.VMEM((2 * ATT_TQ, LANES), F32),
                        pltpu.VMEM((s, 2 * DV), BF16)],
        compiler_params=_cparams(("arbitrary", "arbitrary", "arbitrary"), 56),
        name="diff_attention",
    )(cls, slotj, ctab, *lam4, qkv, qkv, qkv, qkv, qkv, nbt, nbt, nbt, nbt, nbt, nbt, sub_g3)


def _attn_any_kernel(tab_ref, lq1_ref, lk1_ref, lq2_ref, lk2_ref, q_ref, k_ref, v_ref, pq_ref, pk_ref,
                     sg_ref, o_ref, *, lam_init):
    h = pl.program_id(1)
    tq = q_ref.shape[0]
    bias = _bias_tile(tab_ref, h, pk_ref[...] - pq_ref[...])
    sc = lax.dot_general(_stack_maps(q_ref[...]), k_ref[...], (((1,), (1,)), ((), ())),
                         preferred_element_type=F32)
    v = v_ref[...]

    def one_map(sm):
        sm = sm + bias
        e = jnp.exp2(sm - jnp.max(sm, axis=-1, keepdims=True))
        den = jnp.sum(e, axis=-1, keepdims=True)
        return jnp.dot(e.astype(BF16), v, preferred_element_type=F32) / den

    lam = _lambda(lq1_ref, lk1_ref, lq2_ref, lk2_ref, lam_init)
    o_ref[...] = _attn_finish(one_map(sc[:tq]), one_map(sc[tq:]), lam, sg_ref[...], lam_init, o_ref.dtype)


def _attn_any(qkv, rel_bias, positions, lam4, sub_g3, l):
    b, s, _ = qkv.shape
    tq = 128
    lam_init = 0.8 - 0.6 * math.exp(-0.3 * l)
    lam_specs = [pl.BlockSpec((None, 1, DK), lambda bi, h, i: (l, 0, 0)) for _ in range(4)]
    return pl.pallas_call(
        functools.partial(_attn_any_kernel, lam_init=lam_init),
        grid=(b, N_HEADS, s // tq),
        in_specs=[pl.BlockSpec(memory_space=pltpu.SMEM)] + lam_specs + [
            pl.BlockSpec((None, tq, 2 * DK), lambda bi, h, i: (bi, i, h)),
            pl.BlockSpec((None, s, 2 * DK), lambda bi, h, i: (bi, 0, N_HEADS + h)),
            pl.BlockSpec((None, s, DV), lambda bi, h, i: (bi, 0, 2 * N_HEADS + h)),
            pl.BlockSpec((tq, 1), lambda bi, h, i: (i, 0)),
            pl.BlockSpec((1, s), lambda bi, h, i: (0, 0)),
            pl.BlockSpec((None, 1, DV), lambda bi, h, i: (l, 0, 0)),
        ],
        out_specs=pl.BlockSpec((None, tq, DV), lambda bi, h, i: (bi, i, h)),
        out_shape=jax.ShapeDtypeStruct((b, s, N_HEADS * DV), BF16),
        compiler_params=_cparams(("arbitrary", "arbitrary", "arbitrary"), 48),
        name="diff_attention_any",
    )(rel_bias.reshape(-1), *lam4, qkv, qkv, qkv, positions.reshape(s, 1), positions.reshape(1, s), sub_g3)


def _diff_attention(qkv, nbt, plan, rel_bias, positions, lam4, sub_g3, l):
    return lax.cond(plan[4],
                    lambda: _attn_fast(qkv, nbt, plan, lam4, sub_g3, l),
                    lambda: _attn_any(qkv, rel_bias, positions, lam4, sub_g3, l))


def _post_mix_kernel(x_ref, mo_ref, g1_ref, ng_ref, sc_ref, sh_ref, wr_ref, x1_ref, h_ref, lg_ref):
    x1 = x_ref[...] + g1_ref[...] * mo_ref[...].astype(F32)
    x1_ref[...] = x1
    ms = jnp.mean(x1 * x1, axis=-1, keepdims=True)
    h = x1 * lax.rsqrt(ms + EPS) * ng_ref[...]
    h = h * (1.0 + sc_ref[...]) + sh_ref[...]
    d = h.shape[1]
    wr = wr_ref[...]
    h_hi = h.astype(BF16)
    h_lo = (h - h_hi.astype(F32)).astype(BF16)
    w_hi = wr.astype(BF16)
    w_lo = (wr - w_hi.astype(F32)).astype(BF16)
    lg = (jnp.dot(h_hi, w_hi, preferred_element_type=F32) + jnp.dot(h_lo, w_hi, preferred_element_type=F32)
          + jnp.dot(h_hi, w_lo, preferred_element_type=F32))
    lane = lax.broadcasted_iota(I32, lg.shape, 1)
    valid = lane < N_EXPERTS
    lgm = jnp.where(valid, lg, -1e30)
    ex = jnp.where(valid, jnp.exp(lgm - jnp.max(lgm, axis=-1, keepdims=True)), 0.0)
    aff = ex / jnp.sum(ex, axis=-1, keepdims=True)
    h_ref[:, :d] = h
    h_ref[:, d:] = aff
    lg_ref[...] = aff


def _post_mix(x, mixo, modr, norm2_g3, w_router_p, l):
    b, s, d = x.shape
    tm = 512
    mspec = lambda idx: pl.BlockSpec((None, None, None, 1, d), lambda bi, i: (l, bi, idx, 0, 0))
    return pl.pallas_call(
        _post_mix_kernel,
        grid=(b, s // tm),
        in_specs=[
            pl.BlockSpec((None, tm, d), lambda bi, i: (bi, i, 0)),
            pl.BlockSpec((None, tm, d), lambda bi, i: (bi, i, 0)),
            mspec(2),
            pl.BlockSpec((None, 1, d), lambda bi, i: (l, 0, 0)),
            mspec(4),
            mspec(3),
            pl.BlockSpec((None, d, LANES), lambda bi, i: (l, 0, 0)),
        ],
        out_specs=[
            pl.BlockSpec((None, tm, d), lambda bi, i: (bi, i, 0)),
            pl.BlockSpec((None, tm, d + LANES), lambda bi, i: (bi, i, 0)),
            pl.BlockSpec((None, tm, LANES), lambda bi, i: (bi, i, 0)),
        ],
        out_shape=[
            jax.ShapeDtypeStruct((b, s, d), F32),
            jax.ShapeDtypeStruct((b, s, d + LANES), F32),
            jax.ShapeDtypeStruct((b, s, LANES), F32),
        ],
        compiler_params=_cparams(("arbitrary", "arbitrary")),
        name="post_mix",
    )(x, mixo, modr, norm2_g3, modr, modr, w_router_p)


def _cumsum_lanes(x01):
    rows, n = x01.shape
    blk = MOE_TS
    ii = lax.broadcasted_iota(I32, (blk, blk), 0)
    jj = lax.broadcasted_iota(I32, (blk, blk), 1)
    tri = jnp.where(ii <= jj, 1.0, 0.0).astype(BF16)
    xb = x01.astype(BF16)
    carry = jnp.zeros((rows, 1), F32)
    outs, totals = [], [carry]
    for c in range(n // blk):
        part = jnp.dot(xb[:, c * blk:(c + 1) * blk], tri, preferred_element_type=F32) + carry
        outs.append(part)
        carry = part[:, blk - 1:blk]
        totals.append(carry)
    return jnp.concatenate(outs, axis=1), totals


def _route_kernel(aff_ref, idx_ref, slott_ref, lo_ref, cs_scr, *, cap):
    at = aff_ref[...].T[:N_EXPERTS, :]
    keys = lax.bitcast_convert_type(at, I32)

    def body(i, prefix):
        cand = prefix | jnp.left_shift(jnp.int32(1), 30 - i)
        cnt = jnp.sum(jnp.where(keys >= cand, 1.0, 0.0), axis=1, keepdims=True)
        return jnp.where(cnt >= cap, cand, prefix)

    thr = lax.fori_loop(0, 31, body, jnp.zeros((N_EXPERTS, 1), I32))
    gt = jnp.where(keys > thr, 1.0, 0.0)
    eq = jnp.where(keys == thr, 1.0, 0.0)
    need = cap - jnp.sum(gt, axis=1, keepdims=True)
    take = eq * jnp.where(_cumsum_lanes(eq)[0] <= need, 1.0, 0.0)
    sel = gt + take
    cs, totals = _cumsum_lanes(sel)
    slot = jnp.where(sel > 0.5, cs - 1.0, -1.0).astype(I32)
    s = slot.shape[1]
    pad = jnp.full((LANES - N_EXPERTS, s), -1, I32)
    slott_ref[...] = jnp.concatenate([slot, pad], axis=0).T

    lane_e = lax.broadcasted_iota(I32, (N_EXPERTS, LANES), 1)
    lo = jnp.zeros((N_EXPERTS, LANES), F32)
    for j, tot in enumerate(totals):
        lo = jnp.where(lane_e == j, tot, lo)
    lo_ref[...] = lo.astype(I32)

    cs_scr[...] = cs
    lane = lax.broadcasted_iota(I32, (cap, LANES), 1)
    rows = 128

    def one_expert(e, idxt):
        row = cs_scr[pl.ds(e, 1), :]
        cols = []
        for c0 in range(0, cap, rows):
            cio = (lax.broadcasted_iota(I32, (rows, s), 0) + c0).astype(F32)
            cols.append(jnp.sum(jnp.where(row <= cio, 1.0, 0.0), axis=1, keepdims=True))
        return jnp.where(lane == e, jnp.concatenate(cols, axis=0), idxt)

    idxt = lax.fori_loop(0, N_EXPERTS, one_expert, jnp.zeros((cap, LANES), F32))
    idx_ref[...] = idxt.T[:N_EXPERTS, :].astype(I32)


def _route(aff, cap):
    b, s, _ = aff.shape
    return pl.pallas_call(
        functools.partial(_route_kernel, cap=cap),
        grid=(b,),
        in_specs=[pl.BlockSpec((None, s, LANES), lambda bi: (bi, 0, 0))],
        out_specs=[
            pl.BlockSpec((None, N_EXPERTS, cap), lambda bi: (bi, 0, 0)),
            pl.BlockSpec((None, s, LANES), lambda bi: (bi, 0, 0)),
            pl.BlockSpec((None, N_EXPERTS, LANES), lambda bi: (bi, 0, 0)),
        ],
        out_shape=[
            jax.ShapeDtypeStruct((b, N_EXPERTS, cap), I32),
            jax.ShapeDtypeStruct((b, s, LANES), I32),
            jax.ShapeDtypeStruct((b, N_EXPERTS, LANES), I32),
        ],
        scratch_shapes=[pltpu.VMEM((N_EXPERTS, s), F32)],
        compiler_params=_cparams(("arbitrary",), 48),
        name="route",
    )(aff)


def _expert_kernel(idx_ref, h_ref, w1_ref, w3_ref, w2_ref, y_ref, rows_ref, xb_ref, acc_ref, gate_ref, sem):
    e = pl.program_id(0)
    f = pl.program_id(1)
    ne = pl.num_programs(0)
    nf = pl.num_programs(1)
    nb, cap, d = y_ref.shape
    n_rows = nb * cap
    per_step = n_rows // MOE_NF

    steps_per_batch = cap // per_step
    assert steps_per_batch & (steps_per_batch - 1) == 0 and steps_per_batch * per_step == cap

    def wait_all_rows():
        for ff in range(MOE_NF):
            pltpu.make_async_copy(h_ref.at[0, pl.ds(0, per_step), :], rows_ref.at[ff], sem).wait()

    @pl.when(jnp.logical_and(e == 0, f == 0))
    def _():
        for ff in range(MOE_NF):
            src = (ff // steps_per_batch) * N_EXPERTS * cap + (ff % steps_per_batch) * per_step

            def body(r, carry, ff=ff, src=src):
                pltpu.make_async_copy(h_ref.at[ff // steps_per_batch, pl.ds(idx_ref[src + r], 1), :],
                                      rows_ref.at[ff, pl.ds(r, 1), :], sem).start()
                return carry
            lax.fori_loop(0, per_step, body, 0)

    @pl.when(f == 0)
    def _():
        wait_all_rows()
        lane = lax.broadcasted_iota(I32, (per_step, LANES), 1)
        for ff in range(MOE_NF):
            rs = slice(ff * per_step, (ff + 1) * per_step)
            xb_ref[rs, :] = rows_ref[ff, :, :d].astype(BF16)
            gate_ref[rs, :] = jnp.sum(jnp.where(lane == e, rows_ref[ff, :, d:], 0.0), axis=1, keepdims=True)
        acc_ref[...] = jnp.zeros(acc_ref.shape, F32)

    e_next = jnp.minimum(e + 1, ne - 1)
    bi = lax.shift_right_logical(f, steps_per_batch.bit_length() - 1)
    c0 = (f & (steps_per_batch - 1)) * per_step
    src0 = (bi * N_EXPERTS + e_next) * cap + c0
    for r in range(per_step):
        pltpu.make_async_copy(h_ref.at[bi, pl.ds(idx_ref[src0 + r], 1), :],
                              rows_ref.at[f, pl.ds(r, 1), :], sem).start(priority=r % 2)

    xg = xb_ref[...]
    a = jnp.dot(xg, w1_ref[...].astype(BF16), preferred_element_type=F32)
    g = jnp.dot(xg, w3_ref[...].astype(BF16), preferred_element_type=F32)
    act = (_silu(a) * g).astype(BF16)
    acc_ref[...] += jnp.dot(act, w2_ref[...].astype(BF16), preferred_element_type=F32)

    @pl.when(f == nf - 1)
    def _():
        y_ref[...] = (acc_ref[...] * gate_ref[...]).reshape(nb, cap, d).astype(y_ref.dtype)

    @pl.when(jnp.logical_and(e == ne - 1, f == nf - 1))
    def _():
        wait_all_rows()


def _experts(idx, h2, w1, w3, w2, l, cap):
    b, s, dext = h2.shape
    d = dext - LANES
    e = w1.shape[1]
    ff = w1.shape[-1]
    fc = ff // MOE_NF
    grid_spec = pltpu.PrefetchScalarGridSpec(
        num_scalar_prefetch=1,
        grid=(e, MOE_NF),
        in_specs=[
            pl.BlockSpec(memory_space=pl.ANY),
            pl.BlockSpec((None, None, d, fc), lambda ei, f, ix: (l, ei, 0, f)),
            pl.BlockSpec((None, None, d, fc), lambda ei, f, ix: (l, ei, 0, f)),
            pl.BlockSpec((None, None, fc, d), lambda ei, f, ix: (l, ei, f, 0)),
        ],
        out_specs=pl.BlockSpec((b, None, cap, d), lambda ei, f, ix: (0, ei, 0, 0)),
        scratch_shapes=[pltpu.VMEM((MOE_NF, b * cap // MOE_NF, dext), F32), pltpu.VMEM((b * cap, d), BF16),
                        pltpu.VMEM((b * cap, d), F32), pltpu.VMEM((b * cap, 1), F32),
                        pltpu.SemaphoreType.DMA],
    )
    return pl.pallas_call(
        _expert_kernel,
        grid_spec=grid_spec,
        out_shape=jax.ShapeDtypeStruct((b, e, cap, d), BF16),
        compiler_params=_cparams(("arbitrary", "arbitrary"), 56),
        name="moe_experts",
    )(idx.reshape(-1), h2, w1, w3, w2)


def _scatter_kernel(lo_ref, slott_ref, y_ref, x1_ref, g2_ref, ng_ref, *rest, final_norm):
    if final_norm:
        o_ref, win_ref, full_ref, oh_ref, wsem, fsem = rest
    else:
        sc_ref, sh_ref, o_ref, h_ref, win_ref, full_ref, oh_ref, wsem, fsem = rest
    _scatter_body(lo_ref, slott_ref, y_ref, x1_ref, g2_ref, o_ref, win_ref, full_ref, oh_ref, wsem, fsem)
    xo = o_ref[...]
    xn = xo * lax.rsqrt(jnp.mean(xo * xo, axis=-1, keepdims=True) + EPS) * ng_ref[...]
    if final_norm:
        o_ref[...] = xn
    else:
        h_ref[...] = (xn * (1.0 + sc_ref[...]) + sh_ref[...]).astype(h_ref.dtype)


def _scatter_body(lo_ref, slott_ref, y_ref, x1_ref, g2_ref, o_ref, win_ref, full_ref, oh_ref, wsem, fsem):
    bi = pl.program_id(0)
    ti = pl.program_id(1)
    nb = pl.num_programs(0)
    nt = pl.num_programs(1)
    ts = slott_ref.shape[0]
    cap = y_ref.shape[2]
    step = bi * nt + ti
    buf = step & 1

    def lo_at(b_, t_, e):
        return lo_ref[(b_ * N_EXPERTS + e) * LANES + t_]

    def win_start(b_, t_, e):
        start = lax.shift_left(lax.shift_right_logical(lo_at(b_, t_, e), 4), 4)
        return pl.multiple_of(jnp.minimum(start, cap - MOE_WIN), 16)

    def win_copy(b_, t_, e, slot):
        return pltpu.make_async_copy(y_ref.at[b_, e, pl.ds(win_start(b_, t_, e), MOE_WIN), :],
                                     win_ref.at[slot, pl.ds(e * MOE_WIN, MOE_WIN), :], wsem.at[slot])

    @pl.when(step == 0)
    def _():
        for e in range(N_EXPERTS):
            win_copy(0, 0, e, 0).start()

    @pl.when(step + 1 < nb * nt)
    def _():
        wrap = ti + 1 == nt
        b_n = jnp.where(wrap, bi + 1, bi)
        t_n = jnp.where(wrap, 0, ti + 1)
        for e in range(N_EXPERTS):
            win_copy(b_n, t_n, e, 1 - buf).start()

    for e in range(N_EXPERTS):
        win_copy(bi, ti, e, buf).wait()

    lane = lax.broadcasted_iota(I32, (ts, LANES), 1)

    def slot_col(e):
        return jnp.sum(jnp.where(lane == e, slott_ref[...].astype(F32), 0.0), axis=1, keepdims=True).astype(I32)

    fits = None
    for e in range(N_EXPERTS):
        ok = lo_at(bi, ti + 1, e) - win_start(bi, ti, e) <= MOE_WIN
        fits = ok if fits is None else jnp.logical_and(fits, ok)

    @pl.when(fits)
    def _():
        widx = lax.broadcasted_iota(I32, (ts, MOE_WIN), 1)
        for e in range(N_EXPERTS):
            oh_ref[:, e * MOE_WIN:(e + 1) * MOE_WIN] = jnp.where(
                widx == slot_col(e) - win_start(bi, ti, e), 1.0, 0.0).astype(BF16)
        moe = jnp.dot(oh_ref[...], win_ref[buf], preferred_element_type=F32)
        o_ref[...] = x1_ref[...] + g2_ref[...] * moe

    @pl.when(jnp.logical_not(fits))
    def _():
        o_ref[...] = x1_ref[...]
        cidx = lax.broadcasted_iota(I32, (ts, cap), 1)

        def one_expert(e, carry):
            cp = pltpu.make_async_copy(y_ref.at[bi, e], full_ref, fsem)
            cp.start()
            cp.wait()
            onehot = jnp.where(cidx == slot_col(e), 1.0, 0.0).astype(BF16)
            o_ref[...] += g2_ref[...] * jnp.dot(onehot, full_ref[...], preferred_element_type=F32)
            return carry

        lax.fori_loop(0, N_EXPERTS, one_expert, 0)


def _scatter(lo, slott, y, x1, modr, l, next_g3, final_norm):
    b, s, d = x1.shape
    cap = y.shape[2]
    assert cap >= MOE_WIN and s // MOE_TS < LANES
    row_tile = pl.BlockSpec((None, MOE_TS, d), lambda bi, i, lo_: (bi, i, 0))
    ln = 0 if final_norm else l + 1
    in_specs = [
        pl.BlockSpec((None, MOE_TS, LANES), lambda bi, i, lo_: (bi, i, 0)),
        pl.BlockSpec(memory_space=pl.ANY),
        row_tile,
        pl.BlockSpec((None, None, None, 1, d), lambda bi, i, lo_: (l, bi, 5, 0, 0)),
        pl.BlockSpec((None, 1, d), lambda bi, i, lo_: (ln, 0, 0)),
    ]
    args = [lo.reshape(-1), slott, y, x1, modr, next_g3]
    out_specs = [row_tile]
    out_shape = [jax.ShapeDtypeStruct((b, s, d), F32)]
    if not final_norm:
        in_specs += [pl.BlockSpec((None, None, None, 1, d), lambda bi, i, lo_: (ln, bi, 1, 0, 0)),
                     pl.BlockSpec((None, None, None, 1, d), lambda bi, i, lo_: (ln, bi, 0, 0, 0))]
        args += [modr, modr]
        out_specs.append(row_tile)
        out_shape.append(jax.ShapeDtypeStruct((b, s, d), BF16))
    grid_spec = pltpu.PrefetchScalarGridSpec(
        num_scalar_prefetch=1,
        grid=(b, s // MOE_TS),
        in_specs=in_specs,
        out_specs=out_specs,
        scratch_shapes=[pltpu.VMEM((2, N_EXPERTS * MOE_WIN, d), BF16), pltpu.VMEM((cap, d), BF16),
                        pltpu.VMEM((MOE_TS, N_EXPERTS * MOE_WIN), BF16),
                        pltpu.SemaphoreType.DMA((2,)), pltpu.SemaphoreType.DMA],
    )
    outs = pl.pallas_call(
        functools.partial(_scatter_kernel, final_norm=final_norm),
        grid_spec=grid_spec,
        out_shape=out_shape,
        compiler_params=_cparams(("arbitrary", "arbitrary"), 58),
        name="moe_scatter",
    )(*args)
    return (outs[0], None) if final_norm else (outs[0], outs[1])


def kernel(x, c, positions, w_mod, b_mod, norm1_g, norm2_g, w_in, fnet_w, pool_w, pool_scale,
           lam_q1, lam_k1, lam_q2, lam_k2, sub_g, rel_bias, w_out, w_router, w1, w3, w2, final_g):
    b, s, d = x.shape
    depth = w_mod.shape[0]
    fnet_wd = fnet_w.shape[1] * fnet_w.shape[2]
    pool_wd = pool_w.shape[1] * pool_w.shape[2]
    ab_w = fnet_wd + pool_wd
    in_w = w_in.shape[-1]
    cap = EC_CAPACITY * s // N_EXPERTS

    c8 = jnp.zeros((8, d), F32).at[:b].set(c)
    mod = _modulation(c8, w_mod, b_mod.reshape(depth, 1, N_MOD * d))
    modr = mod[:, :b].reshape(depth, b, N_MOD, 1, d)

    norm1_g3 = norm1_g.reshape(depth, 1, d)
    norm2_g3 = norm2_g.reshape(depth, 1, d)
    pool_scale3 = pool_scale.reshape(depth, 1, pool_wd)
    sub_g3 = sub_g.reshape(depth, 1, DV)
    lam4 = [a.reshape(depth, 1, DK) for a in (lam_q1, lam_k1, lam_q2, lam_k2)]
    w_router_p = jnp.zeros((depth, d, LANES), F32).at[:, :, :N_EXPERTS].set(w_router)

    wd = _dft_matrix(s)
    ab = _fnet_weights(fnet_w, s)
    plan = _attn_plan(positions, rel_bias)
    nbt = _near_bias_table(rel_bias, positions, plan[1], plan[2])

    h1 = _norm_mod(x, norm1_g3, modr, 0, 0, 1)
    for l in range(depth):
        last = l == depth - 1
        h1 = h1.reshape(b * s, d)
        u_pool, za, zb = _project_ab(h1, w_in, ab, l, b, s, ab_w)
        qkv = _project([h1], w_in, l, ab_w, in_w - ab_w, BF16, tn=1024, lead_cols=N_HEADS * 2 * DK,
                       lead_scale=LOG2E * DK ** -0.5, name="proj_in_qkv").reshape(b, s, in_w - ab_w)

        ya = _dft_apply(wd, za, zb, b)
        yb = _pool_mixer(u_pool, pool_w, pool_scale3, l, 0)
        yc = _diff_attention(qkv, nbt, plan, rel_bias, positions, lam4, sub_g3, l)

        mixo = _project([ya.reshape(b * s, -1), yb.reshape(b * s, -1), yc.reshape(b * s, -1)],
                        w_out, l, 0, d, BF16, tn=1024, name="proj_out").reshape(b, s, d)
        x1, h2, aff = _post_mix(x, mixo, modr, norm2_g3, w_router_p, l)

        idx, slott, lo = _route(aff, cap)
        y = _experts(idx, h2, w1, w3, w2, l, cap)
        x, h1 = _scatter(lo, slott, y, x1, modr, l, final_g.reshape(1, 1, d) if last else norm1_g3,
                         final_norm=last)

    return x
```
